```python
import jax, jax.numpy as jnp
from jax import lax
import numpy as np

D_MODEL = 1024
BATCH = 16
SEQ = 2048
DEPTH = 2

CHUNK = 64
POOL_WIDTH = 1024
POOL_WINDOWS = (2, 4, 8, 16)
POOL_GROUPS = len(POOL_WINDOWS)
POOL_GROUP_WIDTH = POOL_WIDTH // POOL_GROUPS
HGRN_WIDTH = 1024
HGRN_EXPAND = 128
HGRN_HEADS = HGRN_WIDTH // HGRN_EXPAND
HGRN_HEAD_V = HGRN_WIDTH // HGRN_HEADS
MIX_WIDTH = POOL_WIDTH + HGRN_WIDTH
IN_WIDTH = 2 * POOL_WIDTH + 4 * HGRN_WIDTH
EPS = 1e-6

kernel_name = "hybrid_pool_hgrn2_adaln_block"


def rms_norm(x, w):
    xf = x.astype(jnp.float32)
    xf = xf * lax.rsqrt(jnp.mean(xf * xf, axis=-1, keepdims=True) + EPS)
    return xf * w.astype(jnp.float32)


def multiscale_pool(u, pool_w, pool_scale):
    B, S, _ = u.shape
    cs = jnp.cumsum(u, axis=1)
    count = jnp.arange(1, S + 1, dtype=jnp.float32)[None, :, None]
    outs = []
    for g, w in enumerate(POOL_WINDOWS):
        lo, hi = g * POOL_GROUP_WIDTH, (g + 1) * POOL_GROUP_WIDTH
        cs_g = cs[..., lo:hi]
        prev = jnp.pad(cs_g, ((0, 0), (w, 0), (0, 0)))[:, :S]
        mean = (cs_g - prev) / jnp.minimum(count, float(w))
        outs.append(mean - u[..., lo:hi])
    d = jnp.stack(outs, axis=2)
    y = jnp.einsum('bsgc,gcd->bsgd', d, pool_w.astype(jnp.float32))
    return y.reshape(B, S, POOL_WIDTH) * pool_scale.astype(jnp.float32)


def _to_chunks(t):
    B, S, H, D = t.shape
    return t.reshape(B, S // CHUNK, CHUNK, H, D).transpose(1, 0, 3, 2, 4)


def hgrn2_chunkwise(q, log_f, k, v):
    B, S, H, dk = q.shape
    dv = v.shape[-1]
    causal = jnp.tril(jnp.ones((CHUNK, CHUNK), dtype=bool))[:, :, None]

    def step(state, xs):
        qc, lfc, kc, vc = xs
        b = jnp.cumsum(lfc, axis=2)
        o_inter = jnp.einsum('bhtk,bhkv->bhtv', qc * jnp.exp(b), state)
        diff = b[:, :, :, None, :] - b[:, :, None, :, :]
        decay = jnp.exp(jnp.where(causal, diff, -jnp.inf))
        attn = jnp.einsum('bhtk,bhsk,bhtsk->bhts', qc, kc, decay)
        o_intra = jnp.einsum('bhts,bhsv->bhtv', attn, vc)
        b_last = b[:, :, -1, :]
        k_dec = kc * jnp.exp(b_last[:, :, None, :] - b)
        new_state = jnp.exp(b_last)[..., None] * state + jnp.einsum('bhsk,bhsv->bhkv', k_dec, vc)
        return new_state, o_inter + o_intra

    state0 = jnp.zeros((B, H, dk, dv), jnp.float32)
    _, o = lax.scan(step, state0, (_to_chunks(q), _to_chunks(log_f), _to_chunks(k), _to_chunks(v)))
    return o.transpose(1, 0, 3, 2, 4).reshape(B, S, H, dv)


def setup_inputs(seed: int = 0) -> dict:
    key = jax.random.key(seed)
    ks = jax.random.split(key, 12)
    f32 = jnp.float32
    x = jax.random.normal(ks[0], (BATCH, SEQ, D_MODEL), f32)
    c = jax.random.normal(ks[1], (BATCH, D_MODEL), f32)
    norm_pre_w = 1.0 + 0.05 * jax.random.normal(ks[2], (DEPTH, D_MODEL), f32)
    ada_w = 0.5 * D_MODEL ** -0.5 * jax.random.normal(ks[3], (DEPTH, D_MODEL, 3 * D_MODEL), f32)
    ada_b = 0.01 * jax.random.normal(ks[4], (DEPTH, 3 * D_MODEL), f32)
    w_in = D_MODEL ** -0.5 * jax.random.normal(ks[5], (DEPTH, D_MODEL, IN_WIDTH), f32)
    pool_w = POOL_GROUP_WIDTH ** -0.5 * jax.random.normal(
        ks[6], (DEPTH, POOL_GROUPS, POOL_GROUP_WIDTH, POOL_GROUP_WIDTH), f32)
    pool_scale = 1.0 + 0.05 * jax.random.normal(ks[7], (DEPTH, POOL_WIDTH), f32)
    hgrn_lower_bounds = 0.5 * jax.random.normal(ks[8], (DEPTH, HGRN_WIDTH), f32)
    hgrn_norm_w = 1.0 + 0.05 * jax.random.normal(ks[9], (DEPTH, HGRN_HEAD_V), f32)
    w_out = MIX_WIDTH ** -0.5 * jax.random.normal(ks[10], (DEPTH, MIX_WIDTH, D_MODEL), f32)
    norm_post_w = 1.0 + 0.05 * jax.random.normal(ks[11], (DEPTH, D_MODEL), f32)
    return {"x": x, "c": c, "norm_pre_w": norm_pre_w, "ada_w": ada_w, "ada_b": ada_b,
            "w_in": w_in, "pool_w": pool_w, "pool_scale": pool_scale,
            "hgrn_lower_bounds": hgrn_lower_bounds, "hgrn_norm_w": hgrn_norm_w,
            "w_out": w_out, "norm_post_w": norm_post_w}


def reference(x, c, norm_pre_w, ada_w, ada_b, w_in, pool_w, pool_scale,
              hgrn_lower_bounds, hgrn_norm_w, w_out, norm_post_w):
    B, S, D = x.shape
    f32 = jnp.float32
    lb_sm = jax.nn.softmax(hgrn_lower_bounds.astype(f32), axis=0)
    lower_bounds = jnp.cumsum(lb_sm, axis=0) - lb_sm[0]
    silu_c = jax.nn.silu(c.astype(f32))
    h_res = x.astype(f32)
    for l in range(DEPTH):
        mod = silu_c @ ada_w[l].astype(f32) + ada_b[l].astype(f32)
        shift, scale, gate = jnp.split(mod[:, None, :], 3, axis=-1)
        h = rms_norm(h_res, norm_pre_w[l]) * (1.0 + scale) + shift
        z = h @ w_in[l].astype(f32)
        u_pool, g_pool, q, f_pre, i_in, g_h = jnp.split(
            z, np.cumsum([POOL_WIDTH, POOL_WIDTH, HGRN_WIDTH, HGRN_WIDTH, HGRN_WIDTH]), axis=-1)
        pool_out = multiscale_pool(u_pool, pool_w[l], pool_scale[l]) * jax.nn.silu(g_pool)
        lb = lower_bounds[l]
        f = lb + (1.0 - lb) * jax.nn.sigmoid(f_pre)
        log_f = jnp.log(f)
        k = 1.0 - f
        heads = lambda t, d: t.reshape(B, S, HGRN_HEADS, d)
        o = hgrn2_chunkwise(heads(q, HGRN_EXPAND), heads(log_f, HGRN_EXPAND),
                            heads(k, HGRN_EXPAND), heads(i_in, HGRN_HEAD_V))
        o = rms_norm(o, hgrn_norm_w[l]) * jax.nn.silu(heads(g_h, HGRN_HEAD_V))
        hgrn_out = o.reshape(B, S, HGRN_WIDTH)
        y = jnp.concatenate([pool_out, hgrn_out], axis=-1) @ w_out[l].astype(f32)
        h_res = h_res + gate * rms_norm(y, norm_post_w[l])
    return h_res.astype(x.dtype)
```

```python
import functools

import numpy as np
import jax
import jax.numpy as jnp
from jax import lax
from jax.experimental import pallas as pl
from jax.experimental.pallas import tpu as pltpu

D_MODEL = 1024
CHUNK = 64
POOL_WIDTH = 1024
POOL_WINDOWS = (2, 4, 8, 16)
POOL_GROUP_WIDTH = POOL_WIDTH // len(POOL_WINDOWS)
HGRN_WIDTH = 1024
HEAD_DIM = 128
HGRN_HEADS = HGRN_WIDTH // HEAD_DIM
MIX_WIDTH = POOL_WIDTH + HGRN_WIDTH
IN_WIDTH = 2 * POOL_WIDTH + 4 * HGRN_WIDTH
EPS = 1e-6

SEQ_TILE = 256
POOL_TAIL = 16
NUM_LEVELS = 6
EXP_SETS = 2 + NUM_LEVELS
SPLIT = 3
VMEM_LIMIT_BYTES = 56 * 1024 * 1024

F32 = jnp.float32
BF16 = jnp.bfloat16


def _exponent_matrix():
    t = np.arange(CHUNK)[:, None]
    j = np.arange(CHUNK)[None, :]
    mats = [j <= t, j > t]
    for lvl in range(1, NUM_LEVELS + 1):
        m = 1 << (lvl - 1)
        ref = (t // (2 * m)) * (2 * m) + m - 1
        t_side = (t % (2 * m)) >= m
        mats.append(np.where(t_side, (j > ref) & (j <= t), (j > t) & (j <= ref)))
    w = np.concatenate(mats, axis=0).astype(np.float32)
    return np.concatenate([w] * SPLIT, axis=1)


def _level_map():
    t = np.arange(CHUNK)[:, None]
    s = np.arange(CHUNK)[None, :]
    x = t ^ s
    lvl = np.where(x > 0, np.floor(np.log2(np.maximum(x, 1))).astype(np.int32) + 1, 0)
    return np.where(s > t, -1, lvl).astype(np.int32)


def _split_bf16(a, pieces):
    out = []
    rest = a
    for _ in range(pieces):
        p = rest.astype(BF16)
        out.append(p)
        rest = rest - p.astype(F32)
    return out


def _sigmoid(a):
    return 1.0 / (1.0 + jnp.exp(-a))


def _dot(a, b):
    return jnp.dot(a, b, preferred_element_type=F32)


def _dot_nt(a, b):
    return lax.dot_general(a, b, (((1,), (1,)), ((), ())), preferred_element_type=F32)


def _lower_bounds_kernel(lb_ref, out_ref):
    depth = lb_ref.shape[0]
    rows = [lb_ref[l:l + 1, :] for l in range(depth)]
    mx = rows[0]
    for r in rows[1:]:
        mx = jnp.maximum(mx, r)
    ex = [jnp.exp(r - mx) for r in rows]
    den = ex[0]
    for e in ex[1:]:
        den = den + e
    sm = [e / den for e in ex]
    run = sm[0]
    out_ref[0:1, :] = run - sm[0]
    for l in range(1, depth):
        run = run + sm[l]
        out_ref[l:l + 1, :] = run - sm[0]


def _lower_bounds_call(hgrn_lower_bounds):
    return pl.pallas_call(
        _lower_bounds_kernel,
        out_shape=jax.ShapeDtypeStruct(hgrn_lower_bounds.shape, F32),
        name="lower_bounds",
    )(hgrn_lower_bounds.astype(F32))


def _adaln_kernel(c_ref, w_ref, b_ref, mod_ref):
    c = c_ref[...]
    sc = c * _sigmoid(c)
    w = w_ref[0]
    a_hi, a_lo = _split_bf16(sc, 2)
    w_hi, w_lo = _split_bf16(w, 2)
    acc = _dot(a_hi, w_hi) + _dot(a_hi, w_lo) + _dot(a_lo, w_hi)
    mod_ref[0] = acc + b_ref[0]


def _adaln_call(c, ada_w, ada_b):
    depth, d, d3 = ada_w.shape
    batch = c.shape[0]
    n_col = d3 // d
    return pl.pallas_call(
        _adaln_kernel,
        grid=(depth, n_col),
        in_specs=[
            pl.BlockSpec((batch, d), lambda l, j: (0, 0)),
            pl.BlockSpec((1, d, d), lambda l, j: (l, 0, j)),
            pl.BlockSpec((1, 1, d), lambda l, j: (l, 0, j)),
        ],
        out_specs=pl.BlockSpec((1, batch, d), lambda l, j: (l, 0, j)),
        out_shape=jax.ShapeDtypeStruct((depth, batch, d3), F32),
        name="adaln",
    )(c.astype(F32), ada_w.astype(F32), ada_b.astype(F32).reshape(depth, 1, d3))


def _layer_kernel(x_ref, mod_ref, npre_ref, w_in_ref, pool_w_ref, pool_scale_ref, lb_ref,
                  hnw_ref, w_out_ref, npost_ref, wexp_ref, lvl_ref, out_ref,
                  state_ref, tail_ref, uext_ref, q_ref, lf_ref, k_ref, v_ref, g_ref, mix_ref):
    ts = SEQ_TILE
    s_idx = pl.program_id(1)

    @pl.when(s_idx == 0)
    def _():
        state_ref[...] = jnp.zeros_like(state_ref)
        tail_ref[...] = jnp.zeros_like(tail_ref)

    x = x_ref[0]
    mod = mod_ref[0]
    shift = mod[:, 0:D_MODEL]
    scale = mod[:, D_MODEL:2 * D_MODEL]
    gate = mod[:, 2 * D_MODEL:3 * D_MODEL]

    ms = jnp.mean(x * x, axis=-1, keepdims=True)
    xn = x * lax.rsqrt(ms + EPS) * npre_ref[...]
    h = (xn * (1.0 + scale) + shift).astype(BF16)

    def in_proj(col):
        return _dot(h, w_in_ref[:, col * 1024:(col + 1) * 1024])

    u = in_proj(0)
    uext_ref[0:POOL_TAIL, :] = tail_ref[...]
    uext_ref[POOL_TAIL:POOL_TAIL + ts, :] = u
    tail_ref[...] = u[ts - POOL_TAIL:ts, :]
    g_pool = in_proj(1)
    pos = s_idx * ts + lax.broadcasted_iota(jnp.int32, (ts, 1), 0)
    for g, w in enumerate(POOL_WINDOWS):
        lo, hi = g * POOL_GROUP_WIDTH, (g + 1) * POOL_GROUP_WIDTH
        u_g = u[:, lo:hi]
        acc = u_g
        for j in range(1, w):
            acc = acc + uext_ref[POOL_TAIL - j:POOL_TAIL - j + ts, lo:hi]
        cnt = jnp.minimum(pos + 1, w).astype(F32)
        d = (acc / cnt - u_g).astype(BF16)
        y = _dot(d, pool_w_ref[g]) * pool_scale_ref[:, lo:hi]
        gp = g_pool[:, lo:hi]
        mix_ref[:, lo:hi] = (y * (gp * _sigmoid(gp))).astype(BF16)

    q_ref[...] = in_proj(2)
    lb = lb_ref[...]
    f = lb + (1.0 - lb) * _sigmoid(in_proj(3))
    lf_ref[...] = jnp.log(f)
    k_ref[...] = 1.0 - f
    v_ref[...] = in_proj(4)
    g_ref[...] = in_proj(5)
    lvl = lvl_ref[...]
    hnw = hnw_ref[...]

    def chunk_body(c, carry):
        r0 = pl.multiple_of(c * CHUNK, CHUNK)
        rows = pl.ds(r0, CHUNK)
        lf3 = jnp.concatenate(_split_bf16(lf_ref[rows, :], SPLIT), axis=0)
        e_all = jnp.exp(_dot(wexp_ref[...], lf3))
        for hd in range(HGRN_HEADS):
            hl = slice(hd * HEAD_DIM, (hd + 1) * HEAD_DIM)
            q = q_ref[rows, hl]
            k = k_ref[rows, hl]
            v = v_ref[rows, hl].astype(BF16)
            e_b = e_all[0:CHUNK, hl]
            state = state_ref[hd]
            o = _dot((q * e_b).astype(BF16), state.astype(BF16))
            attn = jnp.where(lvl == 0, _dot_nt(q.astype(BF16), k.astype(BF16)), 0.0)
            for level in range(1, NUM_LEVELS + 1):
                e_l = e_all[(1 + level) * CHUNK:(2 + level) * CHUNK, hl]
                a_l = _dot_nt((q * e_l).astype(BF16), (k * e_l).astype(BF16))
                attn = jnp.where(lvl == level, a_l, attn)
            o = o + _dot(attn.astype(BF16), v)
            k_dec_t = jnp.transpose(k * e_all[CHUNK:2 * CHUNK, hl]).astype(BF16)
            decay = jnp.transpose(jnp.broadcast_to(e_b[CHUNK - 1:CHUNK, :], (HEAD_DIM, HEAD_DIM)))
            state_ref[hd] = state * decay + _dot(k_dec_t, v)
            on = o * lax.rsqrt(jnp.mean(o * o, axis=-1, keepdims=True) + EPS) * hnw
            gh = g_ref[rows, hl]
            mix_ref[rows, POOL_WIDTH + hd * HEAD_DIM:POOL_WIDTH + (hd + 1) * HEAD_DIM] = (
                on * (gh * _sigmoid(gh))).astype(BF16)
        return carry

    lax.fori_loop(0, ts // CHUNK, chunk_body, 0)

    y = _dot(mix_ref[...], w_out_ref[...])
    yn = y * lax.rsqrt(jnp.mean(y * y, axis=-1, keepdims=True) + EPS) * npost_ref[...]
    out_ref[0] = x + gate * yn


def _layer_call(x, mod, npre, w_in, pool_w, pool_scale, lb, hnw, w_out, npost, wexp, lvl):
    batch, seq, d = x.shape
    ts = SEQ_TILE
    const2 = lambda b, s: (0, 0)
    const3 = lambda b, s: (0, 0, 0)
    once = pl.Buffered(1)
    return pl.pallas_call(
        _layer_kernel,
        grid=(batch, seq // ts),
        in_specs=[
            pl.BlockSpec((1, ts, d), lambda b, s: (b, s, 0)),
            pl.BlockSpec((1, 1, 3 * d), lambda b, s: (b, 0, 0)),
            pl.BlockSpec((1, d), const2),
            pl.BlockSpec((d, IN_WIDTH), const2, pipeline_mode=once),
            pl.BlockSpec(pool_w.shape, const3, pipeline_mode=once),
            pl.BlockSpec((1, POOL_WIDTH), const2),
            pl.BlockSpec((1, HGRN_WIDTH), const2),
            pl.BlockSpec((1, HEAD_DIM), const2),
            pl.BlockSpec((MIX_WIDTH, d), const2, pipeline_mode=once),
            pl.BlockSpec((1, d), const2),
            pl.BlockSpec(wexp.shape, const2),
            pl.BlockSpec(lvl.shape, const2),
        ],
        out_specs=pl.BlockSpec((1, ts, d), lambda b, s: (b, s, 0)),
        out_shape=jax.ShapeDtypeStruct(x.shape, F32),
        scratch_shapes=[
            pltpu.VMEM((HGRN_HEADS, HEAD_DIM, HEAD_DIM), F32),
            pltpu.VMEM((POOL_TAIL, POOL_WIDTH), F32),
            pltpu.VMEM((POOL_TAIL + ts, POOL_WIDTH), F32),
            pltpu.VMEM((ts, HGRN_WIDTH), F32),
            pltpu.VMEM((ts, HGRN_WIDTH), F32),
            pltpu.VMEM((ts, HGRN_WIDTH), F32),
            pltpu.VMEM((ts, HGRN_WIDTH), F32),
            pltpu.VMEM((ts, HGRN_WIDTH), F32),
            pltpu.VMEM((ts, MIX_WIDTH), BF16),
        ],
        compiler_params=pltpu.CompilerParams(
            dimension_semantics=("arbitrary", "arbitrary"),
            vmem_limit_bytes=VMEM_LIMIT_BYTES),
        name="hybrid_layer",
    )(x, mod, npre, w_in, pool_w, pool_scale, lb, hnw, w_out, npost, wexp, lvl)


def kernel(x, c, norm_pre_w, ada_w, ada_b, w_in, pool_w, pool_scale, hgrn_lower_bounds,
           hgrn_norm_w, w_out, norm_post_w):
    depth = ada_w.shape[0]
    batch = x.shape[0]
    lower = _lower_bounds_call(hgrn_lower_bounds)
    mod = _adaln_call(c, ada_w, ada_b)
    wexp = jnp.asarray(_exponent_matrix(), dtype=BF16)
    lvl = jnp.asarray(_level_map())
    h = x.astype(F32)
    for l in range(depth):
        h = _layer_call(
            h,
            mod[l].reshape(batch, 1, 3 * D_MODEL),
            norm_pre_w[l].astype(F32).reshape(1, D_MODEL),
            w_in[l].astype(BF16),
            pool_w[l].astype(BF16),
            pool_scale[l].astype(F32).reshape(1, POOL_WIDTH),
            lower[l].reshape(1, HGRN_WIDTH),
            hgrn_norm_w[l].astype(F32).reshape(1, HEAD_DIM),
            w_out[l].astype(BF16),
            norm_post_w[l].astype(F32).reshape(1, D_MODEL),
            wexp, lvl)
    return h.astype(x.dtype)
```

```python
import functools

import numpy as np
import jax
import jax.numpy as jnp
from jax import lax
from jax.experimental import pallas as pl
from jax.experimental.pallas import tpu as pltpu

D_MODEL = 1024
CHUNK = 64
POOL_WIDTH = 1024
POOL_WINDOWS = (2, 4, 8, 16)
POOL_GROUP_WIDTH = POOL_WIDTH // len(POOL_WINDOWS)
HGRN_WIDTH = 1024
HEAD_DIM = 128
HGRN_HEADS = HGRN_WIDTH // HEAD_DIM
MIX_WIDTH = POOL_WIDTH + HGRN_WIDTH
IN_WIDTH = 2 * POOL_WIDTH + 4 * HGRN_WIDTH
EPS = 1e-6

SEQ_TILE = 256
POOL_TAIL = 16
assert POOL_WINDOWS == tuple(2 << g for g in range(len(POOL_WINDOWS))) and POOL_TAIL >= POOL_WINDOWS[-1]
NUM_LEVELS = 6
EXP_SETS = 2 + NUM_LEVELS
SPLIT = 3
VMEM_LIMIT_BYTES = 56 * 1024 * 1024

SUBLANES = 8
FIRST_HALF_LEVEL = 4
assert (1 << (FIRST_HALF_LEVEL - 1)) == SUBLANES
LOG2E = 1.4426950408889634

F32 = jnp.float32
BF16 = jnp.bfloat16


def _t_side_groups(level):
    m = 1 << (level - 1)
    if m < SUBLANES:
        return None
    return [r for r in range(0, CHUNK, SUBLANES) if (r % (2 * m)) >= m]


def _exponent_matrix():
    t = np.arange(CHUNK)[:, None]
    j = np.arange(CHUNK)[None, :]
    mats = [j <= t, j > t]
    for lvl in range(1, NUM_LEVELS + 1):
        m = 1 << (lvl - 1)
        ref = (t // (2 * m)) * (2 * m) + m - 1
        t_side = (t % (2 * m)) >= m
        mats.append(np.where(t_side, (j > ref) & (j <= t), (j > t) & (j <= ref)))
    w = np.concatenate(mats, axis=0).astype(np.float32)
    return np.concatenate([w] * SPLIT, axis=1)


def _level_map():
    t = np.arange(CHUNK)[:, None]
    s = np.arange(CHUNK)[None, :]
    x = t ^ s
    lvl = np.where(x > 0, np.floor(np.log2(np.maximum(x, 1))).astype(np.int32) + 1, 0)
    return np.where(s > t, -1, lvl).astype(np.int32)


def _split_bf16(a, pieces):
    out = []
    rest = a
    for _ in range(pieces):
        p = rest.astype(BF16)
        out.append(p)
        rest = rest - p.astype(F32)
    return out


def _sigmoid(a):
    return 1.0 / (1.0 + jnp.exp(-a))


def _dot(a, b):
    return jnp.dot(a, b, preferred_element_type=F32)


def _dot_nt(a, b):
    return lax.dot_general(a, b, (((1,), (1,)), ((), ())), preferred_element_type=F32)


def _lower_bounds_kernel(lb_ref, out_ref):
    depth = lb_ref.shape[0]
    rows = [lb_ref[l:l + 1, :] for l in range(depth)]
    mx = rows[0]
    for r in rows[1:]:
        mx = jnp.maximum(mx, r)
    ex = [jnp.exp(r - mx) for r in rows]
    den = ex[0]
    for e in ex[1:]:
        den = den + e
    sm = [e / den for e in ex]
    run = sm[0]
    out_ref[0:1, :] = run - sm[0]
    for l in range(1, depth):
        run = run + sm[l]
        out_ref[l:l + 1, :] = run - sm[0]


def _lower_bounds_call(hgrn_lower_bounds):
    return pl.pallas_call(
        _lower_bounds_kernel,
        out_shape=jax.ShapeDtypeStruct(hgrn_lower_bounds.shape, F32),
        name="lower_bounds",
    )(hgrn_lower_bounds.astype(F32))


def _adaln_kernel(c_ref, w_ref, b_ref, mod_ref):
    c = c_ref[...]
    sc = c * _sigmoid(c)
    w = w_ref[0]
    a_hi, a_lo = _split_bf16(sc, 2)
    w_hi, w_lo = _split_bf16(w, 2)
    acc = _dot(a_hi, w_hi) + _dot(a_hi, w_lo) + _dot(a_lo, w_hi)
    mod_ref[0] = acc + b_ref[0]


def _adaln_call(c, ada_w, ada_b):
    depth, d, d3 = ada_w.shape
    batch = c.shape[0]
    n_col = d3 // d
    return pl.pallas_call(
        _adaln_kernel,
        grid=(depth, n_col),
        in_specs=[
            pl.BlockSpec((batch, d), lambda l, j: (0, 0)),
            pl.BlockSpec((1, d, d), lambda l, j: (l, 0, j)),
            pl.BlockSpec((1, 1, d), lambda l, j: (l, 0, j)),
        ],
        out_specs=pl.BlockSpec((1, batch, d), lambda l, j: (l, 0, j)),
        out_shape=jax.ShapeDtypeStruct((depth, batch, d3), F32),
        name="adaln",
    )(c.astype(F32), ada_w.astype(F32), ada_b.astype(F32).reshape(depth, 1, d3))


def _layer_kernel(x_ref, mod_ref, npre_ref, w_in_ref, pool_w_ref, pool_scale_ref, lb_ref,
                  hnw_ref, w_out_ref, npost_ref, wexp_ref, lvl_ref, out_ref,
                  state_ref, tail_ref, q_ref, lf_ref, k_ref, v_ref, g_ref, mix_ref,
                  e_ref, ql_ref, qh_ref, kl_ref, attn_ref):
    ts = SEQ_TILE
    s_idx = pl.program_id(1)

    @pl.when(s_idx == 0)
    def _():
        state_ref[...] = jnp.zeros_like(state_ref)
        tail_ref[...] = jnp.zeros_like(tail_ref)

    x = x_ref[0]
    mod = mod_ref[0]
    shift = mod[:, 0:D_MODEL]
    scale = mod[:, D_MODEL:2 * D_MODEL]
    gate = mod[:, 2 * D_MODEL:3 * D_MODEL]

    ms = jnp.mean(x * x, axis=-1, keepdims=True)
    xn = x * lax.rsqrt(ms + EPS) * npre_ref[...]
    h = (xn * (1.0 + scale) + shift).astype(BF16)

    def in_proj(col):
        return _dot(h, w_in_ref[:, col * 1024:(col + 1) * 1024])

    u = in_proj(0)
    part = jnp.concatenate([tail_ref[...], u], axis=0)
    tail_ref[...] = u[ts - POOL_TAIL:ts, :]
    win_sum = []
    for g, w in enumerate(POOL_WINDOWS):
        part = part[:, (POOL_GROUP_WIDTH if g else 0):]
        part = part + pltpu.roll(part, w // 2, 0)
        win_sum.append(part[POOL_TAIL:, 0:POOL_GROUP_WIDTH])
    g_pool = in_proj(1)
    pos = s_idx * ts + lax.broadcasted_iota(jnp.int32, (ts, 1), 0)
    for g, w in enumerate(POOL_WINDOWS):
        lo, hi = g * POOL_GROUP_WIDTH, (g + 1) * POOL_GROUP_WIDTH
        u_g = u[:, lo:hi]
        cnt = jnp.minimum(pos + 1, w).astype(F32)
        d = (win_sum[g] / cnt - u_g).astype(BF16)
        y = _dot(d, pool_w_ref[g]) * pool_scale_ref[:, lo:hi]
        gp = g_pool[:, lo:hi]
        mix_ref[:, lo:hi] = (y * (gp * _sigmoid(gp))).astype(BF16)

    q_ref[...] = in_proj(2)
    lb = lb_ref[...]
    f = lb + (1.0 - lb) * _sigmoid(in_proj(3))
    lf_ref[...] = jnp.log(f) * LOG2E
    k_ref[...] = 1.0 - f
    v_ref[...] = in_proj(4)
    g_ref[...] = in_proj(5)
    lvl = lvl_ref[...]
    hnw = hnw_ref[...]

    def chunk_body(c, carry):
        rows = slice(c * CHUNK, (c + 1) * CHUNK)
        heads = [slice(hd * HEAD_DIM, (hd + 1) * HEAD_DIM) for hd in range(HGRN_HEADS)]
        lf3 = jnp.concatenate(_split_bf16(lf_ref[rows, :], SPLIT), axis=0)
        e_ref[...] = jnp.exp2(_dot(wexp_ref[...], lf3))

        q_c = q_ref[rows, :]
        k_c = k_ref[rows, :]
        ql_ref[0] = q_c.astype(BF16)
        kl_ref[0] = k_c.astype(BF16)
        for level in range(1, NUM_LEVELS + 1):
            e0 = (1 + level) * CHUNK
            t_groups = _t_side_groups(level)
            if t_groups is None:
                e_l = e_ref[e0:e0 + CHUNK, :]
                ql_ref[level] = (q_c * e_l).astype(BF16)
                kl_ref[level] = (k_c * e_l).astype(BF16)
            else:
                scaled = lambda a, r: a[r:r + SUBLANES] * e_ref[e0 + r:e0 + r + SUBLANES, :]
                qh_ref[level - FIRST_HALF_LEVEL] = jnp.concatenate(
                    [scaled(q_c, r) for r in t_groups], axis=0).astype(BF16)
                kl_ref[level] = jnp.concatenate(
                    [k_c[r:r + SUBLANES] if r in t_groups else scaled(k_c, r)
                     for r in range(0, CHUNK, SUBLANES)], axis=0).astype(BF16)
        for hd, hl in enumerate(heads):
            attn = jnp.where(lvl == 0, _dot_nt(ql_ref[0, :, hl], kl_ref[0, :, hl]), 0.0)
            for level in range(1, FIRST_HALF_LEVEL):
                attn = jnp.where(lvl == level, _dot_nt(ql_ref[level, :, hl], kl_ref[level, :, hl]), attn)
            groups = [attn[r:r + SUBLANES] for r in range(0, CHUNK, SUBLANES)]
            for level in range(FIRST_HALF_LEVEL, NUM_LEVELS + 1):
                a_t = _dot_nt(qh_ref[level - FIRST_HALF_LEVEL, :, hl], kl_ref[level, :, hl])
                for i, r in enumerate(_t_side_groups(level)):
                    groups[r // SUBLANES] = jnp.where(
                        lvl[r:r + SUBLANES] == level, a_t[i * SUBLANES:(i + 1) * SUBLANES],
                        groups[r // SUBLANES])
            attn_ref[hd] = jnp.concatenate(groups, axis=0).astype(BF16)

        qe = (q_ref[rows, :] * e_ref[0:CHUNK, :]).astype(BF16)
        vb = v_ref[rows, :].astype(BF16)
        for hd, hl in enumerate(heads):
            o = _dot(qe[:, hl], state_ref[hd].astype(BF16)) + _dot(attn_ref[hd], vb[:, hl])
            on = o * lax.rsqrt(jnp.mean(o * o, axis=-1, keepdims=True) + EPS) * hnw
            gh = g_ref[rows, hl]
            mix_ref[rows, POOL_WIDTH + hd * HEAD_DIM:POOL_WIDTH + (hd + 1) * HEAD_DIM] = (
                on * (gh * _sigmoid(gh))).astype(BF16)

        k_dec = k_ref[rows, :] * e_ref[CHUNK:2 * CHUNK, :]
        e_last = e_ref[CHUNK - 1:CHUNK, :]
        for hd, hl in enumerate(heads):
            k_dec_t = jnp.transpose(k_dec[:, hl]).astype(BF16)
            decay = jnp.transpose(jnp.broadcast_to(e_last[:, hl], (HEAD_DIM, HEAD_DIM)))
            state_ref[hd] = state_ref[hd] * decay + _dot(k_dec_t, vb[:, hl])
        return carry

    for c in range(ts // CHUNK):
        chunk_body(c, 0)

    y = _dot(mix_ref[...], w_out_ref[...])
    yn = y * lax.rsqrt(jnp.mean(y * y, axis=-1, keepdims=True) + EPS) * npost_ref[...]
    out_ref[0] = x + gate * yn


def _layer_call(x, mod, npre, w_in, pool_w, pool_scale, lb, hnw, w_out, npost, wexp, lvl):
    batch, seq, d = x.shape
    ts = SEQ_TILE
    const2 = lambda b, s: (0, 0)
    const3 = lambda b, s: (0, 0, 0)
    once = pl.Buffered(1)
    return pl.pallas_call(
        _layer_kernel,
        grid=(batch, seq // ts),
        in_specs=[
            pl.BlockSpec((1, ts, d), lambda b, s: (b, s, 0)),
            pl.BlockSpec((1, 1, 3 * d), lambda b, s: (b, 0, 0)),
            pl.BlockSpec((1, d), const2),
            pl.BlockSpec((d, IN_WIDTH), const2, pipeline_mode=once),
            pl.BlockSpec(pool_w.shape, const3, pipeline_mode=once),
            pl.BlockSpec((1, POOL_WIDTH), const2),
            pl.BlockSpec((1, HGRN_WIDTH), const2),
            pl.BlockSpec((1, HEAD_DIM), const2),
            pl.BlockSpec((MIX_WIDTH, d), const2, pipeline_mode=once),
            pl.BlockSpec((1, d), const2),
            pl.BlockSpec(wexp.shape, const2),
            pl.BlockSpec(lvl.shape, const2),
        ],
        out_specs=pl.BlockSpec((1, ts, d), lambda b, s: (b, s, 0)),
        out_shape=jax.ShapeDtypeStruct(x.shape, F32),
        scratch_shapes=[
            pltpu.VMEM((HGRN_HEADS, HEAD_DIM, HEAD_DIM), F32),
            pltpu.VMEM((POOL_TAIL, POOL_WIDTH), F32),
            pltpu.VMEM((ts, HGRN_WIDTH), F32),
            pltpu.VMEM((ts, HGRN_WIDTH), F32),
            pltpu.VMEM((ts, HGRN_WIDTH), F32),
            pltpu.VMEM((ts, HGRN_WIDTH), F32),
            pltpu.VMEM((ts, HGRN_WIDTH), F32),
            pltpu.VMEM((ts, MIX_WIDTH), BF16),
            pltpu.VMEM((EXP_SETS * CHUNK, HGRN_WIDTH), F32),
            pltpu.VMEM((FIRST_HALF_LEVEL, CHUNK, HGRN_WIDTH), BF16),
            pltpu.VMEM((NUM_LEVELS + 1 - FIRST_HALF_LEVEL, CHUNK // 2, HGRN_WIDTH), BF16),
            pltpu.VMEM((NUM_LEVELS + 1, CHUNK, HGRN_WIDTH), BF16),
            pltpu.VMEM((HGRN_HEADS, CHUNK, CHUNK), BF16),
        ],
        compiler_params=pltpu.CompilerParams(
            dimension_semantics=("arbitrary", "arbitrary"),
            vmem_limit_bytes=VMEM_LIMIT_BYTES),
        name="hybrid_layer",
    )(x, mod, npre, w_in, pool_w, pool_scale, lb, hnw, w_out, npost, wexp, lvl)


def kernel(x, c, norm_pre_w, ada_w, ada_b, w_in, pool_w, pool_scale, hgrn_lower_bounds,
           hgrn_norm_w, w_out, norm_post_w):
    depth = ada_w.shape[0]
    batch = x.shape[0]
    lower = _lower_bounds_call(hgrn_lower_bounds)
    mod = _adaln_call(c, ada_w, ada_b)
    wexp = jnp.asarray(_exponent_matrix(), dtype=BF16)
    lvl = jnp.asarray(_level_map())
    h = x.astype(F32)
    for l in range(depth):
        h = _layer_call(
            h,
            mod[l].reshape(batch, 1, 3 * D_MODEL),
            norm_pre_w[l].astype(F32).reshape(1, D_MODEL),
            w_in[l].astype(BF16),
            pool_w[l].astype(BF16),
            pool_scale[l].astype(F32).reshape(1, POOL_WIDTH),
            lower[l].reshape(1, HGRN_WIDTH),
            hgrn_norm_w[l].astype(F32).reshape(1, HEAD_DIM),
            w_out[l].astype(BF16),
            norm_post_w[l].astype(F32).reshape(1, D_MODEL),
            wexp, lvl)
    return h.astype(x.dtype)
```

```python
import functools

import numpy as np
import jax
import jax.numpy as jnp
from jax import lax
from jax.experimental import pallas as pl
from jax.experimental.pallas import tpu as pltpu

D_MODEL = 1024
CHUNK = 64
POOL_WIDTH = 1024
POOL_WINDOWS = (2, 4, 8, 16)
POOL_GROUP_WIDTH = POOL_WIDTH // len(POOL_WINDOWS)
HGRN_WIDTH = 1024
HEAD_DIM = 128
HGRN_HEADS = HGRN_WIDTH // HEAD_DIM
MIX_WIDTH = POOL_WIDTH + HGRN_WIDTH
IN_WIDTH = 2 * POOL_WIDTH + 4 * HGRN_WIDTH
EPS = 1e-6

SEQ_TILE = 256
POOL_TAIL = 16
assert POOL_WINDOWS == tuple(2 << g for g in range(len(POOL_WINDOWS))) and POOL_TAIL >= POOL_WINDOWS[-1]
NUM_LEVELS = 6
SPLIT = 3
VMEM_LIMIT_BYTES = 56 * 1024 * 1024

SUBLANES = 8
FIRST_HALF_LEVEL = 4
assert (1 << (FIRST_HALF_LEVEL - 1)) == SUBLANES
MXU_LEVELS = tuple(range(2, FIRST_HALF_LEVEL))
MXU_SETS = 1 + len(MXU_LEVELS)
LOG2E = 1.4426950408889634

F32 = jnp.float32
BF16 = jnp.bfloat16


def _t_side_groups(level):
    m = 1 << (level - 1)
    if m < SUBLANES:
        return None
    return [r for r in range(0, CHUNK, SUBLANES) if (r % (2 * m)) >= m]


def _ref_row(level, row):
    m = 1 << (level - 1)
    return (row // (2 * m)) * (2 * m) + m - 1


def _exponent_matrix():
    t = np.arange(CHUNK)[:, None]
    j = np.arange(CHUNK)[None, :]
    mats = [j <= t]
    for lvl in MXU_LEVELS:
        m = 1 << (lvl - 1)
        ref = _ref_row(lvl, t)
        t_side = (t % (2 * m)) >= m
        mats.append(np.where(t_side, (j > ref) & (j <= t), (j > t) & (j <= ref)))
    w = np.concatenate(mats, axis=0).astype(np.float32)
    return np.concatenate([w] * SPLIT, axis=1)


def _level_map():
    t = np.arange(CHUNK)[:, None]
    s = np.arange(CHUNK)[None, :]
    x = t ^ s
    lvl = np.where(x > 0, np.floor(np.log2(np.maximum(x, 1))).astype(np.int32) + 1, 0)
    return np.where(s > t, -1, lvl).astype(np.int32)


def _split_bf16(a, pieces):
    out = []
    rest = a
    for _ in range(pieces):
        p = rest.astype(BF16)
        out.append(p)
        rest = rest - p.astype(F32)
    return out


def _sigmoid(a):
    return 1.0 / (1.0 + jnp.exp(-a))


def _dot(a, b):
    return jnp.dot(a, b, preferred_element_type=F32)


def _dot_nt(a, b):
    return lax.dot_general(a, b, (((1,), (1,)), ((), ())), preferred_element_type=F32)


def _lower_bounds_kernel(lb_ref, out_ref):
    depth = lb_ref.shape[0]
    rows = [lb_ref[l:l + 1, :] for l in range(depth)]
    mx = rows[0]
    for r in rows[1:]:
        mx = jnp.maximum(mx, r)
    ex = [jnp.exp(r - mx) for r in rows]
    den = ex[0]
    for e in ex[1:]:
        den = den + e
    sm = [e / den for e in ex]
    run = sm[0]
    out_ref[0:1, :] = run - sm[0]
    for l in range(1, depth):
        run = run + sm[l]
        out_ref[l:l + 1, :] = run - sm[0]


def _lower_bounds_call(hgrn_lower_bounds):
    return pl.pallas_call(
        _lower_bounds_kernel,
        out_shape=jax.ShapeDtypeStruct(hgrn_lower_bounds.shape, F32),
        name="lower_bounds",
    )(hgrn_lower_bounds.astype(F32))


def _adaln_kernel(c_ref, w_ref, b_ref, mod_ref):
    c = c_ref[...]
    sc = c * _sigmoid(c)
    w = w_ref[0]
    a_hi, a_lo = _split_bf16(sc, 2)
    w_hi, w_lo = _split_bf16(w, 2)
    acc = _dot(a_hi, w_hi) + _dot(a_hi, w_lo) + _dot(a_lo, w_hi)
    mod_ref[0] = acc + b_ref[0]


def _adaln_call(c, ada_w, ada_b):
    depth, d, d3 = ada_w.shape
    batch = c.shape[0]
    n_col = d3 // d
    return pl.pallas_call(
        _adaln_kernel,
        grid=(depth, n_col),
        in_specs=[
            pl.BlockSpec((batch, d), lambda l, j: (0, 0)),
            pl.BlockSpec((1, d, d), lambda l, j: (l, 0, j)),
            pl.BlockSpec((1, 1, d), lambda l, j: (l, 0, j)),
        ],
        out_specs=pl.BlockSpec((1, batch, d), lambda l, j: (l, 0, j)),
        out_shape=jax.ShapeDtypeStruct((depth, batch, d3), F32),
        name="adaln",
    )(c.astype(F32), ada_w.astype(F32), ada_b.astype(F32).reshape(depth, 1, d3))


def _layer_kernel(x_ref, mod_ref, npre_ref, w_in_ref, pool_w_ref, pool_scale_ref, lb_ref,
                  hnw_ref, w_out_ref, npost_ref, wexp_ref, lvl_ref, out_ref,
                  state_ref, tail_ref, q_ref, lf_ref, k_ref, v_ref, g_ref, mix_ref,
                  b_ref, e_ref, ql_ref, qh_ref, kl_ref, attn_ref):
    ts = SEQ_TILE
    s_idx = pl.program_id(1)

    @pl.when(s_idx == 0)
    def _():
        state_ref[...] = jnp.zeros_like(state_ref)
        tail_ref[...] = jnp.zeros_like(tail_ref)
        qh_ref[...] = jnp.zeros_like(qh_ref)

    x = x_ref[0]
    mod = mod_ref[0]
    shift = mod[:, 0:D_MODEL]
    scale = mod[:, D_MODEL:2 * D_MODEL]
    gate = mod[:, 2 * D_MODEL:3 * D_MODEL]

    ms = jnp.mean(x * x, axis=-1, keepdims=True)
    xn = x * lax.rsqrt(ms + EPS) * npre_ref[...]
    h = (xn * (1.0 + scale) + shift).astype(BF16)

    def in_proj(col):
        return _dot(h, w_in_ref[:, col * 1024:(col + 1) * 1024])

    u = in_proj(0)
    part = jnp.concatenate([tail_ref[...], u], axis=0)
    tail_ref[...] = u[ts - POOL_TAIL:ts, :]
    win_sum = []
    for g, w in enumerate(POOL_WINDOWS):
        part = part[:, (POOL_GROUP_WIDTH if g else 0):]
        part = part + pltpu.roll(part, w // 2, 0)
        win_sum.append(part[POOL_TAIL:, 0:POOL_GROUP_WIDTH])
    g_pool = in_proj(1)
    pos = s_idx * ts + lax.broadcasted_iota(jnp.int32, (ts, 1), 0)
    for g, w in enumerate(POOL_WINDOWS):
        lo, hi = g * POOL_GROUP_WIDTH, (g + 1) * POOL_GROUP_WIDTH
        u_g = u[:, lo:hi]
        cnt = jnp.minimum(pos + 1, w).astype(F32)
        d = (win_sum[g] / cnt - u_g).astype(BF16)
        y = _dot(d, pool_w_ref[g]) * pool_scale_ref[:, lo:hi]
        gp = g_pool[:, lo:hi]
        mix_ref[:, lo:hi] = (y * (gp * _sigmoid(gp))).astype(BF16)

    q_ref[...] = in_proj(2)
    lb = lb_ref[...]
    f = lb + (1.0 - lb) * _sigmoid(in_proj(3))
    lf_ref[...] = jnp.log(f) * LOG2E
    k_ref[...] = 1.0 - f
    v_ref[...] = in_proj(4)
    g_ref[...] = in_proj(5)
    lvl = lvl_ref[...]
    hnw = hnw_ref[...]

    def chunk_body(c, carry):
        rows = slice(c * CHUNK, (c + 1) * CHUNK)
        heads = [slice(hd * HEAD_DIM, (hd + 1) * HEAD_DIM) for hd in range(HGRN_HEADS)]
        lf3 = jnp.concatenate(_split_bf16(lf_ref[rows, :], SPLIT), axis=0)
        sums = _dot(wexp_ref[...], lf3)
        b_ref[...] = sums[0:CHUNK]
        e_ref[...] = jnp.exp2(sums[CHUNK:])

        def b_row(r):
            return b_ref[r:r + 1, :]

        def b_group(r):
            return b_ref[r:r + SUBLANES, :]

        q_c = q_ref[rows, :]
        k_c = k_ref[rows, :]
        ql_ref[0:CHUNK, :] = q_c.astype(BF16)
        kl_ref[0] = k_c.astype(BF16)
        ql_ref[CHUNK:2 * CHUNK, :] = (q_c * (1.0 - k_c)).astype(BF16)
        for level in MXU_LEVELS:
            e_l = e_ref[(level - MXU_LEVELS[0]) * CHUNK:(level - MXU_LEVELS[0] + 1) * CHUNK, :]
            ql_ref[level * CHUNK:(level + 1) * CHUNK, :] = (q_c * e_l).astype(BF16)
            kl_ref[level] = (k_c * e_l).astype(BF16)
        half = CHUNK // 2
        for level in range(FIRST_HALF_LEVEL, NUM_LEVELS + 1):
            t_groups = _t_side_groups(level)
            q_t = jnp.concatenate(
                [q_c[r:r + SUBLANES] * jnp.exp2(b_group(r) - b_row(_ref_row(level, r)))
                 for r in t_groups], axis=0).astype(BF16)
            for hd, hl in enumerate(heads):
                r0 = CHUNK * (level - FIRST_HALF_LEVEL) + (hd % 2) * half
                qh_ref[r0:r0 + half, hl] = q_t[:, hl]
            kl_ref[level] = jnp.concatenate(
                [k_c[r:r + SUBLANES] if r in t_groups else
                 k_c[r:r + SUBLANES] * jnp.exp2(b_row(_ref_row(level, r)) - b_group(r))
                 for r in range(0, CHUNK, SUBLANES)], axis=0).astype(BF16)
        for pair in range(HGRN_HEADS // 2):
            pl2 = slice(pair * 2 * HEAD_DIM, (pair + 1) * 2 * HEAD_DIM)
            a_coarse = {lv: _dot_nt(qh_ref[CHUNK * (lv - FIRST_HALF_LEVEL):CHUNK * (lv - FIRST_HALF_LEVEL + 1), pl2],
                                    kl_ref[lv, :, pl2])
                        for lv in range(FIRST_HALF_LEVEL, NUM_LEVELS + 1)}
            for sub in range(2):
                hl = heads[2 * pair + sub]
                a01 = _dot_nt(ql_ref[0:2 * CHUNK, hl], kl_ref[0, :, hl])
                attn = jnp.where(lvl == 0, a01[0:CHUNK], 0.0)
                attn = jnp.where(lvl == 1, a01[CHUNK:2 * CHUNK], attn)
                for lv in MXU_LEVELS:
                    a_lv = _dot_nt(ql_ref[lv * CHUNK:(lv + 1) * CHUNK, hl], kl_ref[lv, :, hl])
                    attn = jnp.where(lvl == lv, a_lv, attn)
                groups = [attn[r:r + SUBLANES] for r in range(0, CHUNK, SUBLANES)]
                for lv in range(FIRST_HALF_LEVEL, NUM_LEVELS + 1):
                    base = sub * (CHUNK // 2)
                    for i, r in enumerate(_t_side_groups(lv)):
                        groups[r // SUBLANES] = jnp.where(
                            lvl[r:r + SUBLANES] == lv,
                            a_coarse[lv][base + i * SUBLANES:base + (i + 1) * SUBLANES],
                            groups[r // SUBLANES])
                attn_ref[2 * pair + sub] = jnp.concatenate(groups, axis=0).astype(BF16)

        b_c = b_ref[...]
        qe = (q_c * jnp.exp2(b_c)).astype(BF16)
        vb = v_ref[rows, :].astype(BF16)
        for hd, hl in enumerate(heads):
            o = _dot(jnp.concatenate([qe[:, hl], attn_ref[hd]], axis=1),
                     jnp.concatenate([state_ref[hd].astype(BF16), vb[:, hl]], axis=0))
            on = o * lax.rsqrt(jnp.mean(o * o, axis=-1, keepdims=True) + EPS) * hnw
            gh = g_ref[rows, hl]
            mix_ref[rows, POOL_WIDTH + hd * HEAD_DIM:POOL_WIDTH + (hd + 1) * HEAD_DIM] = (
                on * (gh * _sigmoid(gh))).astype(BF16)

        b_last = b_row(CHUNK - 1)
        k_dec = k_c * jnp.exp2(b_last - b_c)
        e_last = jnp.exp2(b_last)
        for hd, hl in enumerate(heads):
            k_dec_t = jnp.transpose(k_dec[:, hl]).astype(BF16)
            decay = jnp.transpose(jnp.broadcast_to(e_last[:, hl], (HEAD_DIM, HEAD_DIM)))
            state_ref[hd] = state_ref[hd] * decay + _dot(k_dec_t, vb[:, hl])
        return carry

    for c in range(ts // CHUNK):
        chunk_body(c, 0)

    y = _dot(mix_ref[...], w_out_ref[...])
    yn = y * lax.rsqrt(jnp.mean(y * y, axis=-1, keepdims=True) + EPS) * npost_ref[...]
    out_ref[0] = x + gate * yn


def _layer_call(x, mod, npre, w_in, pool_w, pool_scale, lb, hnw, w_out, npost, wexp, lvl):
    batch, seq, d = x.shape
    ts = SEQ_TILE
    const2 = lambda b, s: (0, 0)
    const3 = lambda b, s: (0, 0, 0)
    once = pl.Buffered(1)
    return pl.pallas_call(
        _layer_kernel,
        grid=(batch, seq // ts),
        in_specs=[
            pl.BlockSpec((1, ts, d), lambda b, s: (b, s, 0)),
            pl.BlockSpec((1, 1, 3 * d), lambda b, s: (b, 0, 0)),
            pl.BlockSpec((1, d), const2),
            pl.BlockSpec((d, IN_WIDTH), const2, pipeline_mode=once),
            pl.BlockSpec(pool_w.shape, const3, pipeline_mode=once),
            pl.BlockSpec((1, POOL_WIDTH), const2),
            pl.BlockSpec((1, HGRN_WIDTH), const2),
            pl.BlockSpec((1, HEAD_DIM), const2),
            pl.BlockSpec((MIX_WIDTH, d), const2, pipeline_mode=once),
            pl.BlockSpec((1, d), const2),
            pl.BlockSpec(wexp.shape, const2),
            pl.BlockSpec(lvl.shape, const2),
        ],
        out_specs=pl.BlockSpec((1, ts, d), lambda b, s: (b, s, 0)),
        out_shape=jax.ShapeDtypeStruct(x.shape, F32),
        scratch_shapes=[
            pltpu.VMEM((HGRN_HEADS, HEAD_DIM, HEAD_DIM), F32),
            pltpu.VMEM((POOL_TAIL, POOL_WIDTH), F32),
            pltpu.VMEM((ts, HGRN_WIDTH), F32),
            pltpu.VMEM((ts, HGRN_WIDTH), F32),
            pltpu.VMEM((ts, HGRN_WIDTH), F32),
            pltpu.VMEM((ts, HGRN_WIDTH), F32),
            pltpu.VMEM((ts, HGRN_WIDTH), F32),
            pltpu.VMEM((ts, MIX_WIDTH), BF16),
            pltpu.VMEM((CHUNK, HGRN_WIDTH), F32),
            pltpu.VMEM((len(MXU_LEVELS) * CHUNK, HGRN_WIDTH), F32),
            pltpu.VMEM((FIRST_HALF_LEVEL * CHUNK, HGRN_WIDTH), BF16),
            pltpu.VMEM(((NUM_LEVELS + 1 - FIRST_HALF_LEVEL) * CHUNK, HGRN_WIDTH), BF16),
            pltpu.VMEM((NUM_LEVELS + 1, CHUNK, HGRN_WIDTH), BF16),
            pltpu.VMEM((HGRN_HEADS, CHUNK, CHUNK), BF16),
        ],
        compiler_params=pltpu.CompilerParams(
            dimension_semantics=("arbitrary", "arbitrary"),
            vmem_limit_bytes=VMEM_LIMIT_BYTES),
        name="hybrid_layer",
    )(x, mod, npre, w_in, pool_w, pool_scale, lb, hnw, w_out, npost, wexp, lvl)


def kernel(x, c, norm_pre_w, ada_w, ada_b, w_in, pool_w, pool_scale, hgrn_lower_bounds,
           hgrn_norm_w, w_out, norm_post_w):
    depth = ada_w.shape[0]
    batch = x.shape[0]
    lower = _lower_bounds_call(hgrn_lower_bounds)
    mod = _adaln_call(c, ada_w, ada_b)
    wexp = jnp.asarray(_exponent_matrix(), dtype=BF16)
    lvl = jnp.asarray(_level_map())
    h = x.astype(F32)
    for l in range(depth):
        h = _layer_call(
            h,
            mod[l].reshape(batch, 1, 3 * D_MODEL),
            norm_pre_w[l].astype(F32).reshape(1, D_MODEL),
            w_in[l].astype(BF16),
            pool_w[l].astype(BF16),
            pool_scale[l].astype(F32).reshape(1, POOL_WIDTH),
            lower[l].reshape(1, HGRN_WIDTH),
            hgrn_norm_w[l].astype(F32).reshape(1, HEAD_DIM),
            w_out[l].astype(BF16),
            norm_post_w[l].astype(F32).reshape(1, D_MODEL),
            wexp, lvl)
    return h.astype(x.dtype)
```

```python
import functools

import numpy as np
import jax
import jax.numpy as jnp
from jax import lax
from jax.experimental import pallas as pl
from jax.experimental.pallas import tpu as pltpu

D_MODEL = 1024
CHUNK = 64
POOL_WIDTH = 1024
POOL_WINDOWS = (2, 4, 8, 16)
POOL_GROUP_WIDTH = POOL_WIDTH // len(POOL_WINDOWS)
HGRN_WIDTH = 1024
HEAD_DIM = 128
HGRN_HEADS = HGRN_WIDTH // HEAD_DIM
MIX_WIDTH = POOL_WIDTH + HGRN_WIDTH
IN_WIDTH = 2 * POOL_WIDTH + 4 * HGRN_WIDTH
EPS = 1e-6

SEQ_TILE = 512
POOL_TAIL = 16
assert POOL_WINDOWS == tuple(2 << g for g in range(len(POOL_WINDOWS))) and POOL_TAIL >= POOL_WINDOWS[-1]
NUM_LEVELS = 6
SPLIT = 3
VMEM_LIMIT_BYTES = 56 * 1024 * 1024

SUBLANES = 8
FIRST_HALF_LEVEL = 4
assert (1 << (FIRST_HALF_LEVEL - 1)) == SUBLANES
MXU_LEVELS = tuple(range(2, FIRST_HALF_LEVEL))
MXU_SETS = 1 + len(MXU_LEVELS)
LOG2E = 1.4426950408889634

F32 = jnp.float32
BF16 = jnp.bfloat16


def _t_side_groups(level):
    m = 1 << (level - 1)
    if m < SUBLANES:
        return None
    return [r for r in range(0, CHUNK, SUBLANES) if (r % (2 * m)) >= m]


def _ref_row(level, row):
    m = 1 << (level - 1)
    return (row // (2 * m)) * (2 * m) + m - 1


def _exponent_matrix():
    t = np.arange(CHUNK)[:, None]
    j = np.arange(CHUNK)[None, :]
    mats = [j <= t]
    for lvl in MXU_LEVELS:
        m = 1 << (lvl - 1)
        ref = _ref_row(lvl, t)
        t_side = (t % (2 * m)) >= m
        mats.append(np.where(t_side, (j > ref) & (j <= t), (j > t) & (j <= ref)))
    w = np.concatenate(mats, axis=0).astype(np.float32)
    return np.concatenate([w] * SPLIT, axis=1)


def _level_map():
    t = np.arange(CHUNK)[:, None]
    s = np.arange(CHUNK)[None, :]
    x = t ^ s
    lvl = np.where(x > 0, np.floor(np.log2(np.maximum(x, 1))).astype(np.int32) + 1, 0)
    return np.where(s > t, -1, lvl).astype(np.int32)


def _split_bf16(a, pieces):
    out = []
    rest = a
    for _ in range(pieces):
        p = rest.astype(BF16)
        out.append(p)
        rest = rest - p.astype(F32)
    return out


def _sigmoid(a):
    return 1.0 / (1.0 + jnp.exp(-a))


def _dot(a, b):
    return jnp.dot(a, b, preferred_element_type=F32)


def _dot_nt(a, b):
    return lax.dot_general(a, b, (((1,), (1,)), ((), ())), preferred_element_type=F32)


def _lower_bounds_kernel(lb_ref, out_ref):
    depth = lb_ref.shape[0]
    rows = [lb_ref[l:l + 1, :] for l in range(depth)]
    mx = rows[0]
    for r in rows[1:]:
        mx = jnp.maximum(mx, r)
    ex = [jnp.exp(r - mx) for r in rows]
    den = ex[0]
    for e in ex[1:]:
        den = den + e
    sm = [e / den for e in ex]
    run = sm[0]
    out_ref[0:1, :] = run - sm[0]
    for l in range(1, depth):
        run = run + sm[l]
        out_ref[l:l + 1, :] = run - sm[0]


def _lower_bounds_call(hgrn_lower_bounds):
    return pl.pallas_call(
        _lower_bounds_kernel,
        out_shape=jax.ShapeDtypeStruct(hgrn_lower_bounds.shape, F32),
        name="lower_bounds",
    )(hgrn_lower_bounds.astype(F32))


def _adaln_kernel(c_ref, w_ref, b_ref, mod_ref):
    c = c_ref[...]
    sc = c * _sigmoid(c)
    w = w_ref[0]
    a_hi, a_lo = _split_bf16(sc, 2)
    w_hi, w_lo = _split_bf16(w, 2)
    acc = _dot(a_hi, w_hi) + _dot(a_hi, w_lo) + _dot(a_lo, w_hi)
    mod_ref[0] = acc + b_ref[0]


def _adaln_call(c, ada_w, ada_b):
    depth, d, d3 = ada_w.shape
    batch = c.shape[0]
    n_col = d3 // d
    return pl.pallas_call(
        _adaln_kernel,
        grid=(depth, n_col),
        in_specs=[
            pl.BlockSpec((batch, d), lambda l, j: (0, 0)),
            pl.BlockSpec((1, d, d), lambda l, j: (l, 0, j)),
            pl.BlockSpec((1, 1, d), lambda l, j: (l, 0, j)),
        ],
        out_specs=pl.BlockSpec((1, batch, d), lambda l, j: (l, 0, j)),
        out_shape=jax.ShapeDtypeStruct((depth, batch, d3), F32),
        name="adaln",
    )(c.astype(F32), ada_w.astype(F32), ada_b.astype(F32).reshape(depth, 1, d3))


def _layer_kernel(x_ref, mod_ref, npre_ref, w_in_ref, pool_w_ref, pool_scale_ref, lb_ref,
                  hnw_ref, w_out_ref, npost_ref, wexp_ref, lvl_ref, out_ref,
                  state_ref, tail_ref, q_ref, lf_ref, k_ref, v_ref, g_ref, mix_ref,
                  b_ref, e_ref, ql_ref, qh_ref, kl_ref, attn_ref):
    ts = SEQ_TILE
    s_idx = pl.program_id(1)

    @pl.when(s_idx == 0)
    def _():
        state_ref[...] = jnp.zeros_like(state_ref)
        tail_ref[...] = jnp.zeros_like(tail_ref)
        qh_ref[...] = jnp.zeros_like(qh_ref)

    x = x_ref[0]
    mod = mod_ref[0]
    shift = mod[:, 0:D_MODEL]
    scale = mod[:, D_MODEL:2 * D_MODEL]
    gate = mod[:, 2 * D_MODEL:3 * D_MODEL]

    ms = jnp.mean(x * x, axis=-1, keepdims=True)
    xn = x * lax.rsqrt(ms + EPS) * npre_ref[...]
    h = (xn * (1.0 + scale) + shift).astype(BF16)

    def in_proj(col):
        return _dot(h, w_in_ref[:, col * 1024:(col + 1) * 1024])

    u = in_proj(0)
    part = jnp.concatenate([tail_ref[...], u], axis=0)
    tail_ref[...] = u[ts - POOL_TAIL:ts, :]
    win_sum = []
    for g, w in enumerate(POOL_WINDOWS):
        part = part[:, (POOL_GROUP_WIDTH if g else 0):]
        part = part + pltpu.roll(part, w // 2, 0)
        win_sum.append(part[POOL_TAIL:, 0:POOL_GROUP_WIDTH])
    g_pool = in_proj(1)
    pos = s_idx * ts + lax.broadcasted_iota(jnp.int32, (ts, 1), 0)
    for g, w in enumerate(POOL_WINDOWS):
        lo, hi = g * POOL_GROUP_WIDTH, (g + 1) * POOL_GROUP_WIDTH
        u_g = u[:, lo:hi]
        cnt = jnp.minimum(pos + 1, w).astype(F32)
        d = (win_sum[g] / cnt - u_g).astype(BF16)
        y = _dot(d, pool_w_ref[g]) * pool_scale_ref[:, lo:hi]
        gp = g_pool[:, lo:hi]
        mix_ref[:, lo:hi] = (y * (gp * _sigmoid(gp))).astype(BF16)

    q_ref[...] = in_proj(2)
    lb = lb_ref[...]
    f = lb + (1.0 - lb) * _sigmoid(in_proj(3))
    lf_ref[...] = jnp.log(f) * LOG2E
    k_ref[...] = 1.0 - f
    v_ref[...] = in_proj(4)
    g_ref[...] = in_proj(5)
    lvl = lvl_ref[...]
    hnw = hnw_ref[...]

    def chunk_body(c, carry):
        rows = slice(c * CHUNK, (c + 1) * CHUNK)
        heads = [slice(hd * HEAD_DIM, (hd + 1) * HEAD_DIM) for hd in range(HGRN_HEADS)]
        lf3 = jnp.concatenate(_split_bf16(lf_ref[rows, :], SPLIT), axis=0)
        sums = _dot(wexp_ref[...], lf3)
        b_ref[...] = sums[0:CHUNK]
        e_ref[...] = jnp.exp2(sums[CHUNK:])

        def b_row(r):
            return b_ref[r:r + 1, :]

        def b_group(r):
            return b_ref[r:r + SUBLANES, :]

        q_c = q_ref[rows, :]
        k_c = k_ref[rows, :]
        ql_ref[0:CHUNK, :] = q_c.astype(BF16)
        kl_ref[0] = k_c.astype(BF16)
        ql_ref[CHUNK:2 * CHUNK, :] = (q_c * (1.0 - k_c)).astype(BF16)
        for level in MXU_LEVELS:
            e_l = e_ref[(level - MXU_LEVELS[0]) * CHUNK:(level - MXU_LEVELS[0] + 1) * CHUNK, :]
            ql_ref[level * CHUNK:(level + 1) * CHUNK, :] = (q_c * e_l).astype(BF16)
            kl_ref[level] = (k_c * e_l).astype(BF16)
        half = CHUNK // 2
        for level in range(FIRST_HALF_LEVEL, NUM_LEVELS + 1):
            t_groups = _t_side_groups(level)
            q_t = jnp.concatenate(
                [q_c[r:r + SUBLANES] * jnp.exp2(b_group(r) - b_row(_ref_row(level, r)))
                 for r in t_groups], axis=0).astype(BF16)
            for hd, hl in enumerate(heads):
                r0 = CHUNK * (level - FIRST_HALF_LEVEL) + (hd % 2) * half
                qh_ref[r0:r0 + half, hl] = q_t[:, hl]
            kl_ref[level] = jnp.concatenate(
                [k_c[r:r + SUBLANES] if r in t_groups else
                 k_c[r:r + SUBLANES] * jnp.exp2(b_row(_ref_row(level, r)) - b_group(r))
                 for r in range(0, CHUNK, SUBLANES)], axis=0).astype(BF16)
        for pair in range(HGRN_HEADS // 2):
            pl2 = slice(pair * 2 * HEAD_DIM, (pair + 1) * 2 * HEAD_DIM)
            a_coarse = {lv: _dot_nt(qh_ref[CHUNK * (lv - FIRST_HALF_LEVEL):CHUNK * (lv - FIRST_HALF_LEVEL + 1), pl2],
                                    kl_ref[lv, :, pl2])
                        for lv in range(FIRST_HALF_LEVEL, NUM_LEVELS + 1)}
            for sub in range(2):
                hl = heads[2 * pair + sub]
                a01 = _dot_nt(ql_ref[0:2 * CHUNK, hl], kl_ref[0, :, hl])
                attn = jnp.where(lvl == 0, a01[0:CHUNK], 0.0)
                attn = jnp.where(lvl == 1, a01[CHUNK:2 * CHUNK], attn)
                for lv in MXU_LEVELS:
                    a_lv = _dot_nt(ql_ref[lv * CHUNK:(lv + 1) * CHUNK, hl], kl_ref[lv, :, hl])
                    attn = jnp.where(lvl == lv, a_lv, attn)
                groups = [attn[r:r + SUBLANES] for r in range(0, CHUNK, SUBLANES)]
                for lv in range(FIRST_HALF_LEVEL, NUM_LEVELS + 1):
                    base = sub * (CHUNK // 2)
                    for i, r in enumerate(_t_side_groups(lv)):
                        groups[r // SUBLANES] = jnp.where(
                            lvl[r:r + SUBLANES] == lv,
                            a_coarse[lv][base + i * SUBLANES:base + (i + 1) * SUBLANES],
                            groups[r // SUBLANES])
                attn_ref[2 * pair + sub] = jnp.concatenate(groups, axis=0).astype(BF16)

        b_c = b_ref[...]
        qe = (q_c * jnp.exp2(b_c)).astype(BF16)
        vb = v_ref[rows, :].astype(BF16)
        for hd, hl in enumerate(heads):
            o = _dot(jnp.concatenate([qe[:, hl], attn_ref[hd]], axis=1),
                     jnp.concatenate([state_ref[hd].astype(BF16), vb[:, hl]], axis=0))
            on = o * lax.rsqrt(jnp.mean(o * o, axis=-1, keepdims=True) + EPS) * hnw
            gh = g_ref[rows, hl]
            mix_ref[rows, POOL_WIDTH + hd * HEAD_DIM:POOL_WIDTH + (hd + 1) * HEAD_DIM] = (
                on * (gh * _sigmoid(gh))).astype(BF16)

        b_last = b_row(CHUNK - 1)
        k_dec = k_c * jnp.exp2(b_last - b_c)
        e_last = jnp.exp2(b_last)
        for hd, hl in enumerate(heads):
            k_dec_t = jnp.transpose(k_dec[:, hl]).astype(BF16)
            decay = jnp.transpose(jnp.broadcast_to(e_last[:, hl], (HEAD_DIM, HEAD_DIM)))
            state_ref[hd] = state_ref[hd] * decay + _dot(k_dec_t, vb[:, hl])
        return carry

    for c in range(ts // CHUNK):
        chunk_body(c, 0)

    y = _dot(mix_ref[...], w_out_ref[...])
    yn = y * lax.rsqrt(jnp.mean(y * y, axis=-1, keepdims=True) + EPS) * npost_ref[...]
    out_ref[0] = x + gate * yn


def _layer_call(x, mod, npre, w_in, pool_w, pool_scale, lb, hnw, w_out, npost, wexp, lvl):
    batch, seq, d = x.shape
    ts = SEQ_TILE
    const2 = lambda b, s: (0, 0)
    const3 = lambda b, s: (0, 0, 0)
    once = pl.Buffered(1)
    return pl.pallas_call(
        _layer_kernel,
        grid=(batch, seq // ts),
        in_specs=[
            pl.BlockSpec((1, ts, d), lambda b, s: (b, s, 0)),
            pl.BlockSpec((1, 1, 3 * d), lambda b, s: (b, 0, 0)),
            pl.BlockSpec((1, d), const2),
            pl.BlockSpec((d, IN_WIDTH), const2, pipeline_mode=once),
            pl.BlockSpec(pool_w.shape, const3, pipeline_mode=once),
            pl.BlockSpec((1, POOL_WIDTH), const2),
            pl.BlockSpec((1, HGRN_WIDTH), const2),
            pl.BlockSpec((1, HEAD_DIM), const2),
            pl.BlockSpec((MIX_WIDTH, d), const2, pipeline_mode=once),
            pl.BlockSpec((1, d), const2),
            pl.BlockSpec(wexp.shape, const2),
            pl.BlockSpec(lvl.shape, const2),
        ],
        out_specs=pl.BlockSpec((1, ts, d), lambda b, s: (b, s, 0)),
        out_shape=jax.ShapeDtypeStruct(x.shape, F32),
        scratch_shapes=[
            pltpu.VMEM((HGRN_HEADS, HEAD_DIM, HEAD_DIM), F32),
            pltpu.VMEM((POOL_TAIL, POOL_WIDTH), F32),
            pltpu.VMEM((ts, HGRN_WIDTH), F32),
            pltpu.VMEM((ts, HGRN_WIDTH), F32),
            pltpu.VMEM((ts, HGRN_WIDTH), F32),
            pltpu.VMEM((ts, HGRN_WIDTH), F32),
            pltpu.VMEM((ts, HGRN_WIDTH), F32),
            pltpu.VMEM((ts, MIX_WIDTH), BF16),
            pltpu.VMEM((CHUNK, HGRN_WIDTH), F32),
            pltpu.VMEM((len(MXU_LEVELS) * CHUNK, HGRN_WIDTH), F32),
            pltpu.VMEM((FIRST_HALF_LEVEL * CHUNK, HGRN_WIDTH), BF16),
            pltpu.VMEM(((NUM_LEVELS + 1 - FIRST_HALF_LEVEL) * CHUNK, HGRN_WIDTH), BF16),
            pltpu.VMEM((NUM_LEVELS + 1, CHUNK, HGRN_WIDTH), BF16),
            pltpu.VMEM((HGRN_HEADS, CHUNK, CHUNK), BF16),
        ],
        compiler_params=pltpu.CompilerParams(
            dimension_semantics=("arbitrary", "arbitrary"),
            vmem_limit_bytes=VMEM_LIMIT_BYTES),
        name="hybrid_layer",
    )(x, mod, npre, w_in, pool_w, pool_scale, lb, hnw, w_out, npost, wexp, lvl)


def kernel(x, c, norm_pre_w, ada_w, ada_b, w_in, pool_w, pool_scale, hgrn_lower_bounds,
           hgrn_norm_w, w_out, norm_post_w):
    depth = ada_w.shape[0]
    batch = x.shape[0]
    lower = _lower_bounds_call(hgrn_lower_bounds)
    mod = _adaln_call(c, ada_w, ada_b)
    wexp = jnp.asarray(_exponent_matrix(), dtype=BF16)
    lvl = jnp.asarray(_level_map())
    h = x.astype(F32)
    for l in range(depth):
        h = _layer_call(
            h,
            mod[l].reshape(batch, 1, 3 * D_MODEL),
            norm_pre_w[l].astype(F32).reshape(1, D_MODEL),
            w_in[l].astype(BF16),
            pool_w[l].astype(BF16),
            pool_scale[l].astype(F32).reshape(1, POOL_WIDTH),
            lower[l].reshape(1, HGRN_WIDTH),
            hgrn_norm_w[l].astype(F32).reshape(1, HEAD_DIM),
            w_out[l].astype(BF16),
            norm_post_w[l].astype(F32).reshape(1, D_MODEL),
            wexp, lvl)
    return h.astype(x.dtype)
```

```python
import functools

import numpy as np
import jax
import jax.numpy as jnp
from jax import lax
from jax.experimental import pallas as pl
from jax.experimental.pallas import tpu as pltpu

D_MODEL = 1024
CHUNK = 64
POOL_WIDTH = 1024
POOL_WINDOWS = (2, 4, 8, 16)
POOL_GROUP_WIDTH = POOL_WIDTH // len(POOL_WINDOWS)
HGRN_WIDTH = 1024
HEAD_DIM = 128
HGRN_HEADS = HGRN_WIDTH // HEAD_DIM
MIX_WIDTH = POOL_WIDTH + HGRN_WIDTH
IN_WIDTH = 2 * POOL_WIDTH + 4 * HGRN_WIDTH
EPS = 1e-6

SEQ_TILE = 512
POOL_TAIL = 16
assert POOL_WINDOWS == tuple(2 << g for g in range(len(POOL_WINDOWS))) and POOL_TAIL >= POOL_WINDOWS[-1]
NUM_LEVELS = 6
SPLIT = 3
VMEM_LIMIT_BYTES = 56 * 1024 * 1024

SUBLANES = 8
FIRST_HALF_LEVEL = 4
assert (1 << (FIRST_HALF_LEVEL - 1)) == SUBLANES
MXU_LEVELS = tuple(range(2, FIRST_HALF_LEVEL))
MXU_SETS = 1 + len(MXU_LEVELS)
LOG2E = 1.4426950408889634
MID_ROW = CHUNK // 2 - 1
DIRECT_MAX_LOG2 = 110.0

F32 = jnp.float32
BF16 = jnp.bfloat16


def _t_side_groups(level):
    m = 1 << (level - 1)
    if m < SUBLANES:
        return None
    return [r for r in range(0, CHUNK, SUBLANES) if (r % (2 * m)) >= m]


def _ref_row(level, row):
    m = 1 << (level - 1)
    return (row // (2 * m)) * (2 * m) + m - 1


def _exponent_matrix():
    t = np.arange(CHUNK)[:, None]
    j = np.arange(CHUNK)[None, :]
    mats = [j <= t]
    for lvl in MXU_LEVELS:
        m = 1 << (lvl - 1)
        ref = _ref_row(lvl, t)
        t_side = (t % (2 * m)) >= m
        mats.append(np.where(t_side, (j > ref) & (j <= t), (j > t) & (j <= ref)))
    w = np.concatenate(mats, axis=0).astype(np.float32)
    return np.concatenate([w] * SPLIT, axis=1)


def _level_map():
    t = np.arange(CHUNK)[:, None]
    s = np.arange(CHUNK)[None, :]
    x = t ^ s
    lvl = np.where(x > 0, np.floor(np.log2(np.maximum(x, 1))).astype(np.int32) + 1, 0)
    return np.where(s > t, -1, lvl).astype(np.int32)


def _split_bf16(a, pieces):
    out = []
    rest = a
    for _ in range(pieces):
        p = rest.astype(BF16)
        out.append(p)
        rest = rest - p.astype(F32)
    return out


def _sigmoid(a):
    return 1.0 / (1.0 + jnp.exp(-a))


def _dot(a, b):
    return jnp.dot(a, b, preferred_element_type=F32)


def _dot_nt(a, b):
    return lax.dot_general(a, b, (((1,), (1,)), ((), ())), preferred_element_type=F32)


def _lower_bounds_kernel(lb_ref, out_ref):
    depth = lb_ref.shape[0]
    rows = [lb_ref[l:l + 1, :] for l in range(depth)]
    mx = rows[0]
    for r in rows[1:]:
        mx = jnp.maximum(mx, r)
    ex = [jnp.exp(r - mx) for r in rows]
    den = ex[0]
    for e in ex[1:]:
        den = den + e
    sm = [e / den for e in ex]
    run = sm[0]
    out_ref[0:1, :] = run - sm[0]
    for l in range(1, depth):
        run = run + sm[l]
        out_ref[l:l + 1, :] = run - sm[0]


def _lower_bounds_call(hgrn_lower_bounds):
    return pl.pallas_call(
        _lower_bounds_kernel,
        out_shape=jax.ShapeDtypeStruct(hgrn_lower_bounds.shape, F32),
        name="lower_bounds",
    )(hgrn_lower_bounds.astype(F32))


def _adaln_kernel(c_ref, w_ref, b_ref, mod_ref):
    c = c_ref[...]
    sc = c * _sigmoid(c)
    w = w_ref[0]
    a_hi, a_lo = _split_bf16(sc, 2)
    w_hi, w_lo = _split_bf16(w, 2)
    acc = _dot(a_hi, w_hi) + _dot(a_hi, w_lo) + _dot(a_lo, w_hi)
    mod_ref[0] = acc + b_ref[0]


def _adaln_call(c, ada_w, ada_b):
    depth, d, d3 = ada_w.shape
    batch = c.shape[0]
    n_col = d3 // d
    return pl.pallas_call(
        _adaln_kernel,
        grid=(depth, n_col),
        in_specs=[
            pl.BlockSpec((batch, d), lambda l, j: (0, 0)),
            pl.BlockSpec((1, d, d), lambda l, j: (l, 0, j)),
            pl.BlockSpec((1, 1, d), lambda l, j: (l, 0, j)),
        ],
        out_specs=pl.BlockSpec((1, batch, d), lambda l, j: (l, 0, j)),
        out_shape=jax.ShapeDtypeStruct((depth, batch, d3), F32),
        name="adaln",
    )(c.astype(F32), ada_w.astype(F32), ada_b.astype(F32).reshape(depth, 1, d3))


def _layer_kernel(x_ref, mod_ref, npre_ref, w_in_ref, pool_w_ref, pool_scale_ref, lb_ref,
                  hnw_ref, w_out_ref, npost_ref, wexp_ref, lvl_ref, out_ref,
                  state_ref, tail_ref, q_ref, lf_ref, k_ref, v_ref, g_ref, mix_ref,
                  bsum_ref, b_ref, e_ref, ql_ref, qh_ref, kl_ref, attn_ref):
    ts = SEQ_TILE
    s_idx = pl.program_id(1)

    @pl.when(s_idx == 0)
    def _():
        state_ref[...] = jnp.zeros_like(state_ref)
        tail_ref[...] = jnp.zeros_like(tail_ref)
        qh_ref[...] = jnp.zeros_like(qh_ref)

    x = x_ref[0]
    mod = mod_ref[0]
    shift = mod[:, 0:D_MODEL]
    scale = mod[:, D_MODEL:2 * D_MODEL]
    gate = mod[:, 2 * D_MODEL:3 * D_MODEL]

    ms = jnp.mean(x * x, axis=-1, keepdims=True)
    xn = x * lax.rsqrt(ms + EPS) * npre_ref[...]
    h = (xn * (1.0 + scale) + shift).astype(BF16)

    def in_proj(col):
        return _dot(h, w_in_ref[:, col * 1024:(col + 1) * 1024])

    u = in_proj(0)
    part = jnp.concatenate([tail_ref[...], u], axis=0)
    tail_ref[...] = u[ts - POOL_TAIL:ts, :]
    win_sum = []
    for g, w in enumerate(POOL_WINDOWS):
        part = part[:, (POOL_GROUP_WIDTH if g else 0):]
        part = part + pltpu.roll(part, w // 2, 0)
        win_sum.append(part[POOL_TAIL:, 0:POOL_GROUP_WIDTH])
    g_pool = in_proj(1)
    pos = s_idx * ts + lax.broadcasted_iota(jnp.int32, (ts, 1), 0)
    for g, w in enumerate(POOL_WINDOWS):
        lo, hi = g * POOL_GROUP_WIDTH, (g + 1) * POOL_GROUP_WIDTH
        u_g = u[:, lo:hi]
        cnt = jnp.minimum(pos + 1, w).astype(F32)
        d = (win_sum[g] / cnt - u_g).astype(BF16)
        y = _dot(d, pool_w_ref[g]) * pool_scale_ref[:, lo:hi]
        gp = g_pool[:, lo:hi]
        mix_ref[:, lo:hi] = (y * (gp * _sigmoid(gp))).astype(BF16)

    q_ref[...] = in_proj(2)
    lb = lb_ref[...]
    f = lb + (1.0 - lb) * _sigmoid(in_proj(3))
    lf_ref[...] = jnp.log(f) * LOG2E
    k_ref[...] = 1.0 - f
    v_ref[...] = in_proj(4)
    g_ref[...] = in_proj(5)
    lvl = lvl_ref[...]
    hnw = hnw_ref[...]

    heads = [slice(hd * HEAD_DIM, (hd + 1) * HEAD_DIM) for hd in range(HGRN_HEADS)]
    n_chunks = ts // CHUNK
    causal = lvl >= 0

    def split_log_decay(rows):
        return jnp.concatenate(_split_bf16(lf_ref[rows, :], SPLIT), axis=0)

    def chunk_outputs(rows, qe, k_dec, e_last):
        vb = v_ref[rows, :].astype(BF16)
        for hd, hl in enumerate(heads):
            o = _dot(jnp.concatenate([qe[:, hl], attn_ref[hd]], axis=1),
                     jnp.concatenate([state_ref[hd].astype(BF16), vb[:, hl]], axis=0))
            on = o * lax.rsqrt(jnp.mean(o * o, axis=-1, keepdims=True) + EPS) * hnw
            gh = g_ref[rows, hl]
            mix_ref[rows, POOL_WIDTH + hd * HEAD_DIM:POOL_WIDTH + (hd + 1) * HEAD_DIM] = (
                on * (gh * _sigmoid(gh))).astype(BF16)
        for hd, hl in enumerate(heads):
            k_dec_t = jnp.transpose(k_dec[:, hl]).astype(BF16)
            decay = jnp.transpose(jnp.broadcast_to(e_last[:, hl], (HEAD_DIM, HEAD_DIM)))
            state_ref[hd] = state_ref[hd] * decay + _dot(k_dec_t, vb[:, hl])

    def direct_chunk(c):
        rows = slice(c * CHUNK, (c + 1) * CHUNK)
        b_c = bsum_ref[rows, :]
        b_mid = bsum_ref[c * CHUNK + MID_ROW:c * CHUNK + MID_ROW + 1, :]
        b_last = bsum_ref[(c + 1) * CHUNK - 1:(c + 1) * CHUNK, :]
        d = b_c - b_mid
        q_s = q_ref[rows, :] * jnp.exp2(d)
        k_s = k_ref[rows, :] * jnp.exp2(-d)
        q_b = q_s.astype(BF16)
        k_b = k_s.astype(BF16)
        for hd, hl in enumerate(heads):
            attn_ref[hd] = jnp.where(causal, _dot_nt(q_b[:, hl], k_b[:, hl]), 0.0).astype(BF16)
        chunk_outputs(rows, (q_s * jnp.exp2(b_mid)).astype(BF16),
                      k_s * jnp.exp2(b_last - b_mid), jnp.exp2(b_last))

    def robust_chunk(c, carry):
        rows = pl.ds(pl.multiple_of(c * CHUNK, CHUNK), CHUNK)
        sums = _dot(wexp_ref[...], split_log_decay(rows))
        b_ref[...] = sums[0:CHUNK]
        e_ref[...] = jnp.exp2(sums[CHUNK:])

        def b_row(r):
            return b_ref[r:r + 1, :]

        def b_group(r):
            return b_ref[r:r + SUBLANES, :]

        q_c = q_ref[rows, :]
        k_c = k_ref[rows, :]
        ql_ref[0:CHUNK, :] = q_c.astype(BF16)
        kl_ref[0] = k_c.astype(BF16)
        ql_ref[CHUNK:2 * CHUNK, :] = (q_c * (1.0 - k_c)).astype(BF16)
        for level in MXU_LEVELS:
            e_l = e_ref[(level - MXU_LEVELS[0]) * CHUNK:(level - MXU_LEVELS[0] + 1) * CHUNK, :]
            ql_ref[level * CHUNK:(level + 1) * CHUNK, :] = (q_c * e_l).astype(BF16)
            kl_ref[level] = (k_c * e_l).astype(BF16)
        half = CHUNK // 2
        for level in range(FIRST_HALF_LEVEL, NUM_LEVELS + 1):
            t_groups = _t_side_groups(level)
            q_t = jnp.concatenate(
                [q_c[r:r + SUBLANES] * jnp.exp2(b_group(r) - b_row(_ref_row(level, r)))
                 for r in t_groups], axis=0).astype(BF16)
            for hd, hl in enumerate(heads):
                r0 = CHUNK * (level - FIRST_HALF_LEVEL) + (hd % 2) * half
                qh_ref[r0:r0 + half, hl] = q_t[:, hl]
            kl_ref[level] = jnp.concatenate(
                [k_c[r:r + SUBLANES] if r in t_groups else
                 k_c[r:r + SUBLANES] * jnp.exp2(b_row(_ref_row(level, r)) - b_group(r))
                 for r in range(0, CHUNK, SUBLANES)], axis=0).astype(BF16)
        for pair in range(HGRN_HEADS // 2):
            pl2 = slice(pair * 2 * HEAD_DIM, (pair + 1) * 2 * HEAD_DIM)
            a_coarse = {lv: _dot_nt(qh_ref[CHUNK * (lv - FIRST_HALF_LEVEL):CHUNK * (lv - FIRST_HALF_LEVEL + 1), pl2],
                                    kl_ref[lv, :, pl2])
                        for lv in range(FIRST_HALF_LEVEL, NUM_LEVELS + 1)}
            for sub in range(2):
                hl = heads[2 * pair + sub]
                a01 = _dot_nt(ql_ref[0:2 * CHUNK, hl], kl_ref[0, :, hl])
                attn = jnp.where(lvl == 0, a01[0:CHUNK], 0.0)
                attn = jnp.where(lvl == 1, a01[CHUNK:2 * CHUNK], attn)
                for lv in MXU_LEVELS:
                    a_lv = _dot_nt(ql_ref[lv * CHUNK:(lv + 1) * CHUNK, hl], kl_ref[lv, :, hl])
                    attn = jnp.where(lvl == lv, a_lv, attn)
                groups = [attn[r:r + SUBLANES] for r in range(0, CHUNK, SUBLANES)]
                for lv in range(FIRST_HALF_LEVEL, NUM_LEVELS + 1):
                    base = sub * (CHUNK // 2)
                    for i, r in enumerate(_t_side_groups(lv)):
                        groups[r // SUBLANES] = jnp.where(
                            lvl[r:r + SUBLANES] == lv,
                            a_coarse[lv][base + i * SUBLANES:base + (i + 1) * SUBLANES],
                            groups[r // SUBLANES])
                attn_ref[2 * pair + sub] = jnp.concatenate(groups, axis=0).astype(BF16)

        b_c = b_ref[...]
        b_last = b_row(CHUNK - 1)
        chunk_outputs(rows, (q_c * jnp.exp2(b_c)).astype(BF16),
                      k_c * jnp.exp2(b_last - b_c), jnp.exp2(b_last))
        return carry

    unsafe = None
    for c in range(n_chunks):
        rows = slice(c * CHUNK, (c + 1) * CHUNK)
        bsum_ref[rows, :] = _dot(wexp_ref[0:CHUNK, :], split_log_decay(rows))
        b_mid = bsum_ref[c * CHUNK + MID_ROW:c * CHUNK + MID_ROW + 1, :]
        b_last = bsum_ref[(c + 1) * CHUNK - 1:(c + 1) * CHUNK, :]
        ok = jnp.logical_and(-b_mid <= DIRECT_MAX_LOG2, b_mid - b_last <= DIRECT_MAX_LOG2)
        bad = jnp.where(ok, 0.0, 1.0)
        unsafe = bad if unsafe is None else jnp.maximum(unsafe, bad)
    use_direct = jnp.max(unsafe) < 0.5

    @pl.when(use_direct)
    def _():
        for c in range(n_chunks):
            direct_chunk(c)

    @pl.when(jnp.logical_not(use_direct))
    def _():
        lax.fori_loop(0, n_chunks, robust_chunk, 0)

    y = _dot(mix_ref[...], w_out_ref[...])
    yn = y * lax.rsqrt(jnp.mean(y * y, axis=-1, keepdims=True) + EPS) * npost_ref[...]
    out_ref[0] = x + gate * yn


def _layer_call(x, mod, npre, w_in, pool_w, pool_scale, lb, hnw, w_out, npost, wexp, lvl):
    batch, seq, d = x.shape
    ts = SEQ_TILE
    const2 = lambda b, s: (0, 0)
    const3 = lambda b, s: (0, 0, 0)
    once = pl.Buffered(1)
    return pl.pallas_call(
        _layer_kernel,
        grid=(batch, seq // ts),
        in_specs=[
            pl.BlockSpec((1, ts, d), lambda b, s: (b, s, 0)),
            pl.BlockSpec((1, 1, 3 * d), lambda b, s: (b, 0, 0)),
            pl.BlockSpec((1, d), const2),
            pl.BlockSpec((d, IN_WIDTH), const2, pipeline_mode=once),
            pl.BlockSpec(pool_w.shape, const3, pipeline_mode=once),
            pl.BlockSpec((1, POOL_WIDTH), const2),
            pl.BlockSpec((1, HGRN_WIDTH), const2),
            pl.BlockSpec((1, HEAD_DIM), const2),
            pl.BlockSpec((MIX_WIDTH, d), const2, pipeline_mode=once),
            pl.BlockSpec((1, d), const2),
            pl.BlockSpec(wexp.shape, const2),
            pl.BlockSpec(lvl.shape, const2),
        ],
        out_specs=pl.BlockSpec((1, ts, d), lambda b, s: (b, s, 0)),
        out_shape=jax.ShapeDtypeStruct(x.shape, F32),
        scratch_shapes=[
            pltpu.VMEM((HGRN_HEADS, HEAD_DIM, HEAD_DIM), F32),
            pltpu.VMEM((POOL_TAIL, POOL_WIDTH), F32),
            pltpu.VMEM((ts, HGRN_WIDTH), F32),
            pltpu.VMEM((ts, HGRN_WIDTH), F32),
            pltpu.VMEM((ts, HGRN_WIDTH), F32),
            pltpu.VMEM((ts, HGRN_WIDTH), F32),
            pltpu.VMEM((ts, HGRN_WIDTH), F32),
            pltpu.VMEM((ts, MIX_WIDTH), BF16),
            pltpu.VMEM((ts, HGRN_WIDTH), F32),
            pltpu.VMEM((CHUNK, HGRN_WIDTH), F32),
            pltpu.VMEM((len(MXU_LEVELS) * CHUNK, HGRN_WIDTH), F32),
            pltpu.VMEM((FIRST_HALF_LEVEL * CHUNK, HGRN_WIDTH), BF16),
            pltpu.VMEM(((NUM_LEVELS + 1 - FIRST_HALF_LEVEL) * CHUNK, HGRN_WIDTH), BF16),
            pltpu.VMEM((NUM_LEVELS + 1, CHUNK, HGRN_WIDTH), BF16),
            pltpu.VMEM((HGRN_HEADS, CHUNK, CHUNK), BF16),
        ],
        compiler_params=pltpu.CompilerParams(
            dimension_semantics=("arbitrary", "arbitrary"),
            vmem_limit_bytes=VMEM_LIMIT_BYTES),
        name="hybrid_layer",
    )(x, mod, npre, w_in, pool_w, pool_scale, lb, hnw, w_out, npost, wexp, lvl)


def kernel(x, c, norm_pre_w, ada_w, ada_b, w_in, pool_w, pool_scale, hgrn_lower_bounds,
           hgrn_norm_w, w_out, norm_post_w):
    depth = ada_w.shape[0]
    batch = x.shape[0]
    lower = _lower_bounds_call(hgrn_lower_bounds)
    mod = _adaln_call(c, ada_w, ada_b)
    wexp = jnp.asarray(_exponent_matrix(), dtype=BF16)
    lvl = jnp.asarray(_level_map())
    h = x.astype(F32)
    for l in range(depth):
        h = _layer_call(
            h,
            mod[l].reshape(batch, 1, 3 * D_MODEL),
            norm_pre_w[l].astype(F32).reshape(1, D_MODEL),
            w_in[l].astype(BF16),
            pool_w[l].astype(BF16),
            pool_scale[l].astype(F32).reshape(1, POOL_WIDTH),
            lower[l].reshape(1, HGRN_WIDTH),
            hgrn_norm_w[l].astype(F32).reshape(1, HEAD_DIM),
            w_out[l].astype(BF16),
            norm_post_w[l].astype(F32).reshape(1, D_MODEL),
            wexp, lvl)
    return h.astype(x.dtype)
```

```python
import functools

import numpy as np
import jax
import jax.numpy as jnp
from jax import lax
from jax.experimental import pallas as pl
from jax.experimental.pallas import tpu as pltpu

D_MODEL = 1024
CHUNK = 64
POOL_WIDTH = 1024
POOL_WINDOWS = (2, 4, 8, 16)
POOL_GROUP_WIDTH = POOL_WIDTH // len(POOL_WINDOWS)
HGRN_WIDTH = 1024
HEAD_DIM = 128
HGRN_HEADS = HGRN_WIDTH // HEAD_DIM
MIX_WIDTH = POOL_WIDTH + HGRN_WIDTH
IN_WIDTH = 2 * POOL_WIDTH + 4 * HGRN_WIDTH
EPS = 1e-6

SEQ_TILE = 512
POOL_TAIL = 16
assert POOL_WINDOWS == tuple(2 << g for g in range(len(POOL_WINDOWS))) and POOL_TAIL >= POOL_WINDOWS[-1]
NUM_LEVELS = 6
SPLIT = 3
VMEM_LIMIT_BYTES = 56 * 1024 * 1024

SUBLANES = 8
FIRST_HALF_LEVEL = 4
assert (1 << (FIRST_HALF_LEVEL - 1)) == SUBLANES
MXU_LEVELS = tuple(range(2, FIRST_HALF_LEVEL))
MXU_SETS = 1 + len(MXU_LEVELS)
LOG2E = 1.4426950408889634
MID_ROW = CHUNK // 2 - 1
DIRECT_MAX_LOG2 = 110.0

F32 = jnp.float32
BF16 = jnp.bfloat16


def _t_side_groups(level):
    m = 1 << (level - 1)
    if m < SUBLANES:
        return None
    return [r for r in range(0, CHUNK, SUBLANES) if (r % (2 * m)) >= m]


def _ref_row(level, row):
    m = 1 << (level - 1)
    return (row // (2 * m)) * (2 * m) + m - 1


def _exponent_matrix():
    t = np.arange(CHUNK)[:, None]
    j = np.arange(CHUNK)[None, :]
    mats = [j <= t]
    for lvl in MXU_LEVELS:
        m = 1 << (lvl - 1)
        ref = _ref_row(lvl, t)
        t_side = (t % (2 * m)) >= m
        mats.append(np.where(t_side, (j > ref) & (j <= t), (j > t) & (j <= ref)))
    w = np.concatenate(mats, axis=0).astype(np.float32)
    return np.concatenate([w] * SPLIT, axis=1)


def _level_map():
    t = np.arange(CHUNK)[:, None]
    s = np.arange(CHUNK)[None, :]
    x = t ^ s
    lvl = np.where(x > 0, np.floor(np.log2(np.maximum(x, 1))).astype(np.int32) + 1, 0)
    return np.where(s > t, -1, lvl).astype(np.int32)


def _split_bf16(a, pieces):
    out = []
    rest = a
    for _ in range(pieces):
        p = rest.astype(BF16)
        out.append(p)
        rest = rest - p.astype(F32)
    return out


def _sigmoid(a):
    return 1.0 / (1.0 + jnp.exp(-a))


def _dot(a, b):
    return jnp.dot(a, b, preferred_element_type=F32)


def _dot_nt(a, b):
    return lax.dot_general(a, b, (((1,), (1,)), ((), ())), preferred_element_type=F32)


def _lower_bounds_kernel(lb_ref, out_ref):
    depth = lb_ref.shape[0]
    rows = [lb_ref[l:l + 1, :] for l in range(depth)]
    mx = rows[0]
    for r in rows[1:]:
        mx = jnp.maximum(mx, r)
    ex = [jnp.exp(r - mx) for r in rows]
    den = ex[0]
    for e in ex[1:]:
        den = den + e
    sm = [e / den for e in ex]
    run = sm[0]
    out_ref[0:1, :] = run - sm[0]
    for l in range(1, depth):
        run = run + sm[l]
        out_ref[l:l + 1, :] = run - sm[0]


def _lower_bounds_call(hgrn_lower_bounds):
    return pl.pallas_call(
        _lower_bounds_kernel,
        out_shape=jax.ShapeDtypeStruct(hgrn_lower_bounds.shape, F32),
        name="lower_bounds",
    )(hgrn_lower_bounds.astype(F32))


def _adaln_kernel(c_ref, w_ref, b_ref, mod_ref):
    c = c_ref[...]
    sc = c * _sigmoid(c)
    w = w_ref[0]
    a_hi, a_lo = _split_bf16(sc, 2)
    w_hi, w_lo = _split_bf16(w, 2)
    acc = _dot(a_hi, w_hi) + _dot(a_hi, w_lo) + _dot(a_lo, w_hi)
    mod_ref[0] = acc + b_ref[0]


def _adaln_call(c, ada_w, ada_b):
    depth, d, d3 = ada_w.shape
    batch = c.shape[0]
    n_col = d3 // d
    return pl.pallas_call(
        _adaln_kernel,
        grid=(depth, n_col),
        in_specs=[
            pl.BlockSpec((batch, d), lambda l, j: (0, 0)),
            pl.BlockSpec((1, d, d), lambda l, j: (l, 0, j)),
            pl.BlockSpec((1, 1, d), lambda l, j: (l, 0, j)),
        ],
        out_specs=pl.BlockSpec((1, batch, d), lambda l, j: (l, 0, j)),
        out_shape=jax.ShapeDtypeStruct((depth, batch, d3), F32),
        name="adaln",
    )(c.astype(F32), ada_w.astype(F32), ada_b.astype(F32).reshape(depth, 1, d3))


def _layer_kernel(x_ref, mod_ref, npre_ref, w_in_ref, pool_w_ref, pool_scale_ref, lb_ref,
                  hnw_ref, w_out_ref, npost_ref, wexp_ref, lvl_ref, out_ref,
                  state_ref, tail_ref, q_ref, lf_ref, k_ref, v_ref, g_ref, mix_ref,
                  bsum_ref, b_ref, e_ref, ql_ref, qh_ref, kl_ref, attn_ref):
    ts = SEQ_TILE
    s_idx = pl.program_id(1)

    @pl.when(s_idx == 0)
    def _():
        state_ref[...] = jnp.zeros_like(state_ref)
        tail_ref[...] = jnp.zeros_like(tail_ref)
        qh_ref[...] = jnp.zeros_like(qh_ref)

    x = x_ref[0]
    mod = mod_ref[0]
    shift = mod[:, 0:D_MODEL]
    scale = mod[:, D_MODEL:2 * D_MODEL]
    gate = mod[:, 2 * D_MODEL:3 * D_MODEL]

    ms = jnp.mean(x * x, axis=-1, keepdims=True)
    xn = x * lax.rsqrt(ms + EPS) * npre_ref[...]
    h = (xn * (1.0 + scale) + shift).astype(BF16)

    def in_proj(col):
        return _dot(h, w_in_ref[:, col * 1024:(col + 1) * 1024])

    u = in_proj(0)
    part = jnp.concatenate([tail_ref[...], u], axis=0)
    tail_ref[...] = u[ts - POOL_TAIL:ts, :]
    win_sum = []
    for g, w in enumerate(POOL_WINDOWS):
        part = part[:, (POOL_GROUP_WIDTH if g else 0):]
        part = part + pltpu.roll(part, w // 2, 0)
        win_sum.append(part[POOL_TAIL:, 0:POOL_GROUP_WIDTH])
    g_pool = in_proj(1)
    pos = s_idx * ts + lax.broadcasted_iota(jnp.int32, (ts, 1), 0)
    for g, w in enumerate(POOL_WINDOWS):
        lo, hi = g * POOL_GROUP_WIDTH, (g + 1) * POOL_GROUP_WIDTH
        u_g = u[:, lo:hi]
        cnt = jnp.minimum(pos + 1, w).astype(F32)
        d = (win_sum[g] / cnt - u_g).astype(BF16)
        y = _dot(d, pool_w_ref[g]) * pool_scale_ref[:, lo:hi]
        gp = g_pool[:, lo:hi]
        mix_ref[:, lo:hi] = (y * (gp * _sigmoid(gp))).astype(BF16)

    q_ref[...] = in_proj(2)
    lb = lb_ref[...]
    f = lb + (1.0 - lb) * _sigmoid(in_proj(3))
    lf_ref[...] = jnp.log(f) * LOG2E
    k_ref[...] = 1.0 - f
    v_ref[...] = in_proj(4)
    g_ref[...] = in_proj(5)
    lvl = lvl_ref[...]
    hnw = hnw_ref[...]

    heads = [slice(hd * HEAD_DIM, (hd + 1) * HEAD_DIM) for hd in range(HGRN_HEADS)]
    n_chunks = ts // CHUNK
    causal = lvl >= 0

    def split_log_decay(rows):
        return jnp.concatenate(_split_bf16(lf_ref[rows, :], SPLIT), axis=0)

    def chunk_outputs(rows, qe, k_dec, e_last):
        v_c = v_ref[rows, :]
        k_dec_b = k_dec.astype(BF16)
        for hd, hl in enumerate(heads):
            v_t = jnp.transpose(v_c[:, hl]).astype(BF16)
            state_t = state_ref[hd]
            o = _dot_nt(jnp.concatenate([qe[:, hl], attn_ref[hd]], axis=1),
                        jnp.concatenate([state_t.astype(BF16), v_t], axis=1))
            on = o * lax.rsqrt(jnp.mean(o * o, axis=-1, keepdims=True) + EPS) * hnw
            gh = g_ref[rows, hl]
            mix_ref[rows, POOL_WIDTH + hd * HEAD_DIM:POOL_WIDTH + (hd + 1) * HEAD_DIM] = (
                on * (gh * _sigmoid(gh))).astype(BF16)
            state_ref[hd] = state_t * e_last[:, hl] + _dot(v_t, k_dec_b[:, hl])

    def direct_chunk(c):
        rows = slice(c * CHUNK, (c + 1) * CHUNK)
        b_c = bsum_ref[rows, :]
        b_mid = bsum_ref[c * CHUNK + MID_ROW:c * CHUNK + MID_ROW + 1, :]
        b_last = bsum_ref[(c + 1) * CHUNK - 1:(c + 1) * CHUNK, :]
        d = b_c - b_mid
        q_s = q_ref[rows, :] * jnp.exp2(d)
        k_s = k_ref[rows, :] * jnp.exp2(-d)
        q_b = q_s.astype(BF16)
        k_b = k_s.astype(BF16)
        for hd, hl in enumerate(heads):
            attn_ref[hd] = jnp.where(causal, _dot_nt(q_b[:, hl], k_b[:, hl]), 0.0).astype(BF16)
        chunk_outputs(rows, (q_s * jnp.exp2(b_mid)).astype(BF16),
                      k_s * jnp.exp2(b_last - b_mid), jnp.exp2(b_last))

    def robust_chunk(c, carry):
        rows = pl.ds(pl.multiple_of(c * CHUNK, CHUNK), CHUNK)
        sums = _dot(wexp_ref[...], split_log_decay(rows))
        b_ref[...] = sums[0:CHUNK]
        e_ref[...] = jnp.exp2(sums[CHUNK:])

        def b_row(r):
            return b_ref[r:r + 1, :]

        def b_group(r):
            return b_ref[r:r + SUBLANES, :]

        q_c = q_ref[rows, :]
        k_c = k_ref[rows, :]
        ql_ref[0:CHUNK, :] = q_c.astype(BF16)
        kl_ref[0] = k_c.astype(BF16)
        ql_ref[CHUNK:2 * CHUNK, :] = (q_c * (1.0 - k_c)).astype(BF16)
        for level in MXU_LEVELS:
            e_l = e_ref[(level - MXU_LEVELS[0]) * CHUNK:(level - MXU_LEVELS[0] + 1) * CHUNK, :]
            ql_ref[level * CHUNK:(level + 1) * CHUNK, :] = (q_c * e_l).astype(BF16)
            kl_ref[level] = (k_c * e_l).astype(BF16)
        half = CHUNK // 2
        for level in range(FIRST_HALF_LEVEL, NUM_LEVELS + 1):
            t_groups = _t_side_groups(level)
            q_t = jnp.concatenate(
                [q_c[r:r + SUBLANES] * jnp.exp2(b_group(r) - b_row(_ref_row(level, r)))
                 for r in t_groups], axis=0).astype(BF16)
            for hd, hl in enumerate(heads):
                r0 = CHUNK * (level - FIRST_HALF_LEVEL) + (hd % 2) * half
                qh_ref[r0:r0 + half, hl] = q_t[:, hl]
            kl_ref[level] = jnp.concatenate(
                [k_c[r:r + SUBLANES] if r in t_groups else
                 k_c[r:r + SUBLANES] * jnp.exp2(b_row(_ref_row(level, r)) - b_group(r))
                 for r in range(0, CHUNK, SUBLANES)], axis=0).astype(BF16)
        for pair in range(HGRN_HEADS // 2):
            pl2 = slice(pair * 2 * HEAD_DIM, (pair + 1) * 2 * HEAD_DIM)
            a_coarse = {lv: _dot_nt(qh_ref[CHUNK * (lv - FIRST_HALF_LEVEL):CHUNK * (lv - FIRST_HALF_LEVEL + 1), pl2],
                                    kl_ref[lv, :, pl2])
                        for lv in range(FIRST_HALF_LEVEL, NUM_LEVELS + 1)}
            for sub in range(2):
                hl = heads[2 * pair + sub]
                a01 = _dot_nt(ql_ref[0:2 * CHUNK, hl], kl_ref[0, :, hl])
                attn = jnp.where(lvl == 0, a01[0:CHUNK], 0.0)
                attn = jnp.where(lvl == 1, a01[CHUNK:2 * CHUNK], attn)
                for lv in MXU_LEVELS:
                    a_lv = _dot_nt(ql_ref[lv * CHUNK:(lv + 1) * CHUNK, hl], kl_ref[lv, :, hl])
                    attn = jnp.where(lvl == lv, a_lv, attn)
                groups = [attn[r:r + SUBLANES] for r in range(0, CHUNK, SUBLANES)]
                for lv in range(FIRST_HALF_LEVEL, NUM_LEVELS + 1):
                    base = sub * (CHUNK // 2)
                    for i, r in enumerate(_t_side_groups(lv)):
                        groups[r // SUBLANES] = jnp.where(
                            lvl[r:r + SUBLANES] == lv,
                            a_coarse[lv][base + i * SUBLANES:base + (i + 1) * SUBLANES],
                            groups[r // SUBLANES])
                attn_ref[2 * pair + sub] = jnp.concatenate(groups, axis=0).astype(BF16)

        b_c = b_ref[...]
        b_last = b_row(CHUNK - 1)
        chunk_outputs(rows, (q_c * jnp.exp2(b_c)).astype(BF16),
                      k_c * jnp.exp2(b_last - b_c), jnp.exp2(b_last))
        return carry

    unsafe = None
    for c in range(n_chunks):
        rows = slice(c * CHUNK, (c + 1) * CHUNK)
        bsum_ref[rows, :] = _dot(wexp_ref[0:CHUNK, :], split_log_decay(rows))
        b_mid = bsum_ref[c * CHUNK + MID_ROW:c * CHUNK + MID_ROW + 1, :]
        b_last = bsum_ref[(c + 1) * CHUNK - 1:(c + 1) * CHUNK, :]
        ok = jnp.logical_and(-b_mid <= DIRECT_MAX_LOG2, b_mid - b_last <= DIRECT_MAX_LOG2)
        bad = jnp.where(ok, 0.0, 1.0)
        unsafe = bad if unsafe is None else jnp.maximum(unsafe, bad)
    use_direct = jnp.max(unsafe) < 0.5

    @pl.when(use_direct)
    def _():
        for c in range(n_chunks):
            direct_chunk(c)

    @pl.when(jnp.logical_not(use_direct))
    def _():
        lax.fori_loop(0, n_chunks, robust_chunk, 0)

    y = _dot(mix_ref[...], w_out_ref[...])
    yn = y * lax.rsqrt(jnp.mean(y * y, axis=-1, keepdims=True) + EPS) * npost_ref[...]
    out_ref[0] = x + gate * yn


def _layer_call(x, mod, npre, w_in, pool_w, pool_scale, lb, hnw, w_out, npost, wexp, lvl):
    batch, seq, d = x.shape
    ts = SEQ_TILE
    const2 = lambda b, s: (0, 0)
    const3 = lambda b, s: (0, 0, 0)
    once = pl.Buffered(1)
    return pl.pallas_call(
        _layer_kernel,
        grid=(batch, seq // ts),
        in_specs=[
            pl.BlockSpec((1, ts, d), lambda b, s: (b, s, 0)),
            pl.BlockSpec((1, 1, 3 * d), lambda b, s: (b, 0, 0)),
            pl.BlockSpec((1, d), const2),
            pl.BlockSpec((d, IN_WIDTH), const2, pipeline_mode=once),
            pl.BlockSpec(pool_w.shape, const3, pipeline_mode=once),
            pl.BlockSpec((1, POOL_WIDTH), const2),
            pl.BlockSpec((1, HGRN_WIDTH), const2),
            pl.BlockSpec((1, HEAD_DIM), const2),
            pl.BlockSpec((MIX_WIDTH, d), const2, pipeline_mode=once),
            pl.BlockSpec((1, d), const2),
            pl.BlockSpec(wexp.shape, const2),
            pl.BlockSpec(lvl.shape, const2),
        ],
        out_specs=pl.BlockSpec((1, ts, d), lambda b, s: (b, s, 0)),
        out_shape=jax.ShapeDtypeStruct(x.shape, F32),
        scratch_shapes=[
            pltpu.VMEM((HGRN_HEADS, HEAD_DIM, HEAD_DIM), F32),
            pltpu.VMEM((POOL_TAIL, POOL_WIDTH), F32),
            pltpu.VMEM((ts, HGRN_WIDTH), F32),
            pltpu.VMEM((ts, HGRN_WIDTH), F32),
            pltpu.VMEM((ts, HGRN_WIDTH), F32),
            pltpu.VMEM((ts, HGRN_WIDTH), F32),
            pltpu.VMEM((ts, HGRN_WIDTH), F32),
            pltpu.VMEM((ts, MIX_WIDTH), BF16),
            pltpu.VMEM((ts, HGRN_WIDTH), F32),
            pltpu.VMEM((CHUNK, HGRN_WIDTH), F32),
            pltpu.VMEM((len(MXU_LEVELS) * CHUNK, HGRN_WIDTH), F32),
            pltpu.VMEM((FIRST_HALF_LEVEL * CHUNK, HGRN_WIDTH), BF16),
            pltpu.VMEM(((NUM_LEVELS + 1 - FIRST_HALF_LEVEL) * CHUNK, HGRN_WIDTH), BF16),
            pltpu.VMEM((NUM_LEVELS + 1, CHUNK, HGRN_WIDTH), BF16),
            pltpu.VMEM((HGRN_HEADS, CHUNK, CHUNK), BF16),
        ],
        compiler_params=pltpu.CompilerParams(
            dimension_semantics=("arbitrary", "arbitrary"),
            vmem_limit_bytes=VMEM_LIMIT_BYTES),
        name="hybrid_layer",
    )(x, mod, npre, w_in, pool_w, pool_scale, lb, hnw, w_out, npost, wexp, lvl)


def kernel(x, c, norm_pre_w, ada_w, ada_b, w_in, pool_w, pool_scale, hgrn_lower_bounds,
           hgrn_norm_w, w_out, norm_post_w):
    depth = ada_w.shape[0]
    batch = x.shape[0]
    lower = _lower_bounds_call(hgrn_lower_bounds)
    mod = _adaln_call(c, ada_w, ada_b)
    wexp = jnp.asarray(_exponent_matrix(), dtype=BF16)
    lvl = jnp.asarray(_level_map())
    h = x.astype(F32)
    for l in range(depth):
        h = _layer_call(
            h,
            mod[l].reshape(batch, 1, 3 * D_MODEL),
            norm_pre_w[l].astype(F32).reshape(1, D_MODEL),
            w_in[l].astype(BF16),
            pool_w[l].astype(BF16),
            pool_scale[l].astype(F32).reshape(1, POOL_WIDTH),
            lower[l].reshape(1, HGRN_WIDTH),
            hgrn_norm_w[l].astype(F32).reshape(1, HEAD_DIM),
            w_out[l].astype(BF16),
            norm_post_w[l].astype(F32).reshape(1, D_MODEL),
            wexp, lvl)
    return h.astype(x.dtype)
```

```python
import functools

import numpy as np
import jax
import jax.numpy as jnp
from jax import lax
from jax.experimental import pallas as pl
from jax.experimental.pallas import tpu as pltpu

D_MODEL = 1024
CHUNK = 64
POOL_WIDTH = 1024
POOL_WINDOWS = (2, 4, 8, 16)
POOL_GROUP_WIDTH = POOL_WIDTH // len(POOL_WINDOWS)
HGRN_WIDTH = 1024
HEAD_DIM = 128
HGRN_HEADS = HGRN_WIDTH // HEAD_DIM
MIX_WIDTH = POOL_WIDTH + HGRN_WIDTH
IN_WIDTH = 2 * POOL_WIDTH + 4 * HGRN_WIDTH
EPS = 1e-6

SEQ_TILE = 256
POOL_TAIL = 16
assert POOL_WINDOWS == tuple(2 << g for g in range(len(POOL_WINDOWS))) and POOL_TAIL >= POOL_WINDOWS[-1]
NUM_LEVELS = 6
SPLIT = 3
VMEM_LIMIT_BYTES = 56 * 1024 * 1024
SUBLANES = 8
LOG2E = 1.4426950408889634
MID_ROW = CHUNK // 2 - 1
DIRECT_MAX_LOG2 = 110.0
SLOTS = 2

F32 = jnp.float32
BF16 = jnp.bfloat16


def _ref_row(level, row):
    m = 1 << (level - 1)
    return (row // (2 * m)) * (2 * m) + m - 1


def _cumsum_matrix():
    t = np.arange(CHUNK)[:, None]
    j = np.arange(CHUNK)[None, :]
    w = (j <= t).astype(np.float32)
    return np.concatenate([w] * SPLIT, axis=1)


def _level_map():
    t = np.arange(CHUNK)[:, None]
    s = np.arange(CHUNK)[None, :]
    x = t ^ s
    lvl = np.where(x > 0, np.floor(np.log2(np.maximum(x, 1))).astype(np.int32) + 1, 0)
    return np.where(s > t, -1, lvl).astype(np.int32)


def _split_bf16(a, pieces):
    out = []
    rest = a
    for _ in range(pieces):
        p = rest.astype(BF16)
        out.append(p)
        rest = rest - p.astype(F32)
    return out


def _sigmoid(a):
    return 1.0 / (1.0 + jnp.exp(-a))


def _dot(a, b):
    return jnp.dot(a, b, preferred_element_type=F32)


def _dot_nt(a, b):
    return lax.dot_general(a, b, (((1,), (1,)), ((), ())), preferred_element_type=F32)


def _lower_bounds_kernel(lb_ref, out_ref):
    depth = lb_ref.shape[0]
    rows = [lb_ref[l:l + 1, :] for l in range(depth)]
    mx = rows[0]
    for r in rows[1:]:
        mx = jnp.maximum(mx, r)
    ex = [jnp.exp(r - mx) for r in rows]
    den = ex[0]
    for e in ex[1:]:
        den = den + e
    sm = [e / den for e in ex]
    run = sm[0]
    out_ref[0:1, :] = run - sm[0]
    for l in range(1, depth):
        run = run + sm[l]
        out_ref[l:l + 1, :] = run - sm[0]


def _lower_bounds_call(hgrn_lower_bounds):
    return pl.pallas_call(
        _lower_bounds_kernel,
        out_shape=jax.ShapeDtypeStruct(hgrn_lower_bounds.shape, F32),
        name="lower_bounds",
    )(hgrn_lower_bounds.astype(F32))


def _adaln_kernel(c_ref, w_ref, b_ref, mod_ref):
    c = c_ref[...]
    sc = c * _sigmoid(c)
    w = w_ref[0]
    a_hi, a_lo = _split_bf16(sc, 2)
    w_hi, w_lo = _split_bf16(w, 2)
    acc = _dot(a_hi, w_hi) + _dot(a_hi, w_lo) + _dot(a_lo, w_hi)
    mod_ref[0] = acc + b_ref[0]


def _adaln_call(c, ada_w, ada_b):
    depth, d, d3 = ada_w.shape
    batch = c.shape[0]
    n_col = d3 // d
    return pl.pallas_call(
        _adaln_kernel,
        grid=(depth, n_col),
        in_specs=[
            pl.BlockSpec((batch, d), lambda l, j: (0, 0)),
            pl.BlockSpec((1, d, d), lambda l, j: (l, 0, j)),
            pl.BlockSpec((1, 1, d), lambda l, j: (l, 0, j)),
        ],
        out_specs=pl.BlockSpec((1, batch, d), lambda l, j: (l, 0, j)),
        out_shape=jax.ShapeDtypeStruct((depth, batch, d3), F32),
        name="adaln",
    )(c.astype(F32), ada_w.astype(F32), ada_b.astype(F32).reshape(depth, 1, d3))


def _layer_kernel(tiles_per_seq, n_tiles,
                  xf_ref, xb_ref, modf_ref, modb_ref, npre_ref, w_in_ref, pool_w_ref,
                  pool_scale_ref, lb_ref, hnw_ref, w_out_ref, npost_ref, wsum_ref, lvl_ref,
                  out_ref,
                  state_ref, tail_ref, q_ref, k_ref, v_ref, g_ref, bsum_ref, mixp_ref, flag_ref,
                  mixh_ref, attn_ref, b_ref, ql_ref, kl_ref):
    ts = SEQ_TILE
    n_chunks = ts // CHUNK
    step = pl.program_id(0)
    front_tile = jnp.minimum(step, n_tiles - 1)
    back_tile = jnp.maximum(step - 1, 0)
    slot_f = step % SLOTS
    slot_b = (step + 1) % SLOTS

    @pl.when(step == 0)
    def _():
        for ref in (q_ref, k_ref, v_ref, g_ref, bsum_ref, mixp_ref):
            ref[...] = jnp.zeros_like(ref)
        flag_ref[1] = 1

    @pl.when(front_tile % tiles_per_seq == 0)
    def _():
        tail_ref[...] = jnp.zeros_like(tail_ref)

    @pl.when(back_tile % tiles_per_seq == 0)
    def _():
        state_ref[...] = jnp.zeros_like(state_ref)

    lvl = lvl_ref[...]
    causal = lvl >= 0
    hnw = hnw_ref[...]
    heads = [slice(hd * HEAD_DIM, (hd + 1) * HEAD_DIM) for hd in range(HGRN_HEADS)]

    def front():
        x = xf_ref[0]
        mod = modf_ref[0]
        shift = mod[:, 0:D_MODEL]
        scale = mod[:, D_MODEL:2 * D_MODEL]
        ms = jnp.mean(x * x, axis=-1, keepdims=True)
        xn = x * lax.rsqrt(ms + EPS) * npre_ref[...]
        h = (xn * (1.0 + scale) + shift).astype(BF16)

        def in_proj(col):
            return _dot(h, w_in_ref[:, col * 1024:(col + 1) * 1024])

        u = in_proj(0)
        part = jnp.concatenate([tail_ref[...], u], axis=0)
        tail_ref[...] = u[ts - POOL_TAIL:ts, :]
        win_sum = []
        for g, w in enumerate(POOL_WINDOWS):
            part = part[:, (POOL_GROUP_WIDTH if g else 0):]
            part = part + pltpu.roll(part, w // 2, 0)
            win_sum.append(part[POOL_TAIL:, 0:POOL_GROUP_WIDTH])
        g_pool = in_proj(1)
        pos = (front_tile % tiles_per_seq) * ts + lax.broadcasted_iota(jnp.int32, (ts, 1), 0)
        for g, w in enumerate(POOL_WINDOWS):
            lo, hi = g * POOL_GROUP_WIDTH, (g + 1) * POOL_GROUP_WIDTH
            u_g = u[:, lo:hi]
            cnt = jnp.minimum(pos + 1, w).astype(F32)
            d = (win_sum[g] / cnt - u_g).astype(BF16)
            y = _dot(d, pool_w_ref[g]) * pool_scale_ref[:, lo:hi]
            gp = g_pool[:, lo:hi]
            mixp_ref[slot_f, :, lo:hi] = (y * (gp * _sigmoid(gp))).astype(BF16)

        q_ref[slot_f] = in_proj(2)
        lb = lb_ref[...]
        f = lb + (1.0 - lb) * _sigmoid(in_proj(3))
        k_ref[slot_f] = 1.0 - f
        lf = jnp.log(f) * LOG2E
        v_ref[slot_f] = in_proj(4)
        g_ref[slot_f] = in_proj(5)
        unsafe = None
        for c in range(n_chunks):
            lf3 = jnp.concatenate(_split_bf16(lf[c * CHUNK:(c + 1) * CHUNK, :], SPLIT), axis=0)
            b_c = _dot(wsum_ref[...], lf3)
            bsum_ref[slot_f, c * CHUNK:(c + 1) * CHUNK, :] = b_c
            b_mid = b_c[MID_ROW:MID_ROW + 1, :]
            b_last = b_c[CHUNK - 1:CHUNK, :]
            ok = jnp.logical_and(-b_mid <= DIRECT_MAX_LOG2, b_mid - b_last <= DIRECT_MAX_LOG2)
            bad = jnp.where(ok, 0.0, 1.0)
            unsafe = bad if unsafe is None else jnp.maximum(unsafe, bad)
        flag_ref[slot_f] = (jnp.max(unsafe) < 0.5).astype(jnp.int32)

    def chunk_outputs(rows, qe, k_dec, e_last):
        v_c = v_ref[slot_b, rows, :]
        k_dec_b = k_dec.astype(BF16)
        for hd, hl in enumerate(heads):
            v_t = jnp.transpose(v_c[:, hl]).astype(BF16)
            state_t = state_ref[hd]
            o = _dot_nt(jnp.concatenate([qe[:, hl], attn_ref[hd]], axis=1),
                        jnp.concatenate([state_t.astype(BF16), v_t], axis=1))
            on = o * lax.rsqrt(jnp.mean(o * o, axis=-1, keepdims=True) + EPS) * hnw
            gh = g_ref[slot_b, rows, hl]
            mixh_ref[rows, hl] = (on * (gh * _sigmoid(gh))).astype(BF16)
            state_ref[hd] = state_t * e_last[:, hl] + _dot(v_t, k_dec_b[:, hl])

    def direct_chunk(c):
        rows = slice(c * CHUNK, (c + 1) * CHUNK)
        b_c = bsum_ref[slot_b, rows, :]
        b_mid = bsum_ref[slot_b, c * CHUNK + MID_ROW:c * CHUNK + MID_ROW + 1, :]
        b_last = bsum_ref[slot_b, (c + 1) * CHUNK - 1:(c + 1) * CHUNK, :]
        d = b_c - b_mid
        q_s = q_ref[slot_b, rows, :] * jnp.exp2(d)
        k_s = k_ref[slot_b, rows, :] * jnp.exp2(-d)
        q_b = q_s.astype(BF16)
        k_b = k_s.astype(BF16)
        for hd, hl in enumerate(heads):
            attn_ref[hd] = jnp.where(causal, _dot_nt(q_b[:, hl], k_b[:, hl]), 0.0).astype(BF16)
        chunk_outputs(rows, (q_s * jnp.exp2(b_mid)).astype(BF16),
                      k_s * jnp.exp2(b_last - b_mid), jnp.exp2(b_last))

    def robust_chunk(c, carry):
        rows = pl.ds(pl.multiple_of(c * CHUNK, CHUNK), CHUNK)
        b_ref[...] = bsum_ref[slot_b, rows, :]
        b_c = b_ref[...]
        q_c = q_ref[slot_b, rows, :]
        k_c = k_ref[slot_b, rows, :]
        ql_ref[0:CHUNK, :] = q_c.astype(BF16)
        kl_ref[0] = k_c.astype(BF16)
        ql_ref[CHUNK:2 * CHUNK, :] = (q_c * (1.0 - k_c)).astype(BF16)
        row_in_group = lax.broadcasted_iota(jnp.int32, (SUBLANES, 1), 0)
        for level in range(2, NUM_LEVELS + 1):
            m = 1 << (level - 1)
            refs = []
            for r in range(0, CHUNK, SUBLANES):
                if 2 * m >= SUBLANES:
                    refs.append(jnp.broadcast_to(b_ref[_ref_row(level, r):_ref_row(level, r) + 1, :],
                                                 (SUBLANES, HGRN_WIDTH)))
                else:
                    lo_ref = b_ref[_ref_row(level, r):_ref_row(level, r) + 1, :]
                    hi_ref = b_ref[_ref_row(level, r + 2 * m):_ref_row(level, r + 2 * m) + 1, :]
                    refs.append(jnp.where(row_in_group < 2 * m, lo_ref, hi_ref))
            e_l = jnp.exp2(-jnp.abs(b_c - jnp.concatenate(refs, axis=0)))
            ql_ref[level * CHUNK:(level + 1) * CHUNK, :] = (q_c * e_l).astype(BF16)
            kl_ref[level] = (k_c * e_l).astype(BF16)
        for hd, hl in enumerate(heads):
            a01 = _dot_nt(ql_ref[0:2 * CHUNK, hl], kl_ref[0, :, hl])
            attn = jnp.where(lvl == 0, a01[0:CHUNK], 0.0)
            attn = jnp.where(lvl == 1, a01[CHUNK:2 * CHUNK], attn)
            for level in range(2, NUM_LEVELS + 1):
                a_l = _dot_nt(ql_ref[level * CHUNK:(level + 1) * CHUNK, hl], kl_ref[level, :, hl])
                attn = jnp.where(lvl == level, a_l, attn)
            attn_ref[hd] = attn.astype(BF16)
        b_last = b_ref[CHUNK - 1:CHUNK, :]
        chunk_outputs(rows, (q_c * jnp.exp2(b_c)).astype(BF16),
                      k_c * jnp.exp2(b_last - b_c), jnp.exp2(b_last))
        return carry

    def back_epilogue():
        gate = modb_ref[0][:, 2 * D_MODEL:3 * D_MODEL]
        mix = jnp.concatenate([mixp_ref[slot_b], mixh_ref[...]], axis=1)
        y = _dot(mix, w_out_ref[...])
        yn = y * lax.rsqrt(jnp.mean(y * y, axis=-1, keepdims=True) + EPS) * npost_ref[...]
        out_ref[0] = xb_ref[0] + gate * yn

    use_direct = flag_ref[slot_b] == 1

    @pl.when(use_direct)
    def _():
        front()
        for c in range(n_chunks):
            direct_chunk(c)
        back_epilogue()

    @pl.when(jnp.logical_not(use_direct))
    def _():
        front()
        lax.fori_loop(0, n_chunks, robust_chunk, 0)
        back_epilogue()


def _layer_call(x, mod, npre, w_in, pool_w, pool_scale, lb, hnw, w_out, npost, wsum, lvl):
    batch, seq, d = x.shape
    ts = SEQ_TILE
    tiles_per_seq = seq // ts
    n_tiles = batch * tiles_per_seq

    def front_tile(g):
        t = jnp.minimum(g, n_tiles - 1)
        return t // tiles_per_seq, t % tiles_per_seq

    def back_tile(g):
        t = jnp.maximum(g - 1, 0)
        return t // tiles_per_seq, t % tiles_per_seq

    const2 = lambda g: (0, 0)
    const3 = lambda g: (0, 0, 0)
    once = pl.Buffered(1)
    return pl.pallas_call(
        functools.partial(_layer_kernel, tiles_per_seq, n_tiles),
        grid=(n_tiles + 1,),
        in_specs=[
            pl.BlockSpec((1, ts, d), lambda g: (*front_tile(g), 0)),
            pl.BlockSpec((1, ts, d), lambda g: (*back_tile(g), 0)),
            pl.BlockSpec((1, 1, 3 * d), lambda g: (front_tile(g)[0], 0, 0)),
            pl.BlockSpec((1, 1, 3 * d), lambda g: (back_tile(g)[0], 0, 0)),
            pl.BlockSpec((1, d), const2),
            pl.BlockSpec((d, IN_WIDTH), const2, pipeline_mode=once),
            pl.BlockSpec(pool_w.shape, const3, pipeline_mode=once),
            pl.BlockSpec((1, POOL_WIDTH), const2),
            pl.BlockSpec((1, HGRN_WIDTH), const2),
            pl.BlockSpec((1, HEAD_DIM), const2),
            pl.BlockSpec((MIX_WIDTH, d), const2, pipeline_mode=once),
            pl.BlockSpec((1, d), const2),
            pl.BlockSpec(wsum.shape, const2),
            pl.BlockSpec(lvl.shape, const2),
        ],
        out_specs=pl.BlockSpec((1, ts, d), lambda g: (*back_tile(g), 0)),
        out_shape=jax.ShapeDtypeStruct(x.shape, F32),
        scratch_shapes=[
            pltpu.VMEM((HGRN_HEADS, HEAD_DIM, HEAD_DIM), F32),
            pltpu.VMEM((POOL_TAIL, POOL_WIDTH), F32),
            pltpu.VMEM((SLOTS, ts, HGRN_WIDTH), F32),
            pltpu.VMEM((SLOTS, ts, HGRN_WIDTH), F32),
            pltpu.VMEM((SLOTS, ts, HGRN_WIDTH), F32),
            pltpu.VMEM((SLOTS, ts, HGRN_WIDTH), F32),
            pltpu.VMEM((SLOTS, ts, HGRN_WIDTH), F32),
            pltpu.VMEM((SLOTS, ts, POOL_WIDTH), BF16),
            pltpu.SMEM((SLOTS,), jnp.int32),
            pltpu.VMEM((ts, HGRN_WIDTH), BF16),
            pltpu.VMEM((HGRN_HEADS, CHUNK, CHUNK), BF16),
            pltpu.VMEM((CHUNK, HGRN_WIDTH), F32),
            pltpu.VMEM(((NUM_LEVELS + 1) * CHUNK, HGRN_WIDTH), BF16),
            pltpu.VMEM((NUM_LEVELS + 1, CHUNK, HGRN_WIDTH), BF16),
        ],
        compiler_params=pltpu.CompilerParams(
            dimension_semantics=("arbitrary",),
            vmem_limit_bytes=VMEM_LIMIT_BYTES),
        name="hybrid_layer",
    )(x, x, mod, mod, npre, w_in, pool_w, pool_scale, lb, hnw, w_out, npost, wsum, lvl)


def kernel(x, c, norm_pre_w, ada_w, ada_b, w_in, pool_w, pool_scale, hgrn_lower_bounds,
           hgrn_norm_w, w_out, norm_post_w):
    depth = ada_w.shape[0]
    batch = x.shape[0]
    lower = _lower_bounds_call(hgrn_lower_bounds)
    mod = _adaln_call(c, ada_w, ada_b)
    wsum = jnp.asarray(_cumsum_matrix(), dtype=BF16)
    lvl = jnp.asarray(_level_map())
    h = x.astype(F32)
    for l in range(depth):
        h = _layer_call(
            h,
            mod[l].reshape(batch, 1, 3 * D_MODEL),
            norm_pre_w[l].astype(F32).reshape(1, D_MODEL),
            w_in[l].astype(BF16),
            pool_w[l].astype(BF16),
            pool_scale[l].astype(F32).reshape(1, POOL_WIDTH),
            lower[l].reshape(1, HGRN_WIDTH),
            hgrn_norm_w[l].astype(F32).reshape(1, HEAD_DIM),
            w_out[l].astype(BF16),
            norm_post_w[l].astype(F32).reshape(1, D_MODEL),
            wsum, lvl)
    return h.astype(x.dtype)
```

```python
import functools

import numpy as np
import jax
import jax.numpy as jnp
from jax import lax
from jax.experimental import pallas as pl
from jax.experimental.pallas import tpu as pltpu

D_MODEL = 1024
CHUNK = 64
POOL_WIDTH = 1024
POOL_WINDOWS = (2, 4, 8, 16)
POOL_GROUP_WIDTH = POOL_WIDTH // len(POOL_WINDOWS)
HGRN_WIDTH = 1024
HEAD_DIM = 128
HGRN_HEADS = HGRN_WIDTH // HEAD_DIM
MIX_WIDTH = POOL_WIDTH + HGRN_WIDTH
IN_WIDTH = 2 * POOL_WIDTH + 4 * HGRN_WIDTH
EPS = 1e-6

SEQ_TILE = 512
POOL_TAIL = 16
assert POOL_WINDOWS == tuple(2 << g for g in range(len(POOL_WINDOWS))) and POOL_TAIL >= POOL_WINDOWS[-1]
NUM_LEVELS = 6
SPLIT = 3
VMEM_LIMIT_BYTES = 62 * 1024 * 1024
SUBLANES = 8
LOG2E = 1.4426950408889634
MID_ROW = CHUNK // 2 - 1
DIRECT_MAX_LOG2 = 110.0
SLOTS = 2

F32 = jnp.float32
BF16 = jnp.bfloat16


def _ref_row(level, row):
    m = 1 << (level - 1)
    return (row // (2 * m)) * (2 * m) + m - 1


def _cumsum_matrix():
    t = np.arange(CHUNK)[:, None]
    j = np.arange(CHUNK)[None, :]
    w = (j <= t).astype(np.float32)
    return np.concatenate([w] * SPLIT, axis=1)


def _level_map():
    t = np.arange(CHUNK)[:, None]
    s = np.arange(CHUNK)[None, :]
    x = t ^ s
    lvl = np.where(x > 0, np.floor(np.log2(np.maximum(x, 1))).astype(np.int32) + 1, 0)
    return np.where(s > t, -1, lvl).astype(np.int32)


def _split_bf16(a, pieces):
    out = []
    rest = a
    for _ in range(pieces):
        p = rest.astype(BF16)
        out.append(p)
        rest = rest - p.astype(F32)
    return out


def _sigmoid(a):
    return 1.0 / (1.0 + jnp.exp(-a))


def _dot(a, b):
    return jnp.dot(a, b, preferred_element_type=F32)


def _dot_nt(a, b):
    return lax.dot_general(a, b, (((1,), (1,)), ((), ())), preferred_element_type=F32)


def _lower_bounds_kernel(lb_ref, out_ref):
    depth = lb_ref.shape[0]
    rows = [lb_ref[l:l + 1, :] for l in range(depth)]
    mx = rows[0]
    for r in rows[1:]:
        mx = jnp.maximum(mx, r)
    ex = [jnp.exp(r - mx) for r in rows]
    den = ex[0]
    for e in ex[1:]:
        den = den + e
    sm = [e / den for e in ex]
    run = sm[0]
    out_ref[0:1, :] = run - sm[0]
    for l in range(1, depth):
        run = run + sm[l]
        out_ref[l:l + 1, :] = run - sm[0]


def _lower_bounds_call(hgrn_lower_bounds):
    return pl.pallas_call(
        _lower_bounds_kernel,
        out_shape=jax.ShapeDtypeStruct(hgrn_lower_bounds.shape, F32),
        name="lower_bounds",
    )(hgrn_lower_bounds.astype(F32))


def _adaln_kernel(c_ref, w_ref, b_ref, mod_ref):
    c = c_ref[...]
    sc = c * _sigmoid(c)
    w = w_ref[0]
    a_hi, a_lo = _split_bf16(sc, 2)
    w_hi, w_lo = _split_bf16(w, 2)
    acc = _dot(a_hi, w_hi) + _dot(a_hi, w_lo) + _dot(a_lo, w_hi)
    mod_ref[0] = acc + b_ref[0]


def _adaln_call(c, ada_w, ada_b):
    depth, d, d3 = ada_w.shape
    batch = c.shape[0]
    n_col = d3 // d
    return pl.pallas_call(
        _adaln_kernel,
        grid=(depth, n_col),
        in_specs=[
            pl.BlockSpec((batch, d), lambda l, j: (0, 0)),
            pl.BlockSpec((1, d, d), lambda l, j: (l, 0, j)),
            pl.BlockSpec((1, 1, d), lambda l, j: (l, 0, j)),
        ],
        out_specs=pl.BlockSpec((1, batch, d), lambda l, j: (l, 0, j)),
        out_shape=jax.ShapeDtypeStruct((depth, batch, d3), F32),
        name="adaln",
    )(c.astype(F32), ada_w.astype(F32), ada_b.astype(F32).reshape(depth, 1, d3))


def _layer_kernel(tiles_per_seq, n_tiles,
                  xf_ref, xb_ref, modf_ref, modb_ref, npre_ref, w_in_ref, pool_w_ref,
                  pool_scale_ref, lb_ref, hnw_ref, w_out_ref, npost_ref, wsum_ref, lvl_ref,
                  out_ref,
                  state_ref, tail_ref, q_ref, k_ref, v_ref, g_ref, bsum_ref, mixp_ref, flag_ref,
                  mixh_ref, attn_ref, b_ref, ql_ref, kl_ref):
    ts = SEQ_TILE
    n_chunks = ts // CHUNK
    step = pl.program_id(0)
    front_tile = jnp.minimum(step, n_tiles - 1)
    back_tile = jnp.maximum(step - 1, 0)
    slot_f = step % SLOTS
    slot_b = (step + 1) % SLOTS

    @pl.when(step == 0)
    def _():
        for ref in (q_ref, k_ref, v_ref, g_ref, bsum_ref, mixp_ref):
            ref[...] = jnp.zeros_like(ref)
        flag_ref[1] = 1

    @pl.when(front_tile % tiles_per_seq == 0)
    def _():
        tail_ref[...] = jnp.zeros_like(tail_ref)

    @pl.when(back_tile % tiles_per_seq == 0)
    def _():
        state_ref[...] = jnp.zeros_like(state_ref)

    lvl = lvl_ref[...]
    causal = lvl >= 0
    hnw = hnw_ref[...]
    heads = [slice(hd * HEAD_DIM, (hd + 1) * HEAD_DIM) for hd in range(HGRN_HEADS)]

    def front():
        x = xf_ref[0]
        mod = modf_ref[0]
        shift = mod[:, 0:D_MODEL]
        scale = mod[:, D_MODEL:2 * D_MODEL]
        ms = jnp.mean(x * x, axis=-1, keepdims=True)
        xn = x * lax.rsqrt(ms + EPS) * npre_ref[...]
        h = (xn * (1.0 + scale) + shift).astype(BF16)

        def in_proj(col):
            return _dot(h, w_in_ref[:, col * 1024:(col + 1) * 1024])

        u = in_proj(0)
        part = jnp.concatenate([tail_ref[...], u], axis=0)
        tail_ref[...] = u[ts - POOL_TAIL:ts, :]
        win_sum = []
        for g, w in enumerate(POOL_WINDOWS):
            part = part[:, (POOL_GROUP_WIDTH if g else 0):]
            part = part + pltpu.roll(part, w // 2, 0)
            win_sum.append(part[POOL_TAIL:, 0:POOL_GROUP_WIDTH])
        g_pool = in_proj(1)
        pos = (front_tile % tiles_per_seq) * ts + lax.broadcasted_iota(jnp.int32, (ts, 1), 0)
        for g, w in enumerate(POOL_WINDOWS):
            lo, hi = g * POOL_GROUP_WIDTH, (g + 1) * POOL_GROUP_WIDTH
            u_g = u[:, lo:hi]
            cnt = jnp.minimum(pos + 1, w).astype(F32)
            d = (win_sum[g] / cnt - u_g).astype(BF16)
            y = _dot(d, pool_w_ref[g]) * pool_scale_ref[:, lo:hi]
            gp = g_pool[:, lo:hi]
            mixp_ref[slot_f, :, lo:hi] = (y * (gp * _sigmoid(gp))).astype(BF16)

        q_ref[slot_f] = in_proj(2)
        lb = lb_ref[...]
        f = lb + (1.0 - lb) * _sigmoid(in_proj(3))
        k_ref[slot_f] = 1.0 - f
        lf = jnp.log(f) * LOG2E
        v_ref[slot_f] = in_proj(4)
        g_ref[slot_f] = in_proj(5)
        unsafe = None
        for c in range(n_chunks):
            lf3 = jnp.concatenate(_split_bf16(lf[c * CHUNK:(c + 1) * CHUNK, :], SPLIT), axis=0)
            b_c = _dot(wsum_ref[...], lf3)
            bsum_ref[slot_f, c * CHUNK:(c + 1) * CHUNK, :] = b_c
            b_mid = b_c[MID_ROW:MID_ROW + 1, :]
            b_last = b_c[CHUNK - 1:CHUNK, :]
            ok = jnp.logical_and(-b_mid <= DIRECT_MAX_LOG2, b_mid - b_last <= DIRECT_MAX_LOG2)
            bad = jnp.where(ok, 0.0, 1.0)
            unsafe = bad if unsafe is None else jnp.maximum(unsafe, bad)
        flag_ref[slot_f] = (jnp.max(unsafe) < 0.5).astype(jnp.int32)

    def chunk_outputs(rows, qe, k_dec, e_last):
        v_c = v_ref[slot_b, rows, :]
        k_dec_b = k_dec.astype(BF16)
        for hd, hl in enumerate(heads):
            v_t = jnp.transpose(v_c[:, hl]).astype(BF16)
            state_t = state_ref[hd]
            o = _dot_nt(jnp.concatenate([qe[:, hl], attn_ref[hd]], axis=1),
                        jnp.concatenate([state_t.astype(BF16), v_t], axis=1))
            on = o * lax.rsqrt(jnp.mean(o * o, axis=-1, keepdims=True) + EPS) * hnw
            gh = g_ref[slot_b, rows, hl]
            mixh_ref[rows, hl] = (on * (gh * _sigmoid(gh))).astype(BF16)
            state_ref[hd] = state_t * e_last[:, hl] + _dot(v_t, k_dec_b[:, hl])

    def direct_chunk(c):
        rows = slice(c * CHUNK, (c + 1) * CHUNK)
        b_c = bsum_ref[slot_b, rows, :]
        b_mid = bsum_ref[slot_b, c * CHUNK + MID_ROW:c * CHUNK + MID_ROW + 1, :]
        b_last = bsum_ref[slot_b, (c + 1) * CHUNK - 1:(c + 1) * CHUNK, :]
        d = b_c - b_mid
        q_s = q_ref[slot_b, rows, :] * jnp.exp2(d)
        k_s = k_ref[slot_b, rows, :] * jnp.exp2(-d)
        q_b = q_s.astype(BF16)
        k_b = k_s.astype(BF16)
        for hd, hl in enumerate(heads):
            attn_ref[hd] = jnp.where(causal, _dot_nt(q_b[:, hl], k_b[:, hl]), 0.0).astype(BF16)
        chunk_outputs(rows, (q_s * jnp.exp2(b_mid)).astype(BF16),
                      k_s * jnp.exp2(b_last - b_mid), jnp.exp2(b_last))

    def robust_chunk(c, carry):
        rows = pl.ds(pl.multiple_of(c * CHUNK, CHUNK), CHUNK)
        b_ref[...] = bsum_ref[slot_b, rows, :]
        b_c = b_ref[...]
        q_c = q_ref[slot_b, rows, :]
        k_c = k_ref[slot_b, rows, :]
        ql_ref[0:CHUNK, :] = q_c.astype(BF16)
        kl_ref[0] = k_c.astype(BF16)
        ql_ref[CHUNK:2 * CHUNK, :] = (q_c * (1.0 - k_c)).astype(BF16)
        row_in_group = lax.broadcasted_iota(jnp.int32, (SUBLANES, 1), 0)
        for level in range(2, NUM_LEVELS + 1):
            m = 1 << (level - 1)
            refs = []
            for r in range(0, CHUNK, SUBLANES):
                if 2 * m >= SUBLANES:
                    refs.append(jnp.broadcast_to(b_ref[_ref_row(level, r):_ref_row(level, r) + 1, :],
                                                 (SUBLANES, HGRN_WIDTH)))
                else:
                    lo_ref = b_ref[_ref_row(level, r):_ref_row(level, r) + 1, :]
                    hi_ref = b_ref[_ref_row(level, r + 2 * m):_ref_row(level, r + 2 * m) + 1, :]
                    refs.append(jnp.where(row_in_group < 2 * m, lo_ref, hi_ref))
            e_l = jnp.exp2(-jnp.abs(b_c - jnp.concatenate(refs, axis=0)))
            ql_ref[level * CHUNK:(level + 1) * CHUNK, :] = (q_c * e_l).astype(BF16)
            kl_ref[level] = (k_c * e_l).astype(BF16)
        for hd, hl in enumerate(heads):
            a01 = _dot_nt(ql_ref[0:2 * CHUNK, hl], kl_ref[0, :, hl])
            attn = jnp.where(lvl == 0, a01[0:CHUNK], 0.0)
            attn = jnp.where(lvl == 1, a01[CHUNK:2 * CHUNK], attn)
            for level in range(2, NUM_LEVELS + 1):
                a_l = _dot_nt(ql_ref[level * CHUNK:(level + 1) * CHUNK, hl], kl_ref[level, :, hl])
                attn = jnp.where(lvl == level, a_l, attn)
            attn_ref[hd] = attn.astype(BF16)
        b_last = b_ref[CHUNK - 1:CHUNK, :]
        chunk_outputs(rows, (q_c * jnp.exp2(b_c)).astype(BF16),
                      k_c * jnp.exp2(b_last - b_c), jnp.exp2(b_last))
        return carry

    def back_epilogue():
        gate = modb_ref[0][:, 2 * D_MODEL:3 * D_MODEL]
        mix = jnp.concatenate([mixp_ref[slot_b], mixh_ref[...]], axis=1)
        y = _dot(mix, w_out_ref[...])
        yn = y * lax.rsqrt(jnp.mean(y * y, axis=-1, keepdims=True) + EPS) * npost_ref[...]
        out_ref[0] = xb_ref[0] + gate * yn

    use_direct = flag_ref[slot_b] == 1

    @pl.when(use_direct)
    def _():
        front()
        for c in range(n_chunks):
            direct_chunk(c)
        back_epilogue()

    @pl.when(jnp.logical_not(use_direct))
    def _():
        front()
        lax.fori_loop(0, n_chunks, robust_chunk, 0)
        back_epilogue()


def _layer_call(x, mod, npre, w_in, pool_w, pool_scale, lb, hnw, w_out, npost, wsum, lvl):
    batch, seq, d = x.shape
    ts = SEQ_TILE
    tiles_per_seq = seq // ts
    n_tiles = batch * tiles_per_seq

    def front_tile(g):
        t = jnp.minimum(g, n_tiles - 1)
        return t // tiles_per_seq, t % tiles_per_seq

    def back_tile(g):
        t = jnp.maximum(g - 1, 0)
        return t // tiles_per_seq, t % tiles_per_seq

    const2 = lambda g: (0, 0)
    const3 = lambda g: (0, 0, 0)
    once = pl.Buffered(1)
    return pl.pallas_call(
        functools.partial(_layer_kernel, tiles_per_seq, n_tiles),
        grid=(n_tiles + 1,),
        in_specs=[
            pl.BlockSpec((1, ts, d), lambda g: (*front_tile(g), 0)),
            pl.BlockSpec((1, ts, d), lambda g: (*back_tile(g), 0)),
            pl.BlockSpec((1, 1, 3 * d), lambda g: (front_tile(g)[0], 0, 0)),
            pl.BlockSpec((1, 1, 3 * d), lambda g: (back_tile(g)[0], 0, 0)),
            pl.BlockSpec((1, d), const2),
            pl.BlockSpec((d, IN_WIDTH), const2, pipeline_mode=once),
            pl.BlockSpec(pool_w.shape, const3, pipeline_mode=once),
            pl.BlockSpec((1, POOL_WIDTH), const2),
            pl.BlockSpec((1, HGRN_WIDTH), const2),
            pl.BlockSpec((1, HEAD_DIM), const2),
            pl.BlockSpec((MIX_WIDTH, d), const2, pipeline_mode=once),
            pl.BlockSpec((1, d), const2),
            pl.BlockSpec(wsum.shape, const2),
            pl.BlockSpec(lvl.shape, const2),
        ],
        out_specs=pl.BlockSpec((1, ts, d), lambda g: (*back_tile(g), 0)),
        out_shape=jax.ShapeDtypeStruct(x.shape, F32),
        scratch_shapes=[
            pltpu.VMEM((HGRN_HEADS, HEAD_DIM, HEAD_DIM), F32),
            pltpu.VMEM((POOL_TAIL, POOL_WIDTH), F32),
            pltpu.VMEM((SLOTS, ts, HGRN_WIDTH), F32),
            pltpu.VMEM((SLOTS, ts, HGRN_WIDTH), F32),
            pltpu.VMEM((SLOTS, ts, HGRN_WIDTH), F32),
            pltpu.VMEM((SLOTS, ts, HGRN_WIDTH), F32),
            pltpu.VMEM((SLOTS, ts, HGRN_WIDTH), F32),
            pltpu.VMEM((SLOTS, ts, POOL_WIDTH), BF16),
            pltpu.SMEM((SLOTS,), jnp.int32),
            pltpu.VMEM((ts, HGRN_WIDTH), BF16),
            pltpu.VMEM((HGRN_HEADS, CHUNK, CHUNK), BF16),
            pltpu.VMEM((CHUNK, HGRN_WIDTH), F32),
            pltpu.VMEM(((NUM_LEVELS + 1) * CHUNK, HGRN_WIDTH), BF16),
            pltpu.VMEM((NUM_LEVELS + 1, CHUNK, HGRN_WIDTH), BF16),
        ],
        compiler_params=pltpu.CompilerParams(
            dimension_semantics=("arbitrary",),
            vmem_limit_bytes=VMEM_LIMIT_BYTES),
        name="hybrid_layer",
    )(x, x, mod, mod, npre, w_in, pool_w, pool_scale, lb, hnw, w_out, npost, wsum, lvl)


def kernel(x, c, norm_pre_w, ada_w, ada_b, w_in, pool_w, pool_scale, hgrn_lower_bounds,
           hgrn_norm_w, w_out, norm_post_w):
    depth = ada_w.shape[0]
    batch = x.shape[0]
    lower = _lower_bounds_call(hgrn_lower_bounds)
    mod = _adaln_call(c, ada_w, ada_b)
    wsum = jnp.asarray(_cumsum_matrix(), dtype=BF16)
    lvl = jnp.asarray(_level_map())
    h = x.astype(F32)
    for l in range(depth):
        h = _layer_call(
            h,
            mod[l].reshape(batch, 1, 3 * D_MODEL),
            norm_pre_w[l].astype(F32).reshape(1, D_MODEL),
            w_in[l].astype(BF16),
            pool_w[l].astype(BF16),
            pool_scale[l].astype(F32).reshape(1, POOL_WIDTH),
            lower[l].reshape(1, HGRN_WIDTH),
            hgrn_norm_w[l].astype(F32).reshape(1, HEAD_DIM),
            w_out[l].astype(BF16),
            norm_post_w[l].astype(F32).reshape(1, D_MODEL),
            wsum, lvl)
    return h.astype(x.dtype)
```

```python
import functools

import numpy as np
import jax
import jax.numpy as jnp
from jax import lax
from jax.experimental import pallas as pl
from jax.experimental.pallas import tpu as pltpu

D_MODEL = 1024
CHUNK = 64
POOL_WIDTH = 1024
POOL_WINDOWS = (2, 4, 8, 16)
POOL_GROUP_WIDTH = POOL_WIDTH // len(POOL_WINDOWS)
HGRN_WIDTH = 1024
HEAD_DIM = 128
HGRN_HEADS = HGRN_WIDTH // HEAD_DIM
MIX_WIDTH = POOL_WIDTH + HGRN_WIDTH
IN_WIDTH = 2 * POOL_WIDTH + 4 * HGRN_WIDTH
COL_U, COL_GP, COL_Q, COL_F, COL_V, COL_G = range(6)
EPS = 1e-6

SEQ_TILE = 512
POOL_TAIL = 16
assert POOL_WINDOWS == tuple(2 << g for g in range(len(POOL_WINDOWS))) and POOL_TAIL >= POOL_WINDOWS[-1]
NUM_LEVELS = 6
SPLIT = 3
VMEM_LIMIT_BYTES = 62 * 1024 * 1024
SUBLANES = 8
LOG2E = 1.4426950408889634
MID_ROW = CHUNK // 2 - 1
DIRECT_MAX_LOG2 = 110.0
SLOTS = 2

F32 = jnp.float32
BF16 = jnp.bfloat16


def _ref_row(level, row):
    m = 1 << (level - 1)
    return (row // (2 * m)) * (2 * m) + m - 1


def _cumsum_matrix():
    t = np.arange(CHUNK)[:, None]
    j = np.arange(CHUNK)[None, :]
    w = (j <= t).astype(np.float32)
    return np.concatenate([w] * SPLIT, axis=1)


def _level_map():
    t = np.arange(CHUNK)[:, None]
    s = np.arange(CHUNK)[None, :]
    x = t ^ s
    lvl = np.where(x > 0, np.floor(np.log2(np.maximum(x, 1))).astype(np.int32) + 1, 0)
    return np.where(s > t, -1, lvl).astype(np.int32)


def _split_bf16(a, pieces):
    out = []
    rest = a
    for _ in range(pieces):
        p = rest.astype(BF16)
        out.append(p)
        rest = rest - p.astype(F32)
    return out


def _sigmoid(a):
    return 1.0 / (1.0 + jnp.exp(-a))


def _dot(a, b):
    return jnp.dot(a, b, preferred_element_type=F32)


def _dot_nt(a, b):
    return lax.dot_general(a, b, (((1,), (1,)), ((), ())), preferred_element_type=F32)


def _lower_bounds_kernel(lb_ref, out_ref):
    depth = lb_ref.shape[0]
    rows = [lb_ref[l:l + 1, :] for l in range(depth)]
    mx = rows[0]
    for r in rows[1:]:
        mx = jnp.maximum(mx, r)
    ex = [jnp.exp(r - mx) for r in rows]
    den = ex[0]
    for e in ex[1:]:
        den = den + e
    sm = [e / den for e in ex]
    run = sm[0]
    out_ref[0:1, :] = run - sm[0]
    for l in range(1, depth):
        run = run + sm[l]
        out_ref[l:l + 1, :] = run - sm[0]


def _lower_bounds_call(hgrn_lower_bounds):
    return pl.pallas_call(
        _lower_bounds_kernel,
        out_shape=jax.ShapeDtypeStruct(hgrn_lower_bounds.shape, F32),
        name="lower_bounds",
    )(hgrn_lower_bounds.astype(F32))


def _adaln_kernel(c_ref, w_ref, b_ref, mod_ref):
    c = c_ref[...]
    sc = c * _sigmoid(c)
    w = w_ref[0]
    a_hi, a_lo = _split_bf16(sc, 2)
    w_hi, w_lo = _split_bf16(w, 2)
    acc = _dot(a_hi, w_hi) + _dot(a_hi, w_lo) + _dot(a_lo, w_hi)
    mod_ref[0] = acc + b_ref[0]


def _adaln_call(c, ada_w, ada_b):
    depth, d, d3 = ada_w.shape
    batch = c.shape[0]
    n_col = d3 // d
    return pl.pallas_call(
        _adaln_kernel,
        grid=(depth, n_col),
        in_specs=[
            pl.BlockSpec((batch, d), lambda l, j: (0, 0)),
            pl.BlockSpec((1, d, d), lambda l, j: (l, 0, j)),
            pl.BlockSpec((1, 1, d), lambda l, j: (l, 0, j)),
        ],
        out_specs=pl.BlockSpec((1, batch, d), lambda l, j: (l, 0, j)),
        out_shape=jax.ShapeDtypeStruct((depth, batch, d3), F32),
        name="adaln",
    )(c.astype(F32), ada_w.astype(F32), ada_b.astype(F32).reshape(depth, 1, d3))


def _layer_kernel(tiles_per_seq, n_tiles,
                  xf_ref, xb_ref, modf_ref, modb_ref, npre_ref, w_in_ref, pool_w_ref,
                  pool_scale_ref, lb_ref, hnw_ref, w_out_ref, npost_ref, wsum_ref, lvl_ref,
                  out_ref,
                  state_ref, tail_ref, qs_ref, ks_ref, v_ref, gs_ref, emid_ref, erest_ref,
                  mixp_ref, flag_ref, mixh_ref, attn_ref, b_ref, ql_ref, kl_ref):
    ts = SEQ_TILE
    n_chunks = ts // CHUNK
    step = pl.program_id(0)
    front_tile = jnp.minimum(step, n_tiles - 1)
    back_tile = jnp.maximum(step - 1, 0)
    slot_f = step % SLOTS
    slot_b = (step + 1) % SLOTS

    @pl.when(step == 0)
    def _():
        for ref in (qs_ref, ks_ref, v_ref, gs_ref, emid_ref, erest_ref, mixp_ref):
            ref[...] = jnp.zeros_like(ref)
        flag_ref[1] = 1

    @pl.when(front_tile % tiles_per_seq == 0)
    def _():
        tail_ref[...] = jnp.zeros_like(tail_ref)

    @pl.when(back_tile % tiles_per_seq == 0)
    def _():
        state_ref[...] = jnp.zeros_like(state_ref)

    lvl = lvl_ref[...]
    causal = lvl >= 0
    heads = [slice(hd * HEAD_DIM, (hd + 1) * HEAD_DIM) for hd in range(HGRN_HEADS)]

    def pre_norm(x, mod):
        shift = mod[:, 0:D_MODEL]
        scale = mod[:, D_MODEL:2 * D_MODEL]
        ms = jnp.mean(x * x, axis=-1, keepdims=True)
        xn = x * lax.rsqrt(ms + EPS) * npre_ref[...]
        return (xn * (1.0 + scale) + shift).astype(BF16)

    def in_proj(h, col):
        return _dot(h, w_in_ref[:, col * 1024:(col + 1) * 1024])

    def forget_gate(h):
        lb = lb_ref[...]
        f = lb + (1.0 - lb) * _sigmoid(in_proj(h, COL_F))
        return f, jnp.log(f) * LOG2E

    def cumulative_log_decay(lf, c):
        lf3 = jnp.concatenate(_split_bf16(lf[c * CHUNK:(c + 1) * CHUNK, :], SPLIT), axis=0)
        return _dot(wsum_ref[...], lf3)

    def front():
        h = pre_norm(xf_ref[0], modf_ref[0])

        u = in_proj(h, COL_U)
        part = jnp.concatenate([tail_ref[...], u], axis=0)
        tail_ref[...] = u[ts - POOL_TAIL:ts, :]
        win_sum = []
        for g, w in enumerate(POOL_WINDOWS):
            part = part[:, (POOL_GROUP_WIDTH if g else 0):]
            part = part + pltpu.roll(part, w // 2, 0)
            win_sum.append(part[POOL_TAIL:, 0:POOL_GROUP_WIDTH])
        g_pool = in_proj(h, COL_GP)
        pos = (front_tile % tiles_per_seq) * ts + lax.broadcasted_iota(jnp.int32, (ts, 1), 0)
        for g, w in enumerate(POOL_WINDOWS):
            lo, hi = g * POOL_GROUP_WIDTH, (g + 1) * POOL_GROUP_WIDTH
            u_g = u[:, lo:hi]
            cnt = jnp.minimum(pos + 1, w).astype(F32)
            d = (win_sum[g] / cnt - u_g).astype(BF16)
            y = _dot(d, pool_w_ref[g]) * pool_scale_ref[:, lo:hi]
            gp = g_pool[:, lo:hi]
            mixp_ref[slot_f, :, lo:hi] = (y * (gp * _sigmoid(gp))).astype(BF16)

        q = in_proj(h, COL_Q)
        f, lf = forget_gate(h)
        k = 1.0 - f
        v_ref[slot_f] = in_proj(h, COL_V)
        g_out = in_proj(h, COL_G)
        gs_ref[slot_f] = (g_out * _sigmoid(g_out)) * jnp.concatenate([hnw_ref[...]] * HGRN_HEADS, axis=1)
        unsafe = None
        for c in range(n_chunks):
            rows = slice(c * CHUNK, (c + 1) * CHUNK)
            b_c = cumulative_log_decay(lf, c)
            b_mid = b_c[MID_ROW:MID_ROW + 1, :]
            b_last = b_c[CHUNK - 1:CHUNK, :]
            ok = jnp.logical_and(-b_mid <= DIRECT_MAX_LOG2, b_mid - b_last <= DIRECT_MAX_LOG2)
            bad = jnp.where(ok, 0.0, 1.0)
            unsafe = bad if unsafe is None else jnp.maximum(unsafe, bad)
            d = b_c - b_mid
            qs_ref[slot_f, rows, :] = (q[rows, :] * jnp.exp2(d)).astype(BF16)
            ks_ref[slot_f, rows, :] = (k[rows, :] * jnp.exp2(-d)).astype(BF16)
            emid_ref[slot_f, c:c + 1, :] = jnp.exp2(b_mid)
            erest_ref[slot_f, c:c + 1, :] = jnp.exp2(b_last - b_mid)
        flag_ref[slot_f] = (jnp.max(unsafe) < 0.5).astype(jnp.int32)

    def gated_norm_store(rows, hl, o):
        on = o * lax.rsqrt(jnp.mean(o * o, axis=-1, keepdims=True) + EPS)
        mixh_ref[rows, hl] = (on * gs_ref[slot_b, rows, hl]).astype(BF16)

    def direct_chunk(c):
        rows = slice(c * CHUNK, (c + 1) * CHUNK)
        q_s = qs_ref[slot_b, rows, :]
        k_s = ks_ref[slot_b, rows, :]
        for hd, hl in enumerate(heads):
            attn_ref[hd] = jnp.where(causal, _dot_nt(q_s[:, hl], k_s[:, hl]), 0.0).astype(BF16)
        e_mid = emid_ref[slot_b, c:c + 1, :]
        e_rest = erest_ref[slot_b, c:c + 1, :]
        v_c = v_ref[slot_b, rows, :]
        for hd, hl in enumerate(heads):
            v_t = jnp.transpose(v_c[:, hl]).astype(BF16)
            s_mid = state_ref[hd] * e_mid[:, hl]
            o = _dot_nt(jnp.concatenate([q_s[:, hl], attn_ref[hd]], axis=1),
                        jnp.concatenate([s_mid.astype(BF16), v_t], axis=1))
            gated_norm_store(rows, hl, o)
            state_ref[hd] = (s_mid + _dot(v_t, k_s[:, hl])) * e_rest[:, hl]

    def robust_chunk(c, carry):
        rows = pl.ds(pl.multiple_of(c * CHUNK, CHUNK), CHUNK)
        h = pre_norm(xb_ref[0, rows, :], modb_ref[0])
        q_c = in_proj(h, COL_Q)
        f, lf = forget_gate(h)
        k_c = 1.0 - f
        b_ref[...] = cumulative_log_decay(lf, 0)
        b_c = b_ref[...]
        ql_ref[0:CHUNK, :] = q_c.astype(BF16)
        kl_ref[0] = k_c.astype(BF16)
        ql_ref[CHUNK:2 * CHUNK, :] = (q_c * (1.0 - k_c)).astype(BF16)
        row_in_group = lax.broadcasted_iota(jnp.int32, (SUBLANES, 1), 0)
        for level in range(2, NUM_LEVELS + 1):
            m = 1 << (level - 1)
            refs = []
            for r in range(0, CHUNK, SUBLANES):
                if 2 * m >= SUBLANES:
                    refs.append(jnp.broadcast_to(b_ref[_ref_row(level, r):_ref_row(level, r) + 1, :],
                                                 (SUBLANES, HGRN_WIDTH)))
                else:
                    lo_ref = b_ref[_ref_row(level, r):_ref_row(level, r) + 1, :]
                    hi_ref = b_ref[_ref_row(level, r + 2 * m):_ref_row(level, r + 2 * m) + 1, :]
                    refs.append(jnp.where(row_in_group < 2 * m, lo_ref, hi_ref))
            e_l = jnp.exp2(-jnp.abs(b_c - jnp.concatenate(refs, axis=0)))
            ql_ref[level * CHUNK:(level + 1) * CHUNK, :] = (q_c * e_l).astype(BF16)
            kl_ref[level] = (k_c * e_l).astype(BF16)
        for hd, hl in enumerate(heads):
            a01 = _dot_nt(ql_ref[0:2 * CHUNK, hl], kl_ref[0, :, hl])
            attn = jnp.where(lvl == 0, a01[0:CHUNK], 0.0)
            attn = jnp.where(lvl == 1, a01[CHUNK:2 * CHUNK], attn)
            for level in range(2, NUM_LEVELS + 1):
                a_l = _dot_nt(ql_ref[level * CHUNK:(level + 1) * CHUNK, hl], kl_ref[level, :, hl])
                attn = jnp.where(lvl == level, a_l, attn)
            attn_ref[hd] = attn.astype(BF16)
        b_last = b_ref[CHUNK - 1:CHUNK, :]
        q_e = (q_c * jnp.exp2(b_c)).astype(BF16)
        k_dec = (k_c * jnp.exp2(b_last - b_c)).astype(BF16)
        e_last = jnp.exp2(b_last)
        v_c = v_ref[slot_b, rows, :]
        for hd, hl in enumerate(heads):
            v_t = jnp.transpose(v_c[:, hl]).astype(BF16)
            state_t = state_ref[hd]
            o = _dot_nt(jnp.concatenate([q_e[:, hl], attn_ref[hd]], axis=1),
                        jnp.concatenate([state_t.astype(BF16), v_t], axis=1))
            gated_norm_store(rows, hl, o)
            state_ref[hd] = state_t * e_last[:, hl] + _dot(v_t, k_dec[:, hl])
        return carry

    def back_epilogue():
        gate = modb_ref[0][:, 2 * D_MODEL:3 * D_MODEL]
        mix = jnp.concatenate([mixp_ref[slot_b], mixh_ref[...]], axis=1)
        y = _dot(mix, w_out_ref[...])
        yn = y * lax.rsqrt(jnp.mean(y * y, axis=-1, keepdims=True) + EPS) * npost_ref[...]
        out_ref[0] = xb_ref[0] + gate * yn

    use_direct = flag_ref[slot_b] == 1

    @pl.when(use_direct)
    def _():
        front()
        for c in range(n_chunks):
            direct_chunk(c)
        back_epilogue()

    @pl.when(jnp.logical_not(use_direct))
    def _():
        front()
        lax.fori_loop(0, n_chunks, robust_chunk, 0)
        back_epilogue()


def _layer_call(x, mod, npre, w_in, pool_w, pool_scale, lb, hnw, w_out, npost, wsum, lvl):
    batch, seq, d = x.shape
    ts = SEQ_TILE
    tiles_per_seq = seq // ts
    n_tiles = batch * tiles_per_seq
    n_chunks = ts // CHUNK

    def front_tile(g):
        t = jnp.minimum(g, n_tiles - 1)
        return t // tiles_per_seq, t % tiles_per_seq

    def back_tile(g):
        t = jnp.maximum(g - 1, 0)
        return t // tiles_per_seq, t % tiles_per_seq

    const2 = lambda g: (0, 0)
    const3 = lambda g: (0, 0, 0)
    once = pl.Buffered(1)
    return pl.pallas_call(
        functools.partial(_layer_kernel, tiles_per_seq, n_tiles),
        grid=(n_tiles + 1,),
        in_specs=[
            pl.BlockSpec((1, ts, d), lambda g: (*front_tile(g), 0)),
            pl.BlockSpec((1, ts, d), lambda g: (*back_tile(g), 0)),
            pl.BlockSpec((1, 1, 3 * d), lambda g: (front_tile(g)[0], 0, 0)),
            pl.BlockSpec((1, 1, 3 * d), lambda g: (back_tile(g)[0], 0, 0)),
            pl.BlockSpec((1, d), const2),
            pl.BlockSpec((d, IN_WIDTH), const2, pipeline_mode=once),
            pl.BlockSpec(pool_w.shape, const3, pipeline_mode=once),
            pl.BlockSpec((1, POOL_WIDTH), const2),
            pl.BlockSpec((1, HGRN_WIDTH), const2),
            pl.BlockSpec((1, HEAD_DIM), const2),
            pl.BlockSpec((MIX_WIDTH, d), const2, pipeline_mode=once),
            pl.BlockSpec((1, d), const2),
            pl.BlockSpec(wsum.shape, const2),
            pl.BlockSpec(lvl.shape, const2),
        ],
        out_specs=pl.BlockSpec((1, ts, d), lambda g: (*back_tile(g), 0)),
        out_shape=jax.ShapeDtypeStruct(x.shape, F32),
        scratch_shapes=[
            pltpu.VMEM((HGRN_HEADS, HEAD_DIM, HEAD_DIM), F32),
            pltpu.VMEM((POOL_TAIL, POOL_WIDTH), F32),
            pltpu.VMEM((SLOTS, ts, HGRN_WIDTH), BF16),
            pltpu.VMEM((SLOTS, ts, HGRN_WIDTH), BF16),
            pltpu.VMEM((SLOTS, ts, HGRN_WIDTH), F32),
            pltpu.VMEM((SLOTS, ts, HGRN_WIDTH), F32),
            pltpu.VMEM((SLOTS, n_chunks, HGRN_WIDTH), F32),
            pltpu.VMEM((SLOTS, n_chunks, HGRN_WIDTH), F32),
            pltpu.VMEM((SLOTS, ts, POOL_WIDTH), BF16),
            pltpu.SMEM((SLOTS,), jnp.int32),
            pltpu.VMEM((ts, HGRN_WIDTH), BF16),
            pltpu.VMEM((HGRN_HEADS, CHUNK, CHUNK), BF16),
            pltpu.VMEM((CHUNK, HGRN_WIDTH), F32),
            pltpu.VMEM(((NUM_LEVELS + 1) * CHUNK, HGRN_WIDTH), BF16),
            pltpu.VMEM((NUM_LEVELS + 1, CHUNK, HGRN_WIDTH), BF16),
        ],
        compiler_params=pltpu.CompilerParams(
            dimension_semantics=("arbitrary",),
            vmem_limit_bytes=VMEM_LIMIT_BYTES),
        name="hybrid_layer",
    )(x, x, mod, mod, npre, w_in, pool_w, pool_scale, lb, hnw, w_out, npost, wsum, lvl)


def kernel(x, c, norm_pre_w, ada_w, ada_b, w_in, pool_w, pool_scale, hgrn_lower_bounds,
           hgrn_norm_w, w_out, norm_post_w):
    depth = ada_w.shape[0]
    batch = x.shape[0]
    lower = _lower_bounds_call(hgrn_lower_bounds)
    mod = _adaln_call(c, ada_w, ada_b)
    wsum = jnp.asarray(_cumsum_matrix(), dtype=BF16)
    lvl = jnp.asarray(_level_map())
    h = x.astype(F32)
    for l in range(depth):
        h = _layer_call(
            h,
            mod[l].reshape(batch, 1, 3 * D_MODEL),
            norm_pre_w[l].astype(F32).reshape(1, D_MODEL),
            w_in[l].astype(BF16),
            pool_w[l].astype(BF16),
            pool_scale[l].astype(F32).reshape(1, POOL_WIDTH),
            lower[l].reshape(1, HGRN_WIDTH),
            hgrn_norm_w[l].astype(F32).reshape(1, HEAD_DIM),
            w_out[l].astype(BF16),
            norm_post_w[l].astype(F32).reshape(1, D_MODEL),
            wsum, lvl)
    return h.astype(x.dtype)
```

```python
import functools

import numpy as np
import jax
import jax.numpy as jnp
from jax import lax
from jax.experimental import pallas as pl
from jax.experimental.pallas import tpu as pltpu

D_MODEL = 1024
CHUNK = 64
POOL_WIDTH = 1024
POOL_WINDOWS = (2, 4, 8, 16)
POOL_GROUP_WIDTH = POOL_WIDTH // len(POOL_WINDOWS)
HGRN_WIDTH = 1024
HEAD_DIM = 128
HGRN_HEADS = HGRN_WIDTH // HEAD_DIM
MIX_WIDTH = POOL_WIDTH + HGRN_WIDTH
IN_WIDTH = 2 * POOL_WIDTH + 4 * HGRN_WIDTH
EPS = 1e-6

SEQ_TILE = 512
POOL_TAIL = 16
assert POOL_WINDOWS == tuple(2 << g for g in range(len(POOL_WINDOWS))) and POOL_TAIL >= POOL_WINDOWS[-1]
NUM_LEVELS = 6
SPLIT = 3
VMEM_LIMIT_BYTES = 61 * 1024 * 1024 + 512 * 1024
SUBLANES = 8
LOG2E = 1.4426950408889634
MID_ROW = CHUNK // 2 - 1
DIRECT_MAX_LOG2 = 110.0
SLOTS = 2

F32 = jnp.float32
BF16 = jnp.bfloat16


def _ref_row(level, row):
    m = 1 << (level - 1)
    return (row // (2 * m)) * (2 * m) + m - 1


def _cumsum_matrix():
    t = np.arange(CHUNK)[:, None]
    j = np.arange(CHUNK)[None, :]
    w = (j <= t).astype(np.float32)
    return np.concatenate([w] * SPLIT, axis=1)


def _level_map():
    t = np.arange(CHUNK)[:, None]
    s = np.arange(CHUNK)[None, :]
    x = t ^ s
    lvl = np.where(x > 0, np.floor(np.log2(np.maximum(x, 1))).astype(np.int32) + 1, 0)
    return np.where(s > t, -1, lvl).astype(np.int32)


def _split_bf16(a, pieces):
    out = []
    rest = a
    for _ in range(pieces):
        p = rest.astype(BF16)
        out.append(p)
        rest = rest - p.astype(F32)
    return out


def _sigmoid(a):
    return 1.0 / (1.0 + jnp.exp(-a))


def _dot(a, b):
    return jnp.dot(a, b, preferred_element_type=F32)


def _dot_nt(a, b):
    return lax.dot_general(a, b, (((1,), (1,)), ((), ())), preferred_element_type=F32)


def _lower_bounds_kernel(lb_ref, out_ref):
    depth = lb_ref.shape[0]
    rows = [lb_ref[l:l + 1, :] for l in range(depth)]
    mx = rows[0]
    for r in rows[1:]:
        mx = jnp.maximum(mx, r)
    ex = [jnp.exp(r - mx) for r in rows]
    den = ex[0]
    for e in ex[1:]:
        den = den + e
    sm = [e / den for e in ex]
    run = sm[0]
    out_ref[0:1, :] = run - sm[0]
    for l in range(1, depth):
        run = run + sm[l]
        out_ref[l:l + 1, :] = run - sm[0]


def _lower_bounds_call(hgrn_lower_bounds):
    return pl.pallas_call(
        _lower_bounds_kernel,
        out_shape=jax.ShapeDtypeStruct(hgrn_lower_bounds.shape, F32),
        name="lower_bounds",
    )(hgrn_lower_bounds.astype(F32))


def _adaln_kernel(c_ref, w_ref, b_ref, mod_ref):
    c = c_ref[...]
    sc = c * _sigmoid(c)
    w = w_ref[0]
    a_hi, a_lo = _split_bf16(sc, 2)
    w_hi, w_lo = _split_bf16(w, 2)
    acc = _dot(a_hi, w_hi) + _dot(a_hi, w_lo) + _dot(a_lo, w_hi)
    mod_ref[0] = acc + b_ref[0]


def _adaln_call(c, ada_w, ada_b):
    depth, d, d3 = ada_w.shape
    batch = c.shape[0]
    n_col = d3 // d
    return pl.pallas_call(
        _adaln_kernel,
        grid=(depth, n_col),
        in_specs=[
            pl.BlockSpec((batch, d), lambda l, j: (0, 0)),
            pl.BlockSpec((1, d, d), lambda l, j: (l, 0, j)),
            pl.BlockSpec((1, 1, d), lambda l, j: (l, 0, j)),
        ],
        out_specs=pl.BlockSpec((1, batch, d), lambda l, j: (l, 0, j)),
        out_shape=jax.ShapeDtypeStruct((depth, batch, d3), F32),
        name="adaln",
    )(c.astype(F32), ada_w.astype(F32), ada_b.astype(F32).reshape(depth, 1, d3))


def _layer_kernel(tiles_per_seq, n_tiles,
                  xf_ref, xb_ref, modf_ref, modb_ref, npre_ref, w_in_ref, pool_w_ref,
                  pool_scale_ref, lb_ref, hnw_ref, w_out_ref, npost_ref, wsum_ref, lvl_ref,
                  out_ref,
                  state_ref, tail_ref, h_ref, q_ref, k_ref, v_ref, g_ref, bsum_ref, mixp_ref, flag_ref,
                  mixh_ref, attn_ref, b_ref, ql_ref, kl_ref):
    ts = SEQ_TILE
    n_chunks = ts // CHUNK
    step = pl.program_id(0)
    front_tile = jnp.minimum(step, n_tiles - 1)
    back_tile = jnp.maximum(step - 1, 0)
    slot_f = step % SLOTS
    slot_b = (step + 1) % SLOTS

    def pre_norm(x, mod):
        shift = mod[:, 0:D_MODEL]
        scale = mod[:, D_MODEL:2 * D_MODEL]
        ms = jnp.mean(x * x, axis=-1, keepdims=True)
        xn = x * lax.rsqrt(ms + EPS) * npre_ref[...]
        return (xn * (1.0 + scale) + shift).astype(BF16)

    @pl.when(step == 0)
    def _():
        for ref in (q_ref, k_ref, v_ref, g_ref, bsum_ref, mixp_ref):
            ref[...] = jnp.zeros_like(ref)
        flag_ref[1] = 1
        h_ref[0] = pre_norm(xb_ref[0], modb_ref[0])

    @pl.when(front_tile % tiles_per_seq == 0)
    def _():
        tail_ref[...] = jnp.zeros_like(tail_ref)

    @pl.when(back_tile % tiles_per_seq == 0)
    def _():
        state_ref[...] = jnp.zeros_like(state_ref)

    lvl = lvl_ref[...]
    causal = lvl >= 0
    hnw = hnw_ref[...]
    heads = [slice(hd * HEAD_DIM, (hd + 1) * HEAD_DIM) for hd in range(HGRN_HEADS)]

    def front():
        h = h_ref[slot_f]

        def in_proj(col):
            return _dot(h, w_in_ref[:, col * 1024:(col + 1) * 1024])

        u = in_proj(0)
        part = jnp.concatenate([tail_ref[...], u], axis=0)
        tail_ref[...] = u[ts - POOL_TAIL:ts, :]
        pos = (front_tile % tiles_per_seq) * ts + lax.broadcasted_iota(jnp.int32, (ts, 1), 0)
        pool_in = []
        for g, w in enumerate(POOL_WINDOWS):
            part = part[:, (POOL_GROUP_WIDTH if g else 0):]
            part = part + pltpu.roll(part, w // 2, 0)
            cnt = jnp.minimum(pos + 1, w).astype(F32)
            u_g = u[:, g * POOL_GROUP_WIDTH:(g + 1) * POOL_GROUP_WIDTH]
            pool_in.append((part[POOL_TAIL:, 0:POOL_GROUP_WIDTH] / cnt - u_g).astype(BF16))
        g_pool = in_proj(1)
        for g in range(len(POOL_WINDOWS)):
            lo, hi = g * POOL_GROUP_WIDTH, (g + 1) * POOL_GROUP_WIDTH
            y = _dot(pool_in[g], pool_w_ref[g]) * pool_scale_ref[:, lo:hi]
            gp = g_pool[:, lo:hi]
            mixp_ref[slot_f, :, lo:hi] = (y * (gp * _sigmoid(gp))).astype(BF16)

        q_ref[slot_f] = in_proj(2)
        lb = lb_ref[...]
        f = lb + (1.0 - lb) * _sigmoid(in_proj(3))
        k_ref[slot_f] = 1.0 - f
        lf = jnp.log(f) * LOG2E
        v_ref[slot_f] = in_proj(4).astype(BF16)
        g_ref[slot_f] = in_proj(5)
        unsafe = None
        for c in range(n_chunks):
            lf3 = jnp.concatenate(_split_bf16(lf[c * CHUNK:(c + 1) * CHUNK, :], SPLIT), axis=0)
            b_c = _dot(wsum_ref[...], lf3)
            bsum_ref[slot_f, c * CHUNK:(c + 1) * CHUNK, :] = b_c
            b_mid = b_c[MID_ROW:MID_ROW + 1, :]
            b_last = b_c[CHUNK - 1:CHUNK, :]
            ok = jnp.logical_and(-b_mid <= DIRECT_MAX_LOG2, b_mid - b_last <= DIRECT_MAX_LOG2)
            bad = jnp.where(ok, 0.0, 1.0)
            unsafe = bad if unsafe is None else jnp.maximum(unsafe, bad)
        flag_ref[slot_f] = (jnp.max(unsafe) < 0.5).astype(jnp.int32)
        h_ref[slot_b] = pre_norm(xf_ref[0], modf_ref[0])

    def chunk_outputs(rows, qe, k_dec, e_last):
        v_c = v_ref[slot_b, rows, :]
        k_dec_b = k_dec.astype(BF16)
        for hd, hl in enumerate(heads):
            v_t = jnp.transpose(v_c[:, hl].astype(F32)).astype(BF16)
            state_t = state_ref[hd]
            o = _dot_nt(jnp.concatenate([qe[:, hl], attn_ref[hd]], axis=1),
                        jnp.concatenate([state_t.astype(BF16), v_t], axis=1))
            on = o * lax.rsqrt(jnp.mean(o * o, axis=-1, keepdims=True) + EPS) * hnw
            gh = g_ref[slot_b, rows, hl]
            mixh_ref[rows, hl] = (on * (gh * _sigmoid(gh))).astype(BF16)
            state_ref[hd] = state_t * e_last[:, hl] + _dot(v_t, k_dec_b[:, hl])

    def direct_chunk(c):
        rows = slice(c * CHUNK, (c + 1) * CHUNK)
        b_c = bsum_ref[slot_b, rows, :]
        b_mid = bsum_ref[slot_b, c * CHUNK + MID_ROW:c * CHUNK + MID_ROW + 1, :]
        b_last = bsum_ref[slot_b, (c + 1) * CHUNK - 1:(c + 1) * CHUNK, :]
        d = b_c - b_mid
        q_s = q_ref[slot_b, rows, :] * jnp.exp2(d)
        k_s = k_ref[slot_b, rows, :] * jnp.exp2(-d)
        q_b = q_s.astype(BF16)
        k_b = k_s.astype(BF16)
        for hd, hl in enumerate(heads):
            attn_ref[hd] = jnp.where(causal, _dot_nt(q_b[:, hl], k_b[:, hl]), 0.0).astype(BF16)
        chunk_outputs(rows, (q_s * jnp.exp2(b_mid)).astype(BF16),
                      k_s * jnp.exp2(b_last - b_mid), jnp.exp2(b_last))

    def robust_chunk(c, carry):
        rows = pl.ds(pl.multiple_of(c * CHUNK, CHUNK), CHUNK)
        b_ref[...] = bsum_ref[slot_b, rows, :]
        b_c = b_ref[...]
        q_c = q_ref[slot_b, rows, :]
        k_c = k_ref[slot_b, rows, :]
        ql_ref[0:CHUNK, :] = q_c.astype(BF16)
        kl_ref[0] = k_c.astype(BF16)
        ql_ref[CHUNK:2 * CHUNK, :] = (q_c * (1.0 - k_c)).astype(BF16)
        row_in_group = lax.broadcasted_iota(jnp.int32, (SUBLANES, 1), 0)
        for level in range(2, NUM_LEVELS + 1):
            m = 1 << (level - 1)
            refs = []
            for r in range(0, CHUNK, SUBLANES):
                if 2 * m >= SUBLANES:
                    refs.append(jnp.broadcast_to(b_ref[_ref_row(level, r):_ref_row(level, r) + 1, :],
                                                 (SUBLANES, HGRN_WIDTH)))
                else:
                    lo_ref = b_ref[_ref_row(level, r):_ref_row(level, r) + 1, :]
                    hi_ref = b_ref[_ref_row(level, r + 2 * m):_ref_row(level, r + 2 * m) + 1, :]
                    refs.append(jnp.where(row_in_group < 2 * m, lo_ref, hi_ref))
            e_l = jnp.exp2(-jnp.abs(b_c - jnp.concatenate(refs, axis=0)))
            ql_ref[level * CHUNK:(level + 1) * CHUNK, :] = (q_c * e_l).astype(BF16)
            kl_ref[level] = (k_c * e_l).astype(BF16)
        for hd, hl in enumerate(heads):
            a01 = _dot_nt(ql_ref[0:2 * CHUNK, hl], kl_ref[0, :, hl])
            attn = jnp.where(lvl == 0, a01[0:CHUNK], 0.0)
            attn = jnp.where(lvl == 1, a01[CHUNK:2 * CHUNK], attn)
            for level in range(2, NUM_LEVELS + 1):
                a_l = _dot_nt(ql_ref[level * CHUNK:(level + 1) * CHUNK, hl], kl_ref[level, :, hl])
                attn = jnp.where(lvl == level, a_l, attn)
            attn_ref[hd] = attn.astype(BF16)
        b_last = b_ref[CHUNK - 1:CHUNK, :]
        chunk_outputs(rows, (q_c * jnp.exp2(b_c)).astype(BF16),
                      k_c * jnp.exp2(b_last - b_c), jnp.exp2(b_last))
        return carry

    def back_epilogue():
        gate = modb_ref[0][:, 2 * D_MODEL:3 * D_MODEL]
        mix = jnp.concatenate([mixp_ref[slot_b], mixh_ref[...]], axis=1)
        y = _dot(mix, w_out_ref[...])
        yn = y * lax.rsqrt(jnp.mean(y * y, axis=-1, keepdims=True) + EPS) * npost_ref[...]
        out_ref[0] = xb_ref[0] + gate * yn

    use_direct = flag_ref[slot_b] == 1

    @pl.when(use_direct)
    def _():
        front()
        for c in range(n_chunks):
            direct_chunk(c)
        back_epilogue()

    @pl.when(jnp.logical_not(use_direct))
    def _():
        front()
        lax.fori_loop(0, n_chunks, robust_chunk, 0)
        back_epilogue()


def _layer_call(x, mod, npre, w_in, pool_w, pool_scale, lb, hnw, w_out, npost, wsum, lvl):
    batch, seq, d = x.shape
    ts = SEQ_TILE
    tiles_per_seq = seq // ts
    n_tiles = batch * tiles_per_seq

    def ahead_tile(g):
        t = jnp.minimum(g + 1, n_tiles - 1)
        return t // tiles_per_seq, t % tiles_per_seq

    def back_tile(g):
        t = jnp.maximum(g - 1, 0)
        return t // tiles_per_seq, t % tiles_per_seq

    const2 = lambda g: (0, 0)
    const3 = lambda g: (0, 0, 0)
    once = pl.Buffered(1)
    return pl.pallas_call(
        functools.partial(_layer_kernel, tiles_per_seq, n_tiles),
        grid=(n_tiles + 1,),
        in_specs=[
            pl.BlockSpec((1, ts, d), lambda g: (*ahead_tile(g), 0)),
            pl.BlockSpec((1, ts, d), lambda g: (*back_tile(g), 0)),
            pl.BlockSpec((1, 1, 3 * d), lambda g: (ahead_tile(g)[0], 0, 0)),
            pl.BlockSpec((1, 1, 3 * d), lambda g: (back_tile(g)[0], 0, 0)),
            pl.BlockSpec((1, d), const2),
            pl.BlockSpec((d, IN_WIDTH), const2, pipeline_mode=once),
            pl.BlockSpec(pool_w.shape, const3, pipeline_mode=once),
            pl.BlockSpec((1, POOL_WIDTH), const2),
            pl.BlockSpec((1, HGRN_WIDTH), const2),
            pl.BlockSpec((1, HEAD_DIM), const2),
            pl.BlockSpec((MIX_WIDTH, d), const2, pipeline_mode=once),
            pl.BlockSpec((1, d), const2),
            pl.BlockSpec(wsum.shape, const2),
            pl.BlockSpec(lvl.shape, const2),
        ],
        out_specs=pl.BlockSpec((1, ts, d), lambda g: (*back_tile(g), 0)),
        out_shape=jax.ShapeDtypeStruct(x.shape, F32),
        scratch_shapes=[
            pltpu.VMEM((HGRN_HEADS, HEAD_DIM, HEAD_DIM), F32),
            pltpu.VMEM((POOL_TAIL, POOL_WIDTH), F32),
            pltpu.VMEM((SLOTS, ts, D_MODEL), BF16),
            pltpu.VMEM((SLOTS, ts, HGRN_WIDTH), F32),
            pltpu.VMEM((SLOTS, ts, HGRN_WIDTH), F32),
            pltpu.VMEM((SLOTS, ts, HGRN_WIDTH), BF16),
            pltpu.VMEM((SLOTS, ts, HGRN_WIDTH), F32),
            pltpu.VMEM((SLOTS, ts, HGRN_WIDTH), F32),
            pltpu.VMEM((SLOTS, ts, POOL_WIDTH), BF16),
            pltpu.SMEM((SLOTS,), jnp.int32),
            pltpu.VMEM((ts, HGRN_WIDTH), BF16),
            pltpu.VMEM((HGRN_HEADS, CHUNK, CHUNK), BF16),
            pltpu.VMEM((CHUNK, HGRN_WIDTH), F32),
            pltpu.VMEM(((NUM_LEVELS + 1) * CHUNK, HGRN_WIDTH), BF16),
            pltpu.VMEM((NUM_LEVELS + 1, CHUNK, HGRN_WIDTH), BF16),
        ],
        compiler_params=pltpu.CompilerParams(
            dimension_semantics=("arbitrary",),
            vmem_limit_bytes=VMEM_LIMIT_BYTES),
        name="hybrid_layer",
    )(x, x, mod, mod, npre, w_in, pool_w, pool_scale, lb, hnw, w_out, npost, wsum, lvl)


def kernel(x, c, norm_pre_w, ada_w, ada_b, w_in, pool_w, pool_scale, hgrn_lower_bounds,
           hgrn_norm_w, w_out, norm_post_w):
    depth = ada_w.shape[0]
    batch = x.shape[0]
    lower = _lower_bounds_call(hgrn_lower_bounds)
    mod = _adaln_call(c, ada_w, ada_b)
    wsum = jnp.asarray(_cumsum_matrix(), dtype=BF16)
    lvl = jnp.asarray(_level_map())
    h = x.astype(F32)
    for l in range(depth):
        h = _layer_call(
            h,
            mod[l].reshape(batch, 1, 3 * D_MODEL),
            norm_pre_w[l].astype(F32).reshape(1, D_MODEL),
            w_in[l].astype(BF16),
            pool_w[l].astype(BF16),
            pool_scale[l].astype(F32).reshape(1, POOL_WIDTH),
            lower[l].reshape(1, HGRN_WIDTH),
            hgrn_norm_w[l].astype(F32).reshape(1, HEAD_DIM),
            w_out[l].astype(BF16),
            norm_post_w[l].astype(F32).reshape(1, D_MODEL),
            wsum, lvl)
    return h.astype(x.dtype)
```

```python
import functools

import numpy as np
import jax
import jax.numpy as jnp
from jax import lax
from jax.experimental import pallas as pl
from jax.experimental.pallas import tpu as pltpu

D_MODEL = 1024
CHUNK = 64
POOL_WIDTH = 1024
POOL_WINDOWS = (2, 4, 8, 16)
POOL_GROUP_WIDTH = POOL_WIDTH // len(POOL_WINDOWS)
HGRN_WIDTH = 1024
HEAD_DIM = 128
HGRN_HEADS = HGRN_WIDTH // HEAD_DIM
MIX_WIDTH = POOL_WIDTH + HGRN_WIDTH
IN_WIDTH = 2 * POOL_WIDTH + 4 * HGRN_WIDTH
SECTION_WIDTHS = (POOL_WIDTH, POOL_WIDTH, HGRN_WIDTH, HGRN_WIDTH, HGRN_WIDTH, HGRN_WIDTH)
SECTION_STARTS = tuple(int(v) for v in np.cumsum((0,) + SECTION_WIDTHS))
assert SECTION_STARTS[-1] == IN_WIDTH
SEC_POOL_U, SEC_POOL_GATE, SEC_Q, SEC_FORGET, SEC_V, SEC_OUT_GATE = range(len(SECTION_WIDTHS))
EPS = 1e-6

SEQ_TILE = 512
POOL_TAIL = 16
assert POOL_WINDOWS == tuple(2 << g for g in range(len(POOL_WINDOWS))) and POOL_TAIL >= POOL_WINDOWS[-1]
NUM_LEVELS = 6
SPLIT = 3
VMEM_V7X_BYTES = 64 * 1024 * 1024
VMEM_LIMIT_BYTES = VMEM_V7X_BYTES - 2 * 1024 * 1024
SUBLANES = 8
LOG2E = 1.4426950408889634
MID_ROW = CHUNK // 2 - 1
DIRECT_MAX_LOG2 = 110.0
SLOTS = 2

F32 = jnp.float32
BF16 = jnp.bfloat16


def _ref_row(level, row):
    m = 1 << (level - 1)
    return (row // (2 * m)) * (2 * m) + m - 1


def _cumsum_matrix():
    t = np.arange(CHUNK)[:, None]
    j = np.arange(CHUNK)[None, :]
    w = (j <= t).astype(np.float32)
    return np.concatenate([w] * SPLIT, axis=1)


def _level_map():
    t = np.arange(CHUNK)[:, None]
    s = np.arange(CHUNK)[None, :]
    x = t ^ s
    lvl = np.where(x > 0, np.floor(np.log2(np.maximum(x, 1))).astype(np.int32) + 1, 0)
    return np.where(s > t, -1, lvl).astype(np.int32)


def _split_bf16(a, pieces):
    out = []
    rest = a
    for _ in range(pieces):
        p = rest.astype(BF16)
        out.append(p)
        rest = rest - p.astype(F32)
    return out


def _sigmoid(a):
    return 1.0 / (1.0 + jnp.exp(-a))


def _dot(a, b):
    return jnp.dot(a, b, preferred_element_type=F32)


def _dot_nt(a, b):
    return lax.dot_general(a, b, (((1,), (1,)), ((), ())), preferred_element_type=F32)


def _lower_bounds_kernel(lb_ref, out_ref):
    depth = lb_ref.shape[0]
    rows = [lb_ref[l:l + 1, :] for l in range(depth)]
    mx = rows[0]
    for r in rows[1:]:
        mx = jnp.maximum(mx, r)
    ex = [jnp.exp(r - mx) for r in rows]
    den = ex[0]
    for e in ex[1:]:
        den = den + e
    sm = [e / den for e in ex]
    run = sm[0]
    out_ref[0:1, :] = run - sm[0]
    for l in range(1, depth):
        run = run + sm[l]
        out_ref[l:l + 1, :] = run - sm[0]


def _lower_bounds_call(hgrn_lower_bounds):
    return pl.pallas_call(
        _lower_bounds_kernel,
        out_shape=jax.ShapeDtypeStruct(hgrn_lower_bounds.shape, F32),
        name="lower_bounds",
    )(hgrn_lower_bounds.astype(F32))


def _adaln_kernel(c_ref, w_ref, b_ref, mod_ref):
    c = c_ref[...]
    sc = c * _sigmoid(c)
    w = w_ref[0]
    a_hi, a_lo = _split_bf16(sc, 2)
    w_hi, w_lo = _split_bf16(w, 2)
    acc = _dot(a_hi, w_hi) + _dot(a_hi, w_lo) + _dot(a_lo, w_hi)
    mod_ref[0] = acc + b_ref[0]


def _adaln_call(c, ada_w, ada_b):
    depth, d, d3 = ada_w.shape
    batch = c.shape[0]
    n_col = d3 // d
    return pl.pallas_call(
        _adaln_kernel,
        grid=(depth, n_col),
        in_specs=[
            pl.BlockSpec((batch, d), lambda l, j: (0, 0)),
            pl.BlockSpec((1, d, d), lambda l, j: (l, 0, j)),
            pl.BlockSpec((1, 1, d), lambda l, j: (l, 0, j)),
        ],
        out_specs=pl.BlockSpec((1, batch, d), lambda l, j: (l, 0, j)),
        out_shape=jax.ShapeDtypeStruct((depth, batch, d3), F32),
        name="adaln",
    )(c.astype(F32), ada_w.astype(F32), ada_b.astype(F32).reshape(depth, 1, d3))


def _pool_fold_kernel(w_ref, pw_ref, scale_ref, out_ref):
    w_hi, w_lo = _split_bf16(w_ref[0], 2)
    p_hi, p_lo = _split_bf16(pw_ref[0, 0], 2)
    acc = _dot(w_hi, p_hi) + _dot(w_hi, p_lo) + _dot(w_lo, p_hi)
    out_ref[0] = acc * scale_ref[0]


def _pool_fold_call(w_in, pool_w, pool_scale):
    depth, d, _ = w_in.shape
    groups = pool_w.shape[1]
    gw = POOL_GROUP_WIDTH
    return pl.pallas_call(
        _pool_fold_kernel,
        grid=(depth, groups),
        in_specs=[
            pl.BlockSpec((1, d, gw), lambda l, g: (l, 0, g)),
            pl.BlockSpec((1, 1, gw, gw), lambda l, g: (l, g, 0, 0)),
            pl.BlockSpec((1, 1, gw), lambda l, g: (l, 0, g)),
        ],
        out_specs=pl.BlockSpec((1, d, gw), lambda l, g: (l, 0, g)),
        out_shape=jax.ShapeDtypeStruct((depth, d, POOL_WIDTH), F32),
        name="pool_fold",
    )(w_in.astype(F32), pool_w.astype(F32), pool_scale.astype(F32).reshape(depth, 1, POOL_WIDTH))


def _layer_kernel(tiles_per_seq, n_tiles,
                  xf_ref, xb_ref, modf_ref, modb_ref, npre_ref, w_in_ref,
                  lb_ref, hnw_ref, w_out_ref, npost_ref, wsum_ref, lvl_ref,
                  out_ref,
                  state_ref, tail_ref, q_ref, k_ref, v_ref, g_ref, bsum_ref, mixp_ref, flag_ref,
                  mixh_ref, attn_ref, b_ref, ql_ref, kl_ref):
    ts = SEQ_TILE
    n_chunks = ts // CHUNK
    step = pl.program_id(0)
    front_tile = jnp.minimum(step, n_tiles - 1)
    back_tile = jnp.maximum(step - 1, 0)
    slot_f = step % SLOTS
    slot_b = (step + 1) % SLOTS

    @pl.when(step == 0)
    def _():
        for ref in (q_ref, k_ref, v_ref, g_ref, bsum_ref, mixp_ref):
            ref[...] = jnp.zeros_like(ref)
        flag_ref[1] = 1

    @pl.when(front_tile % tiles_per_seq == 0)
    def _():
        tail_ref[...] = jnp.zeros_like(tail_ref)

    @pl.when(back_tile % tiles_per_seq == 0)
    def _():
        state_ref[...] = jnp.zeros_like(state_ref)

    lvl = lvl_ref[...]
    causal = lvl >= 0
    hnw = hnw_ref[...]
    heads = [slice(hd * HEAD_DIM, (hd + 1) * HEAD_DIM) for hd in range(HGRN_HEADS)]

    def front():
        x = xf_ref[0]
        mod = modf_ref[0]
        shift = mod[:, 0:D_MODEL]
        scale = mod[:, D_MODEL:2 * D_MODEL]
        ms = jnp.mean(x * x, axis=-1, keepdims=True)
        xn = x * lax.rsqrt(ms + EPS) * npre_ref[...]
        h = (xn * (1.0 + scale) + shift).astype(BF16)

        def in_proj(section):
            return _dot(h, w_in_ref[:, SECTION_STARTS[section]:SECTION_STARTS[section + 1]])

        u = in_proj(SEC_POOL_U)
        part = jnp.concatenate([tail_ref[...], u], axis=0)
        tail_ref[...] = u[ts - POOL_TAIL:ts, :]
        g_pool = in_proj(SEC_POOL_GATE)
        pos = (front_tile % tiles_per_seq) * ts + lax.broadcasted_iota(jnp.int32, (ts, 1), 0)
        for g, w in enumerate(POOL_WINDOWS):
            lo, hi = g * POOL_GROUP_WIDTH, (g + 1) * POOL_GROUP_WIDTH
            part = part[:, (POOL_GROUP_WIDTH if g else 0):]
            part = part + pltpu.roll(part, w // 2, 0)
            cnt = jnp.minimum(pos + 1, w).astype(F32)
            y = part[POOL_TAIL:, 0:POOL_GROUP_WIDTH] / cnt - u[:, lo:hi]
            gp = g_pool[:, lo:hi]
            mixp_ref[slot_f, :, lo:hi] = (y * (gp * _sigmoid(gp))).astype(BF16)

        q_ref[slot_f] = in_proj(SEC_Q)
        lb = lb_ref[...]
        f = lb + (1.0 - lb) * _sigmoid(in_proj(SEC_FORGET))
        k_ref[slot_f] = 1.0 - f
        lf = jnp.log(f) * LOG2E
        v_ref[slot_f] = in_proj(SEC_V)
        g_ref[slot_f] = in_proj(SEC_OUT_GATE)
        unsafe = None
        for c in range(n_chunks):
            lf3 = jnp.concatenate(_split_bf16(lf[c * CHUNK:(c + 1) * CHUNK, :], SPLIT), axis=0)
            b_c = _dot(wsum_ref[...], lf3)
            bsum_ref[slot_f, c * CHUNK:(c + 1) * CHUNK, :] = b_c
            b_mid = b_c[MID_ROW:MID_ROW + 1, :]
            b_last = b_c[CHUNK - 1:CHUNK, :]
            ok = jnp.logical_and(-b_mid <= DIRECT_MAX_LOG2, b_mid - b_last <= DIRECT_MAX_LOG2)
            bad = jnp.where(ok, 0.0, 1.0)
            unsafe = bad if unsafe is None else jnp.maximum(unsafe, bad)
        flag_ref[slot_f] = (jnp.max(unsafe) < 0.5).astype(jnp.int32)

    def chunk_outputs(rows, qe, k_dec, e_last):
        v_c = v_ref[slot_b, rows, :]
        k_dec_b = k_dec.astype(BF16)
        for hd, hl in enumerate(heads):
            v_t = jnp.transpose(v_c[:, hl]).astype(BF16)
            state_t = state_ref[hd]
            o = _dot_nt(jnp.concatenate([qe[:, hl], attn_ref[hd]], axis=1),
                        jnp.concatenate([state_t.astype(BF16), v_t], axis=1))
            on = o * lax.rsqrt(jnp.mean(o * o, axis=-1, keepdims=True) + EPS) * hnw
            gh = g_ref[slot_b, rows, hl]
            mixh_ref[rows, hl] = (on * (gh * _sigmoid(gh))).astype(BF16)
            state_ref[hd] = state_t * e_last[:, hl] + _dot(v_t, k_dec_b[:, hl])

    def direct_chunk(c):
        rows = slice(c * CHUNK, (c + 1) * CHUNK)
        b_c = bsum_ref[slot_b, rows, :]
        b_mid = bsum_ref[slot_b, c * CHUNK + MID_ROW:c * CHUNK + MID_ROW + 1, :]
        b_last = bsum_ref[slot_b, (c + 1) * CHUNK - 1:(c + 1) * CHUNK, :]
        d = b_c - b_mid
        q_s = q_ref[slot_b, rows, :] * jnp.exp2(d)
        k_s = k_ref[slot_b, rows, :] * jnp.exp2(-d)
        q_b = q_s.astype(BF16)
        k_b = k_s.astype(BF16)
        for hd, hl in enumerate(heads):
            attn_ref[hd] = jnp.where(causal, _dot_nt(q_b[:, hl], k_b[:, hl]), 0.0).astype(BF16)
        chunk_outputs(rows, (q_s * jnp.exp2(b_mid)).astype(BF16),
                      k_s * jnp.exp2(b_last - b_mid), jnp.exp2(b_last))

    def robust_chunk(c, carry):
        rows = pl.ds(pl.multiple_of(c * CHUNK, CHUNK), CHUNK)
        b_ref[...] = bsum_ref[slot_b, rows, :]
        b_c = b_ref[...]
        q_c = q_ref[slot_b, rows, :]
        k_c = k_ref[slot_b, rows, :]
        ql_ref[0:CHUNK, :] = q_c.astype(BF16)
        kl_ref[0] = k_c.astype(BF16)
        ql_ref[CHUNK:2 * CHUNK, :] = (q_c * (1.0 - k_c)).astype(BF16)
        row_in_group = lax.broadcasted_iota(jnp.int32, (SUBLANES, 1), 0)
        for level in range(2, NUM_LEVELS + 1):
            m = 1 << (level - 1)
            refs = []
            for r in range(0, CHUNK, SUBLANES):
                if 2 * m >= SUBLANES:
                    refs.append(jnp.broadcast_to(b_ref[_ref_row(level, r):_ref_row(level, r) + 1, :],
                                                 (SUBLANES, HGRN_WIDTH)))
                else:
                    lo_ref = b_ref[_ref_row(level, r):_ref_row(level, r) + 1, :]
                    hi_ref = b_ref[_ref_row(level, r + 2 * m):_ref_row(level, r + 2 * m) + 1, :]
                    refs.append(jnp.where(row_in_group < 2 * m, lo_ref, hi_ref))
            e_l = jnp.exp2(-jnp.abs(b_c - jnp.concatenate(refs, axis=0)))
            ql_ref[level * CHUNK:(level + 1) * CHUNK, :] = (q_c * e_l).astype(BF16)
            kl_ref[level] = (k_c * e_l).astype(BF16)
        for hd, hl in enumerate(heads):
            a01 = _dot_nt(ql_ref[0:2 * CHUNK, hl], kl_ref[0, :, hl])
            attn = jnp.where(lvl == 0, a01[0:CHUNK], 0.0)
            attn = jnp.where(lvl == 1, a01[CHUNK:2 * CHUNK], attn)
            for level in range(2, NUM_LEVELS + 1):
                a_l = _dot_nt(ql_ref[level * CHUNK:(level + 1) * CHUNK, hl], kl_ref[level, :, hl])
                attn = jnp.where(lvl == level, a_l, attn)
            attn_ref[hd] = attn.astype(BF16)
        b_last = b_ref[CHUNK - 1:CHUNK, :]
        chunk_outputs(rows, (q_c * jnp.exp2(b_c)).astype(BF16),
                      k_c * jnp.exp2(b_last - b_c), jnp.exp2(b_last))
        return carry

    def back_epilogue():
        gate = modb_ref[0][:, 2 * D_MODEL:3 * D_MODEL]
        mix = jnp.concatenate([mixp_ref[slot_b], mixh_ref[...]], axis=1)
        y = _dot(mix, w_out_ref[...])
        yn = y * lax.rsqrt(jnp.mean(y * y, axis=-1, keepdims=True) + EPS) * npost_ref[...]
        out_ref[0] = xb_ref[0] + gate * yn

    use_direct = flag_ref[slot_b] == 1

    @pl.when(use_direct)
    def _():
        front()
        for c in range(n_chunks):
            direct_chunk(c)
        back_epilogue()

    @pl.when(jnp.logical_not(use_direct))
    def _():
        front()
        lax.fori_loop(0, n_chunks, robust_chunk, 0)
        back_epilogue()


def _layer_call(x, mod, npre, w_in, lb, hnw, w_out, npost, wsum, lvl):
    batch, seq, d = x.shape
    ts = SEQ_TILE
    tiles_per_seq = seq // ts
    n_tiles = batch * tiles_per_seq

    def front_tile(g):
        t = jnp.minimum(g, n_tiles - 1)
        return t // tiles_per_seq, t % tiles_per_seq

    def back_tile(g):
        t = jnp.maximum(g - 1, 0)
        return t // tiles_per_seq, t % tiles_per_seq

    const2 = lambda g: (0, 0)
    once = pl.Buffered(1)
    return pl.pallas_call(
        functools.partial(_layer_kernel, tiles_per_seq, n_tiles),
        grid=(n_tiles + 1,),
        in_specs=[
            pl.BlockSpec((1, ts, d), lambda g: (*front_tile(g), 0)),
            pl.BlockSpec((1, ts, d), lambda g: (*back_tile(g), 0)),
            pl.BlockSpec((1, 1, 3 * d), lambda g: (front_tile(g)[0], 0, 0)),
            pl.BlockSpec((1, 1, 3 * d), lambda g: (back_tile(g)[0], 0, 0)),
            pl.BlockSpec((1, d), const2),
            pl.BlockSpec((d, IN_WIDTH), const2, pipeline_mode=once),
            pl.BlockSpec((1, HGRN_WIDTH), const2),
            pl.BlockSpec((1, HEAD_DIM), const2),
            pl.BlockSpec((MIX_WIDTH, d), const2, pipeline_mode=once),
            pl.BlockSpec((1, d), const2),
            pl.BlockSpec(wsum.shape, const2),
            pl.BlockSpec(lvl.shape, const2),
        ],
        out_specs=pl.BlockSpec((1, ts, d), lambda g: (*back_tile(g), 0)),
        out_shape=jax.ShapeDtypeStruct(x.shape, F32),
        scratch_shapes=[
            pltpu.VMEM((HGRN_HEADS, HEAD_DIM, HEAD_DIM), F32),
            pltpu.VMEM((POOL_TAIL, POOL_WIDTH), F32),
            pltpu.VMEM((SLOTS, ts, HGRN_WIDTH), F32),
            pltpu.VMEM((SLOTS, ts, HGRN_WIDTH), F32),
            pltpu.VMEM((SLOTS, ts, HGRN_WIDTH), F32),
            pltpu.VMEM((SLOTS, ts, HGRN_WIDTH), F32),
            pltpu.VMEM((SLOTS, ts, HGRN_WIDTH), F32),
            pltpu.VMEM((SLOTS, ts, POOL_WIDTH), BF16),
            pltpu.SMEM((SLOTS,), jnp.int32),
            pltpu.VMEM((ts, HGRN_WIDTH), BF16),
            pltpu.VMEM((HGRN_HEADS, CHUNK, CHUNK), BF16),
            pltpu.VMEM((CHUNK, HGRN_WIDTH), F32),
            pltpu.VMEM(((NUM_LEVELS + 1) * CHUNK, HGRN_WIDTH), BF16),
            pltpu.VMEM((NUM_LEVELS + 1, CHUNK, HGRN_WIDTH), BF16),
        ],
        compiler_params=pltpu.CompilerParams(
            dimension_semantics=("arbitrary",),
            vmem_limit_bytes=VMEM_LIMIT_BYTES),
        name="hybrid_layer",
    )(x, x, mod, mod, npre, w_in, lb, hnw, w_out, npost, wsum, lvl)


def kernel(x, c, norm_pre_w, ada_w, ada_b, w_in, pool_w, pool_scale, hgrn_lower_bounds,
           hgrn_norm_w, w_out, norm_post_w):
    depth = ada_w.shape[0]
    batch = x.shape[0]
    lower = _lower_bounds_call(hgrn_lower_bounds)
    mod = _adaln_call(c, ada_w, ada_b)
    w_pool = _pool_fold_call(w_in[:, :, 0:POOL_WIDTH], pool_w, pool_scale)
    wsum = jnp.asarray(_cumsum_matrix(), dtype=BF16)
    lvl = jnp.asarray(_level_map())
    h = x.astype(F32)
    for l in range(depth):
        h = _layer_call(
            h,
            mod[l].reshape(batch, 1, 3 * D_MODEL),
            norm_pre_w[l].astype(F32).reshape(1, D_MODEL),
            jnp.concatenate([w_pool[l].astype(BF16), w_in[l, :, POOL_WIDTH:].astype(BF16)], axis=1),
            lower[l].reshape(1, HGRN_WIDTH),
            hgrn_norm_w[l].astype(F32).reshape(1, HEAD_DIM),
            w_out[l].astype(BF16),
            norm_post_w[l].astype(F32).reshape(1, D_MODEL),
            wsum, lvl)
    return h.astype(x.dtype)
```

```python
import functools

import numpy as np
import jax
import jax.numpy as jnp
from jax import lax
from jax.experimental import pallas as pl
from jax.experimental.pallas import tpu as pltpu

D_MODEL = 1024
CHUNK = 64
POOL_WIDTH = 1024
POOL_WINDOWS = (2, 4, 8, 16)
POOL_GROUP_WIDTH = POOL_WIDTH // len(POOL_WINDOWS)
HGRN_WIDTH = 1024
HEAD_DIM = 128
HGRN_HEADS = HGRN_WIDTH // HEAD_DIM
MIX_WIDTH = POOL_WIDTH + HGRN_WIDTH
IN_WIDTH = 2 * POOL_WIDTH + 4 * HGRN_WIDTH
EPS = 1e-6

SEQ_TILE = 512
POOL_TAIL = 16
assert POOL_WINDOWS == tuple(2 << g for g in range(len(POOL_WINDOWS))) and POOL_TAIL >= POOL_WINDOWS[-1]
NUM_LEVELS = 6
SPLIT = 3
VMEM_LIMIT_BYTES = 62 * 1024 * 1024
SUBLANES = 8
LOG2E = 1.4426950408889634
MID_ROW = CHUNK // 2 - 1
DIRECT_MAX_LOG2 = 110.0
SLOTS = 2

F32 = jnp.float32
BF16 = jnp.bfloat16


def _ref_row(level, row):
    m = 1 << (level - 1)
    return (row // (2 * m)) * (2 * m) + m - 1


def _cumsum_matrix():
    t = np.arange(CHUNK)[:, None]
    j = np.arange(CHUNK)[None, :]
    w = (j <= t).astype(np.float32)
    return np.concatenate([w] * SPLIT, axis=1)


def _level_map():
    t = np.arange(CHUNK)[:, None]
    s = np.arange(CHUNK)[None, :]
    x = t ^ s
    lvl = np.where(x > 0, np.floor(np.log2(np.maximum(x, 1))).astype(np.int32) + 1, 0)
    return np.where(s > t, -1, lvl).astype(np.int32)


def _split_bf16(a, pieces):
    out = []
    rest = a
    for _ in range(pieces):
        p = rest.astype(BF16)
        out.append(p)
        rest = rest - p.astype(F32)
    return out


def _sigmoid(a):
    return 1.0 / (1.0 + jnp.exp(-a))


def _dot(a, b):
    return jnp.dot(a, b, preferred_element_type=F32)


def _dot_nt(a, b):
    return lax.dot_general(a, b, (((1,), (1,)), ((), ())), preferred_element_type=F32)


def _lower_bounds_kernel(lb_ref, out_ref):
    depth = lb_ref.shape[0]
    rows = [lb_ref[l:l + 1, :] for l in range(depth)]
    mx = rows[0]
    for r in rows[1:]:
        mx = jnp.maximum(mx, r)
    ex = [jnp.exp(r - mx) for r in rows]
    den = ex[0]
    for e in ex[1:]:
        den = den + e
    sm = [e / den for e in ex]
    run = sm[0]
    out_ref[0:1, :] = run - sm[0]
    for l in range(1, depth):
        run = run + sm[l]
        out_ref[l:l + 1, :] = run - sm[0]


def _lower_bounds_call(hgrn_lower_bounds):
    return pl.pallas_call(
        _lower_bounds_kernel,
        out_shape=jax.ShapeDtypeStruct(hgrn_lower_bounds.shape, F32),
        name="lower_bounds",
    )(hgrn_lower_bounds.astype(F32))


def _adaln_kernel(c_ref, w_ref, b_ref, mod_ref):
    c = c_ref[...]
    sc = c * _sigmoid(c)
    w = w_ref[0]
    a_hi, a_lo = _split_bf16(sc, 2)
    w_hi, w_lo = _split_bf16(w, 2)
    acc = _dot(a_hi, w_hi) + _dot(a_hi, w_lo) + _dot(a_lo, w_hi)
    mod_ref[0] = acc + b_ref[0]


def _adaln_call(c, ada_w, ada_b):
    depth, d, d3 = ada_w.shape
    batch = c.shape[0]
    n_col = d3 // d
    return pl.pallas_call(
        _adaln_kernel,
        grid=(depth, n_col),
        in_specs=[
            pl.BlockSpec((batch, d), lambda l, j: (0, 0)),
            pl.BlockSpec((1, d, d), lambda l, j: (l, 0, j)),
            pl.BlockSpec((1, 1, d), lambda l, j: (l, 0, j)),
        ],
        out_specs=pl.BlockSpec((1, batch, d), lambda l, j: (l, 0, j)),
        out_shape=jax.ShapeDtypeStruct((depth, batch, d3), F32),
        name="adaln",
    )(c.astype(F32), ada_w.astype(F32), ada_b.astype(F32).reshape(depth, 1, d3))


def _layer_kernel(tiles_per_seq, n_tiles,
                  xf_ref, xb_ref, modf_ref, modb_ref, npre_ref, w_in_ref, pool_w_ref,
                  pool_scale_ref, lb_ref, hnw_ref, w_out_ref, npost_ref, wsum_ref, lvl_ref,
                  out_ref,
                  state_ref, tail_ref, q_ref, k_ref, v_ref, g_ref, bsum_ref, mixp_ref, flag_ref,
                  mixh_ref, attn_ref, b_ref, ql_ref, kl_ref):
    ts = SEQ_TILE
    n_chunks = ts // CHUNK
    step = pl.program_id(0)
    front_tile = jnp.minimum(step, n_tiles - 1)
    back_tile = jnp.maximum(step - 1, 0)
    slot_f = step % SLOTS
    slot_b = (step + 1) % SLOTS

    @pl.when(step == 0)
    def _():
        for ref in (q_ref, k_ref, v_ref, g_ref, bsum_ref, mixp_ref):
            ref[...] = jnp.zeros_like(ref)
        flag_ref[1] = 1

    @pl.when(front_tile % tiles_per_seq == 0)
    def _():
        tail_ref[...] = jnp.zeros_like(tail_ref)

    @pl.when(back_tile % tiles_per_seq == 0)
    def _():
        state_ref[...] = jnp.zeros_like(state_ref)

    lvl = lvl_ref[...]
    causal = lvl >= 0
    hnw = hnw_ref[...]
    heads = [slice(hd * HEAD_DIM, (hd + 1) * HEAD_DIM) for hd in range(HGRN_HEADS)]

    def front():
        x = xf_ref[0]
        mod = modf_ref[0]
        shift = mod[:, 0:D_MODEL]
        scale = mod[:, D_MODEL:2 * D_MODEL]
        ms = jnp.mean(x * x, axis=-1, keepdims=True)
        xn = x * lax.rsqrt(ms + EPS) * npre_ref[...]
        h = (xn * (1.0 + scale) + shift).astype(BF16)

        def in_proj(col):
            return _dot(h, w_in_ref[:, col * 1024:(col + 1) * 1024])

        u = in_proj(0)
        part = jnp.concatenate([tail_ref[...], u], axis=0)
        tail_ref[...] = u[ts - POOL_TAIL:ts, :]
        win_sum = []
        for g, w in enumerate(POOL_WINDOWS):
            part = part[:, (POOL_GROUP_WIDTH if g else 0):]
            part = part + pltpu.roll(part, w // 2, 0)
            win_sum.append(part[POOL_TAIL:, 0:POOL_GROUP_WIDTH])
        g_pool = in_proj(1)
        pos = (front_tile % tiles_per_seq) * ts + lax.broadcasted_iota(jnp.int32, (ts, 1), 0)
        for g, w in enumerate(POOL_WINDOWS):
            lo, hi = g * POOL_GROUP_WIDTH, (g + 1) * POOL_GROUP_WIDTH
            u_g = u[:, lo:hi]
            cnt = jnp.minimum(pos + 1, w).astype(F32)
            d = (win_sum[g] / cnt - u_g).astype(BF16)
            y = _dot(d, pool_w_ref[g]) * pool_scale_ref[:, lo:hi]
            gp = g_pool[:, lo:hi]
            mixp_ref[slot_f, :, lo:hi] = (y * (gp * _sigmoid(gp))).astype(BF16)

        q_ref[slot_f] = in_proj(2)
        lb = lb_ref[...]
        f = lb + (1.0 - lb) * _sigmoid(in_proj(3))
        k_ref[slot_f] = 1.0 - f
        lf = jnp.log(f) * LOG2E
        v_ref[slot_f] = in_proj(4)
        g_ref[slot_f] = in_proj(5)
        unsafe = None
        for c in range(n_chunks):
            lf3 = jnp.concatenate(_split_bf16(lf[c * CHUNK:(c + 1) * CHUNK, :], SPLIT), axis=0)
            b_c = _dot(wsum_ref[...], lf3)
            bsum_ref[slot_f, c * CHUNK:(c + 1) * CHUNK, :] = b_c
            b_mid = b_c[MID_ROW:MID_ROW + 1, :]
            b_last = b_c[CHUNK - 1:CHUNK, :]
            ok = jnp.logical_and(-b_mid <= DIRECT_MAX_LOG2, b_mid - b_last <= DIRECT_MAX_LOG2)
            bad = jnp.where(ok, 0.0, 1.0)
            unsafe = bad if unsafe is None else jnp.maximum(unsafe, bad)
        flag_ref[slot_f] = (jnp.max(unsafe) < 0.5).astype(jnp.int32)

    def chunk_outputs(rows, qe, k_dec, e_last):
        v_c = v_ref[slot_b, rows, :]
        k_dec_b = k_dec.astype(BF16)
        for hd, hl in enumerate(heads):
            v_t = jnp.transpose(v_c[:, hl]).astype(BF16)
            state_t = state_ref[hd]
            o = _dot_nt(jnp.concatenate([qe[:, hl], attn_ref[hd]], axis=1),
                        jnp.concatenate([state_t.astype(BF16), v_t], axis=1))
            on = o * lax.rsqrt(jnp.mean(o * o, axis=-1, keepdims=True) + EPS) * hnw
            gh = g_ref[slot_b, rows, hl]
            mixh_ref[rows, hl] = (on * (gh * _sigmoid(gh))).astype(BF16)
            state_ref[hd] = state_t * e_last[:, hl] + _dot(v_t, k_dec_b[:, hl])

    def direct_chunk(c):
        rows = slice(c * CHUNK, (c + 1) * CHUNK)
        b_c = bsum_ref[slot_b, rows, :]
        b_mid = bsum_ref[slot_b, c * CHUNK + MID_ROW:c * CHUNK + MID_ROW + 1, :]
        b_last = bsum_ref[slot_b, (c + 1) * CHUNK - 1:(c + 1) * CHUNK, :]
        d = b_c - b_mid
        q_s = q_ref[slot_b, rows, :] * jnp.exp2(d)
        k_s = k_ref[slot_b, rows, :] * jnp.exp2(-d)
        q_b = q_s.astype(BF16)
        k_b = k_s.astype(BF16)
        for hd, hl in enumerate(heads):
            attn_ref[hd] = jnp.where(causal, _dot_nt(q_b[:, hl], k_b[:, hl]), 0.0).astype(BF16)
        chunk_outputs(rows, (q_s * jnp.exp2(b_mid)).astype(BF16),
                      k_s * jnp.exp2(b_last - b_mid), jnp.exp2(b_last))

    def robust_chunk(c, carry):
        rows = pl.ds(pl.multiple_of(c * CHUNK, CHUNK), CHUNK)
        b_ref[...] = bsum_ref[slot_b, rows, :]
        b_c = b_ref[...]
        q_c = q_ref[slot_b, rows, :]
        k_c = k_ref[slot_b, rows, :]
        ql_ref[0:CHUNK, :] = q_c.astype(BF16)
        kl_ref[0] = k_c.astype(BF16)
        ql_ref[CHUNK:2 * CHUNK, :] = (q_c * (1.0 - k_c)).astype(BF16)
        row_in_group = lax.broadcasted_iota(jnp.int32, (SUBLANES, 1), 0)
        for level in range(2, NUM_LEVELS + 1):
            m = 1 << (level - 1)
            refs = []
            for r in range(0, CHUNK, SUBLANES):
                if 2 * m >= SUBLANES:
                    refs.append(jnp.broadcast_to(b_ref[_ref_row(level, r):_ref_row(level, r) + 1, :],
                                                 (SUBLANES, HGRN_WIDTH)))
                else:
                    lo_ref = b_ref[_ref_row(level, r):_ref_row(level, r) + 1, :]
                    hi_ref = b_ref[_ref_row(level, r + 2 * m):_ref_row(level, r + 2 * m) + 1, :]
                    refs.append(jnp.where(row_in_group < 2 * m, lo_ref, hi_ref))
            e_l = jnp.exp2(-jnp.abs(b_c - jnp.concatenate(refs, axis=0)))
            ql_ref[level * CHUNK:(level + 1) * CHUNK, :] = (q_c * e_l).astype(BF16)
            kl_ref[level] = (k_c * e_l).astype(BF16)
        for hd, hl in enumerate(heads):
            a01 = _dot_nt(ql_ref[0:2 * CHUNK, hl], kl_ref[0, :, hl])
            attn = jnp.where(lvl == 0, a01[0:CHUNK], 0.0)
            attn = jnp.where(lvl == 1, a01[CHUNK:2 * CHUNK], attn)
            for level in range(2, NUM_LEVELS + 1):
                a_l = _dot_nt(ql_ref[level * CHUNK:(level + 1) * CHUNK, hl], kl_ref[level, :, hl])
                attn = jnp.where(lvl == level, a_l, attn)
            attn_ref[hd] = attn.astype(BF16)
        b_last = b_ref[CHUNK - 1:CHUNK, :]
        chunk_outputs(rows, (q_c * jnp.exp2(b_c)).astype(BF16),
                      k_c * jnp.exp2(b_last - b_c), jnp.exp2(b_last))
        return carry

    def back_epilogue():
        gate = modb_ref[0][:, 2 * D_MODEL:3 * D_MODEL]
        mix = jnp.concatenate([mixp_ref[slot_b], mixh_ref[...]], axis=1)
        y = _dot(mix, w_out_ref[...])
        yn = y * lax.rsqrt(jnp.mean(y * y, axis=-1, keepdims=True) + EPS) * npost_ref[...]
        out_ref[0] = xb_ref[0] + gate * yn

    use_direct = flag_ref[slot_b] == 1

    @pl.when(use_direct)
    def _():
        front()
        for c in range(n_chunks):
            direct_chunk(c)
        back_epilogue()

    @pl.when(jnp.logical_not(use_direct))
    def _():
        front()
        lax.fori_loop(0, n_chunks, robust_chunk, 0)
        back_epilogue()


def _layer_call(layer, x, mod, npre, w_in, pool_w, pool_scale, lb, hnw, w_out, npost, wsum, lvl):
    batch, seq, d = x.shape
    ts = SEQ_TILE
    tiles_per_seq = seq // ts
    n_tiles = batch * tiles_per_seq

    def front_tile(g):
        t = jnp.minimum(g, n_tiles - 1)
        return t // tiles_per_seq, t % tiles_per_seq

    def back_tile(g):
        t = jnp.maximum(g - 1, 0)
        return t // tiles_per_seq, t % tiles_per_seq

    const2 = lambda g: (0, 0)
    of_layer3 = lambda g: (layer, 0, 0)
    of_layer4 = lambda g: (layer, 0, 0, 0)
    once = pl.Buffered(1)
    return pl.pallas_call(
        functools.partial(_layer_kernel, tiles_per_seq, n_tiles),
        grid=(n_tiles + 1,),
        in_specs=[
            pl.BlockSpec((1, ts, d), lambda g: (*front_tile(g), 0)),
            pl.BlockSpec((1, ts, d), lambda g: (*back_tile(g), 0)),
            pl.BlockSpec((None, 1, 1, 3 * d), lambda g: (layer, front_tile(g)[0], 0, 0)),
            pl.BlockSpec((None, 1, 1, 3 * d), lambda g: (layer, back_tile(g)[0], 0, 0)),
            pl.BlockSpec((None, 1, d), of_layer3),
            pl.BlockSpec((None, d, IN_WIDTH), of_layer3, pipeline_mode=once),
            pl.BlockSpec((None,) + pool_w.shape[1:], of_layer4, pipeline_mode=once),
            pl.BlockSpec((None, 1, POOL_WIDTH), of_layer3),
            pl.BlockSpec((None, 1, HGRN_WIDTH), of_layer3),
            pl.BlockSpec((None, 1, HEAD_DIM), of_layer3),
            pl.BlockSpec((None, MIX_WIDTH, d), of_layer3, pipeline_mode=once),
            pl.BlockSpec((None, 1, d), of_layer3),
            pl.BlockSpec(wsum.shape, const2),
            pl.BlockSpec(lvl.shape, const2),
        ],
        out_specs=pl.BlockSpec((1, ts, d), lambda g: (*back_tile(g), 0)),
        out_shape=jax.ShapeDtypeStruct(x.shape, F32),
        scratch_shapes=[
            pltpu.VMEM((HGRN_HEADS, HEAD_DIM, HEAD_DIM), F32),
            pltpu.VMEM((POOL_TAIL, POOL_WIDTH), F32),
            pltpu.VMEM((SLOTS, ts, HGRN_WIDTH), F32),
            pltpu.VMEM((SLOTS, ts, HGRN_WIDTH), F32),
            pltpu.VMEM((SLOTS, ts, HGRN_WIDTH), F32),
            pltpu.VMEM((SLOTS, ts, HGRN_WIDTH), F32),
            pltpu.VMEM((SLOTS, ts, HGRN_WIDTH), F32),
            pltpu.VMEM((SLOTS, ts, POOL_WIDTH), BF16),
            pltpu.SMEM((SLOTS,), jnp.int32),
            pltpu.VMEM((ts, HGRN_WIDTH), BF16),
            pltpu.VMEM((HGRN_HEADS, CHUNK, CHUNK), BF16),
            pltpu.VMEM((CHUNK, HGRN_WIDTH), F32),
            pltpu.VMEM(((NUM_LEVELS + 1) * CHUNK, HGRN_WIDTH), BF16),
            pltpu.VMEM((NUM_LEVELS + 1, CHUNK, HGRN_WIDTH), BF16),
        ],
        compiler_params=pltpu.CompilerParams(
            dimension_semantics=("arbitrary",),
            vmem_limit_bytes=VMEM_LIMIT_BYTES),
        name="hybrid_layer",
    )(x, x, mod, mod, npre, w_in, pool_w, pool_scale, lb, hnw, w_out, npost, wsum, lvl)


def kernel(x, c, norm_pre_w, ada_w, ada_b, w_in, pool_w, pool_scale, hgrn_lower_bounds,
           hgrn_norm_w, w_out, norm_post_w):
    depth = ada_w.shape[0]
    batch = x.shape[0]
    lower = _lower_bounds_call(hgrn_lower_bounds)
    mod = _adaln_call(c, ada_w, ada_b)
    wsum = jnp.asarray(_cumsum_matrix(), dtype=BF16)
    lvl = jnp.asarray(_level_map())
    stacked = (
        mod.reshape(depth, batch, 1, 3 * D_MODEL),
        norm_pre_w.astype(F32).reshape(depth, 1, D_MODEL),
        w_in.astype(BF16),
        pool_w.astype(BF16),
        pool_scale.astype(F32).reshape(depth, 1, POOL_WIDTH),
        lower.reshape(depth, 1, HGRN_WIDTH),
        hgrn_norm_w.astype(F32).reshape(depth, 1, HEAD_DIM),
        w_out.astype(BF16),
        norm_post_w.astype(F32).reshape(depth, 1, D_MODEL),
    )
    h = x.astype(F32)
    for l in range(depth):
        h = _layer_call(l, h, *stacked, wsum, lvl)
    return h.astype(x.dtype)
```

```python
import functools

import numpy as np
import jax
import jax.numpy as jnp
from jax import lax
from jax.experimental import pallas as pl
from jax.experimental.pallas import tpu as pltpu

D_MODEL = 1024
CHUNK = 64
POOL_WIDTH = 1024
POOL_WINDOWS = (2, 4, 8, 16)
POOL_GROUP_WIDTH = POOL_WIDTH // len(POOL_WINDOWS)
HGRN_WIDTH = 1024
HEAD_DIM = 128
HGRN_HEADS = HGRN_WIDTH // HEAD_DIM
MIX_WIDTH = POOL_WIDTH + HGRN_WIDTH
IN_WIDTH = 2 * POOL_WIDTH + 4 * HGRN_WIDTH
EPS = 1e-6

SEQ_TILE = 512
POOL_TAIL = 16
assert POOL_WINDOWS == tuple(2 << g for g in range(len(POOL_WINDOWS))) and POOL_TAIL >= POOL_WINDOWS[-1]
NUM_LEVELS = 6
SPLIT = 3
VMEM_LIMIT_BYTES = 62 * 1024 * 1024
SUBLANES = 8
LOG2E = 1.4426950408889634
MID_ROW = CHUNK // 2 - 1
DIRECT_MAX_LOG2 = 110.0
SLOTS = 2

F32 = jnp.float32
BF16 = jnp.bfloat16


def _ref_row(level, row):
    m = 1 << (level - 1)
    return (row // (2 * m)) * (2 * m) + m - 1


def _cumsum_matrix():
    t = np.arange(CHUNK)[:, None]
    j = np.arange(CHUNK)[None, :]
    w = (j <= t).astype(np.float32)
    return np.concatenate([w] * SPLIT, axis=1)


def _level_map():
    t = np.arange(CHUNK)[:, None]
    s = np.arange(CHUNK)[None, :]
    x = t ^ s
    lvl = np.where(x > 0, np.floor(np.log2(np.maximum(x, 1))).astype(np.int32) + 1, 0)
    return np.where(s > t, -1, lvl).astype(np.int32)


def _split_bf16(a, pieces):
    out = []
    rest = a
    for _ in range(pieces):
        p = rest.astype(BF16)
        out.append(p)
        rest = rest - p.astype(F32)
    return out


def _sigmoid(a):
    return 1.0 / (1.0 + jnp.exp(-a))


def _dot(a, b):
    return jnp.dot(a, b, preferred_element_type=F32)


def _dot_nt(a, b):
    return lax.dot_general(a, b, (((1,), (1,)), ((), ())), preferred_element_type=F32)


def _lower_bounds_kernel(lb_ref, out_ref):
    depth = lb_ref.shape[0]
    rows = [lb_ref[l:l + 1, :] for l in range(depth)]
    mx = rows[0]
    for r in rows[1:]:
        mx = jnp.maximum(mx, r)
    ex = [jnp.exp(r - mx) for r in rows]
    den = ex[0]
    for e in ex[1:]:
        den = den + e
    sm = [e / den for e in ex]
    run = sm[0]
    out_ref[0:1, :] = run - sm[0]
    for l in range(1, depth):
        run = run + sm[l]
        out_ref[l:l + 1, :] = run - sm[0]


def _lower_bounds_call(hgrn_lower_bounds):
    return pl.pallas_call(
        _lower_bounds_kernel,
        out_shape=jax.ShapeDtypeStruct(hgrn_lower_bounds.shape, F32),
        name="lower_bounds",
    )(hgrn_lower_bounds.astype(F32))


def _adaln_kernel(c_ref, w_ref, b_ref, mod_ref):
    c = c_ref[...]
    sc = c * _sigmoid(c)
    w = w_ref[0]
    a_hi, a_lo = _split_bf16(sc, 2)
    w_hi, w_lo = _split_bf16(w, 2)
    acc = _dot(a_hi, w_hi) + _dot(a_hi, w_lo) + _dot(a_lo, w_hi)
    mod_ref[0] = acc + b_ref[0]


def _adaln_call(c, ada_w, ada_b):
    depth, d, d3 = ada_w.shape
    batch = c.shape[0]
    n_col = d3 // d
    return pl.pallas_call(
        _adaln_kernel,
        grid=(depth, n_col),
        in_specs=[
            pl.BlockSpec((batch, d), lambda l, j: (0, 0)),
            pl.BlockSpec((1, d, d), lambda l, j: (l, 0, j)),
            pl.BlockSpec((1, 1, d), lambda l, j: (l, 0, j)),
        ],
        out_specs=pl.BlockSpec((1, batch, d), lambda l, j: (l, 0, j)),
        out_shape=jax.ShapeDtypeStruct((depth, batch, d3), F32),
        name="adaln",
    )(c.astype(F32), ada_w.astype(F32), ada_b.astype(F32).reshape(depth, 1, d3))


def _layer_kernel(tiles_per_seq, n_tiles,
                  xf_ref, xb_ref, modf_ref, modb_ref, npre_ref, w_in_ref, pool_w_ref,
                  pool_scale_ref, lb_ref, hnw_ref, w_out_ref, npost_ref, wsum_ref, lvl_ref,
                  out_ref,
                  state_ref, tail_ref, q_ref, k_ref, v_ref, g_ref, bsum_ref, mixp_ref, flag_ref,
                  mixh_ref, attn_ref, b_ref, ql_ref, kl_ref):
    ts = SEQ_TILE
    n_chunks = ts // CHUNK
    step = pl.program_id(0)
    front_tile = jnp.minimum(step, n_tiles - 1)
    back_tile = jnp.maximum(step - 1, 0)
    slot_f = step % SLOTS
    slot_b = (step + 1) % SLOTS

    @pl.when(step == 0)
    def _():
        for ref in (q_ref, k_ref, v_ref, g_ref, bsum_ref, mixp_ref):
            ref[...] = jnp.zeros_like(ref)
        flag_ref[1] = 1

    @pl.when(front_tile % tiles_per_seq == 0)
    def _():
        tail_ref[...] = jnp.zeros_like(tail_ref)

    @pl.when(back_tile % tiles_per_seq == 0)
    def _():
        state_ref[...] = jnp.zeros_like(state_ref)

    lvl = lvl_ref[...]
    causal = lvl >= 0
    hnw = hnw_ref[...]
    heads = [slice(hd * HEAD_DIM, (hd + 1) * HEAD_DIM) for hd in range(HGRN_HEADS)]

    def front():
        x = xf_ref[0]
        mod = modf_ref[0]
        shift = mod[:, 0:D_MODEL]
        scale = mod[:, D_MODEL:2 * D_MODEL]
        ms = jnp.mean(x * x, axis=-1, keepdims=True)
        xn = x * lax.rsqrt(ms + EPS) * npre_ref[...]
        h = (xn * (1.0 + scale) + shift).astype(BF16)

        def in_proj(col):
            return _dot(h, w_in_ref[:, col * 1024:(col + 1) * 1024])

        u = in_proj(0)
        part = jnp.concatenate([tail_ref[...], u], axis=0)
        tail_ref[...] = u[ts - POOL_TAIL:ts, :]
        win_sum = []
        for g, w in enumerate(POOL_WINDOWS):
            part = part[:, (POOL_GROUP_WIDTH if g else 0):]
            part = part + pltpu.roll(part, w // 2, 0)
            win_sum.append(part[POOL_TAIL:, 0:POOL_GROUP_WIDTH])
        g_pool = in_proj(1)
        pos = (front_tile % tiles_per_seq) * ts + lax.broadcasted_iota(jnp.int32, (ts, 1), 0)
        for g, w in enumerate(POOL_WINDOWS):
            lo, hi = g * POOL_GROUP_WIDTH, (g + 1) * POOL_GROUP_WIDTH
            u_g = u[:, lo:hi]
            cnt = jnp.minimum(pos + 1, w).astype(F32)
            d = (win_sum[g] / cnt - u_g).astype(BF16)
            y = _dot(d, pool_w_ref[g]) * pool_scale_ref[:, lo:hi]
            gp = g_pool[:, lo:hi]
            mixp_ref[slot_f, :, lo:hi] = (y * (gp * _sigmoid(gp))).astype(BF16)

        q_ref[slot_f] = in_proj(2)
        lb = lb_ref[...]
        f = lb + (1.0 - lb) * _sigmoid(in_proj(3))
        k_ref[slot_f] = 1.0 - f
        lf = jnp.log(f) * LOG2E
        v_ref[slot_f] = in_proj(4)
        g_ref[slot_f] = in_proj(5)
        unsafe = None
        for c in range(n_chunks):
            lf3 = jnp.concatenate(_split_bf16(lf[c * CHUNK:(c + 1) * CHUNK, :], SPLIT), axis=0)
            b_c = _dot(wsum_ref[...], lf3)
            bsum_ref[slot_f, c * CHUNK:(c + 1) * CHUNK, :] = b_c
            b_mid = b_c[MID_ROW:MID_ROW + 1, :]
            b_last = b_c[CHUNK - 1:CHUNK, :]
            ok = jnp.logical_and(-b_mid <= DIRECT_MAX_LOG2, b_mid - b_last <= DIRECT_MAX_LOG2)
            bad = jnp.where(ok, 0.0, 1.0)
            unsafe = bad if unsafe is None else jnp.maximum(unsafe, bad)
        flag_ref[slot_f] = (jnp.max(unsafe) < 0.5).astype(jnp.int32)

    def chunk_outputs(rows, qe, k_dec, e_last, attn):
        v_c = v_ref[slot_b, rows, :]
        k_dec_b = k_dec.astype(BF16)
        for hd, hl in enumerate(heads):
            v_t = jnp.transpose(v_c[:, hl]).astype(BF16)
            state_t = state_ref[hd]
            o = _dot_nt(jnp.concatenate([qe[:, hl], attn[hd]], axis=1),
                        jnp.concatenate([state_t.astype(BF16), v_t], axis=1))
            on = o * lax.rsqrt(jnp.mean(o * o, axis=-1, keepdims=True) + EPS) * hnw
            gh = g_ref[slot_b, rows, hl]
            mixh_ref[rows, hl] = (on * (gh * _sigmoid(gh))).astype(BF16)
            state_ref[hd] = state_t * e_last[:, hl] + _dot(v_t, k_dec_b[:, hl])

    def direct_chunk(c):
        rows = slice(c * CHUNK, (c + 1) * CHUNK)
        b_c = bsum_ref[slot_b, rows, :]
        b_mid = bsum_ref[slot_b, c * CHUNK + MID_ROW:c * CHUNK + MID_ROW + 1, :]
        b_last = bsum_ref[slot_b, (c + 1) * CHUNK - 1:(c + 1) * CHUNK, :]
        d = b_c - b_mid
        q_s = q_ref[slot_b, rows, :] * jnp.exp2(d)
        k_s = k_ref[slot_b, rows, :] * jnp.exp2(-d)
        q_b = q_s.astype(BF16)
        k_b = k_s.astype(BF16)
        attn = [jnp.where(causal, _dot_nt(q_b[:, hl], k_b[:, hl]), 0.0).astype(BF16) for hl in heads]
        chunk_outputs(rows, (q_s * jnp.exp2(b_mid)).astype(BF16),
                      k_s * jnp.exp2(b_last - b_mid), jnp.exp2(b_last), attn)

    def robust_chunk(c, carry):
        rows = pl.ds(pl.multiple_of(c * CHUNK, CHUNK), CHUNK)
        b_ref[...] = bsum_ref[slot_b, rows, :]
        b_c = b_ref[...]
        q_c = q_ref[slot_b, rows, :]
        k_c = k_ref[slot_b, rows, :]
        ql_ref[0:CHUNK, :] = q_c.astype(BF16)
        kl_ref[0] = k_c.astype(BF16)
        ql_ref[CHUNK:2 * CHUNK, :] = (q_c * (1.0 - k_c)).astype(BF16)
        row_in_group = lax.broadcasted_iota(jnp.int32, (SUBLANES, 1), 0)
        for level in range(2, NUM_LEVELS + 1):
            m = 1 << (level - 1)
            refs = []
            for r in range(0, CHUNK, SUBLANES):
                if 2 * m >= SUBLANES:
                    refs.append(jnp.broadcast_to(b_ref[_ref_row(level, r):_ref_row(level, r) + 1, :],
                                                 (SUBLANES, HGRN_WIDTH)))
                else:
                    lo_ref = b_ref[_ref_row(level, r):_ref_row(level, r) + 1, :]
                    hi_ref = b_ref[_ref_row(level, r + 2 * m):_ref_row(level, r + 2 * m) + 1, :]
                    refs.append(jnp.where(row_in_group < 2 * m, lo_ref, hi_ref))
            e_l = jnp.exp2(-jnp.abs(b_c - jnp.concatenate(refs, axis=0)))
            ql_ref[level * CHUNK:(level + 1) * CHUNK, :] = (q_c * e_l).astype(BF16)
            kl_ref[level] = (k_c * e_l).astype(BF16)
        for hd, hl in enumerate(heads):
            a01 = _dot_nt(ql_ref[0:2 * CHUNK, hl], kl_ref[0, :, hl])
            attn = jnp.where(lvl == 0, a01[0:CHUNK], 0.0)
            attn = jnp.where(lvl == 1, a01[CHUNK:2 * CHUNK], attn)
            for level in range(2, NUM_LEVELS + 1):
                a_l = _dot_nt(ql_ref[level * CHUNK:(level + 1) * CHUNK, hl], kl_ref[level, :, hl])
                attn = jnp.where(lvl == level, a_l, attn)
            attn_ref[hd] = attn.astype(BF16)
        b_last = b_ref[CHUNK - 1:CHUNK, :]
        chunk_outputs(rows, (q_c * jnp.exp2(b_c)).astype(BF16),
                      k_c * jnp.exp2(b_last - b_c), jnp.exp2(b_last),
                      [attn_ref[hd] for hd in range(HGRN_HEADS)])
        return carry

    def back_epilogue():
        gate = modb_ref[0][:, 2 * D_MODEL:3 * D_MODEL]
        mix = jnp.concatenate([mixp_ref[slot_b], mixh_ref[...]], axis=1)
        y = _dot(mix, w_out_ref[...])
        yn = y * lax.rsqrt(jnp.mean(y * y, axis=-1, keepdims=True) + EPS) * npost_ref[...]
        out_ref[0] = xb_ref[0] + gate * yn

    use_direct = flag_ref[slot_b] == 1

    @pl.when(use_direct)
    def _():
        front()
        for c in range(n_chunks):
            direct_chunk(c)
        back_epilogue()

    @pl.when(jnp.logical_not(use_direct))
    def _():
        front()
        lax.fori_loop(0, n_chunks, robust_chunk, 0)
        back_epilogue()


def _layer_call(layer, x, mod, npre, w_in, pool_w, pool_scale, lb, hnw, w_out, npost, wsum, lvl):
    batch, seq, d = x.shape
    ts = SEQ_TILE
    tiles_per_seq = seq // ts
    n_tiles = batch * tiles_per_seq

    def front_tile(g):
        t = jnp.minimum(g, n_tiles - 1)
        return t // tiles_per_seq, t % tiles_per_seq

    def back_tile(g):
        t = jnp.maximum(g - 1, 0)
        return t // tiles_per_seq, t % tiles_per_seq

    const2 = lambda g: (0, 0)
    of_layer3 = lambda g: (layer, 0, 0)
    of_layer4 = lambda g: (layer, 0, 0, 0)
    once = pl.Buffered(1)
    return pl.pallas_call(
        functools.partial(_layer_kernel, tiles_per_seq, n_tiles),
        grid=(n_tiles + 1,),
        in_specs=[
            pl.BlockSpec((1, ts, d), lambda g: (*front_tile(g), 0)),
            pl.BlockSpec((1, ts, d), lambda g: (*back_tile(g), 0)),
            pl.BlockSpec((None, 1, 1, 3 * d), lambda g: (layer, front_tile(g)[0], 0, 0)),
            pl.BlockSpec((None, 1, 1, 3 * d), lambda g: (layer, back_tile(g)[0], 0, 0)),
            pl.BlockSpec((None, 1, d), of_layer3),
            pl.BlockSpec((None, d, IN_WIDTH), of_layer3, pipeline_mode=once),
            pl.BlockSpec((None,) + pool_w.shape[1:], of_layer4, pipeline_mode=once),
            pl.BlockSpec((None, 1, POOL_WIDTH), of_layer3),
            pl.BlockSpec((None, 1, HGRN_WIDTH), of_layer3),
            pl.BlockSpec((None, 1, HEAD_DIM), of_layer3),
            pl.BlockSpec((None, MIX_WIDTH, d), of_layer3, pipeline_mode=once),
            pl.BlockSpec((None, 1, d), of_layer3),
            pl.BlockSpec(wsum.shape, const2),
            pl.BlockSpec(lvl.shape, const2),
        ],
        out_specs=pl.BlockSpec((1, ts, d), lambda g: (*back_tile(g), 0)),
        out_shape=jax.ShapeDtypeStruct(x.shape, F32),
        scratch_shapes=[
            pltpu.VMEM((HGRN_HEADS, HEAD_DIM, HEAD_DIM), F32),
            pltpu.VMEM((POOL_TAIL, POOL_WIDTH), F32),
            pltpu.VMEM((SLOTS, ts, HGRN_WIDTH), F32),
            pltpu.VMEM((SLOTS, ts, HGRN_WIDTH), F32),
            pltpu.VMEM((SLOTS, ts, HGRN_WIDTH), F32),
            pltpu.VMEM((SLOTS, ts, HGRN_WIDTH), F32),
            pltpu.VMEM((SLOTS, ts, HGRN_WIDTH), F32),
            pltpu.VMEM((SLOTS, ts, POOL_WIDTH), BF16),
            pltpu.SMEM((SLOTS,), jnp.int32),
            pltpu.VMEM((ts, HGRN_WIDTH), BF16),
            pltpu.VMEM((HGRN_HEADS, CHUNK, CHUNK), BF16),
            pltpu.VMEM((CHUNK, HGRN_WIDTH), F32),
            pltpu.VMEM(((NUM_LEVELS + 1) * CHUNK, HGRN_WIDTH), BF16),
            pltpu.VMEM((NUM_LEVELS + 1, CHUNK, HGRN_WIDTH), BF16),
        ],
        compiler_params=pltpu.CompilerParams(
            dimension_semantics=("arbitrary",),
            vmem_limit_bytes=VMEM_LIMIT_BYTES),
        name="hybrid_layer",
    )(x, x, mod, mod, npre, w_in, pool_w, pool_scale, lb, hnw, w_out, npost, wsum, lvl)


def kernel(x, c, norm_pre_w, ada_w, ada_b, w_in, pool_w, pool_scale, hgrn_lower_bounds,
           hgrn_norm_w, w_out, norm_post_w):
    depth = ada_w.shape[0]
    batch = x.shape[0]
    lower = _lower_bounds_call(hgrn_lower_bounds)
    mod = _adaln_call(c, ada_w, ada_b)
    wsum = jnp.asarray(_cumsum_matrix(), dtype=BF16)
    lvl = jnp.asarray(_level_map())
    stacked = (
        mod.reshape(depth, batch, 1, 3 * D_MODEL),
        norm_pre_w.astype(F32).reshape(depth, 1, D_MODEL),
        w_in.astype(BF16),
        pool_w.astype(BF16),
        pool_scale.astype(F32).reshape(depth, 1, POOL_WIDTH),
        lower.reshape(depth, 1, HGRN_WIDTH),
        hgrn_norm_w.astype(F32).reshape(depth, 1, HEAD_DIM),
        w_out.astype(BF16),
        norm_post_w.astype(F32).reshape(depth, 1, D_MODEL),
    )
    h = x.astype(F32)
    for l in range(depth):
        h = _layer_call(l, h, *stacked, wsum, lvl)
    return h.astype(x.dtype)
```

```python
import functools

import numpy as np
import jax
import jax.numpy as jnp
from jax import lax
from jax.experimental import pallas as pl
from jax.experimental.pallas import tpu as pltpu

D_MODEL = 1024
CHUNK = 64
POOL_WIDTH = 1024
POOL_WINDOWS = (2, 4, 8, 16)
POOL_GROUP_WIDTH = POOL_WIDTH // len(POOL_WINDOWS)
HGRN_WIDTH = 1024
HEAD_DIM = 128
HGRN_HEADS = HGRN_WIDTH // HEAD_DIM
MIX_WIDTH = POOL_WIDTH + HGRN_WIDTH
IN_WIDTH = 2 * POOL_WIDTH + 4 * HGRN_WIDTH
EPS = 1e-6

SEQ_TILE = 512
POOL_TAIL = 16
assert POOL_WINDOWS == tuple(2 << g for g in range(len(POOL_WINDOWS))) and POOL_TAIL >= POOL_WINDOWS[-1]
NUM_LEVELS = 6
SPLIT = 3
VMEM_LIMIT_BYTES = 62 * 1024 * 1024
SUBLANES = 8
LOG2E = 1.4426950408889634
MID_ROW = CHUNK // 2 - 1
DIRECT_MAX_LOG2 = 110.0
SLOTS = 2

F32 = jnp.float32
BF16 = jnp.bfloat16


def _ref_row(level, row):
    m = 1 << (level - 1)
    return (row // (2 * m)) * (2 * m) + m - 1


def _cumsum_matrix():
    t = np.arange(CHUNK)[:, None]
    j = np.arange(CHUNK)[None, :]
    w = (j <= t).astype(np.float32)
    return np.concatenate([w] * SPLIT, axis=1)


def _level_map():
    t = np.arange(CHUNK)[:, None]
    s = np.arange(CHUNK)[None, :]
    x = t ^ s
    lvl = np.where(x > 0, np.floor(np.log2(np.maximum(x, 1))).astype(np.int32) + 1, 0)
    return np.where(s > t, -1, lvl).astype(np.int32)


def _split_bf16(a, pieces):
    out = []
    rest = a
    for _ in range(pieces):
        p = rest.astype(BF16)
        out.append(p)
        rest = rest - p.astype(F32)
    return out


def _sigmoid(a):
    return 1.0 / (1.0 + jnp.exp(-a))


def _silu(a):
    return a * (0.5 * jnp.tanh(0.5 * a) + 0.5)


def _dot(a, b):
    return jnp.dot(a, b, preferred_element_type=F32)


def _dot_nt(a, b):
    return lax.dot_general(a, b, (((1,), (1,)), ((), ())), preferred_element_type=F32)


def _lower_bounds_kernel(lb_ref, out_ref):
    depth = lb_ref.shape[0]
    rows = [lb_ref[l:l + 1, :] for l in range(depth)]
    mx = rows[0]
    for r in rows[1:]:
        mx = jnp.maximum(mx, r)
    ex = [jnp.exp(r - mx) for r in rows]
    den = ex[0]
    for e in ex[1:]:
        den = den + e
    sm = [e / den for e in ex]
    run = sm[0]
    out_ref[0:1, :] = run - sm[0]
    for l in range(1, depth):
        run = run + sm[l]
        out_ref[l:l + 1, :] = run - sm[0]


def _lower_bounds_call(hgrn_lower_bounds):
    return pl.pallas_call(
        _lower_bounds_kernel,
        out_shape=jax.ShapeDtypeStruct(hgrn_lower_bounds.shape, F32),
        name="lower_bounds",
    )(hgrn_lower_bounds.astype(F32))


def _adaln_kernel(c_ref, w_ref, b_ref, mod_ref):
    c = c_ref[...]
    sc = c * _sigmoid(c)
    w = w_ref[0]
    a_hi, a_lo = _split_bf16(sc, 2)
    w_hi, w_lo = _split_bf16(w, 2)
    acc = _dot(a_hi, w_hi) + _dot(a_hi, w_lo) + _dot(a_lo, w_hi)
    mod_ref[0] = acc + b_ref[0]


def _adaln_call(c, ada_w, ada_b):
    depth, d, d3 = ada_w.shape
    batch = c.shape[0]
    n_col = d3 // d
    return pl.pallas_call(
        _adaln_kernel,
        grid=(depth, n_col),
        in_specs=[
            pl.BlockSpec((batch, d), lambda l, j: (0, 0)),
            pl.BlockSpec((1, d, d), lambda l, j: (l, 0, j)),
            pl.BlockSpec((1, 1, d), lambda l, j: (l, 0, j)),
        ],
        out_specs=pl.BlockSpec((1, batch, d), lambda l, j: (l, 0, j)),
        out_shape=jax.ShapeDtypeStruct((depth, batch, d3), F32),
        name="adaln",
    )(c.astype(F32), ada_w.astype(F32), ada_b.astype(F32).reshape(depth, 1, d3))


def _layer_kernel(tiles_per_seq, n_tiles,
                  xf_ref, xb_ref, modf_ref, modb_ref, npre_ref, w_in_ref, pool_w_ref,
                  pool_scale_ref, lb_ref, hnw_ref, w_out_ref, npost_ref, wsum_ref, lvl_ref,
                  out_ref,
                  state_ref, tail_ref, q_ref, k_ref, v_ref, g_ref, bsum_ref, mixp_ref, flag_ref,
                  mixh_ref, attn_ref, b_ref, ql_ref, kl_ref):
    ts = SEQ_TILE
    n_chunks = ts // CHUNK
    step = pl.program_id(0)
    front_tile = jnp.minimum(step, n_tiles - 1)
    back_tile = jnp.maximum(step - 1, 0)
    slot_f = step % SLOTS
    slot_b = (step + 1) % SLOTS

    @pl.when(step == 0)
    def _():
        for ref in (q_ref, k_ref, v_ref, g_ref, bsum_ref, mixp_ref):
            ref[...] = jnp.zeros_like(ref)
        flag_ref[1] = 1

    @pl.when(front_tile % tiles_per_seq == 0)
    def _():
        tail_ref[...] = jnp.zeros_like(tail_ref)

    @pl.when(back_tile % tiles_per_seq == 0)
    def _():
        state_ref[...] = jnp.zeros_like(state_ref)

    lvl = lvl_ref[...]
    causal = lvl >= 0
    hnw = hnw_ref[...]
    heads = [slice(hd * HEAD_DIM, (hd + 1) * HEAD_DIM) for hd in range(HGRN_HEADS)]

    def front():
        x = xf_ref[0]
        mod = modf_ref[0]
        shift = mod[:, 0:D_MODEL]
        scale = mod[:, D_MODEL:2 * D_MODEL]
        ms = jnp.mean(x * x, axis=-1, keepdims=True)
        xn = x * lax.rsqrt(ms + EPS) * npre_ref[...]
        h = (xn * (1.0 + scale) + shift).astype(BF16)

        def in_proj(col):
            return _dot(h, w_in_ref[:, col * 1024:(col + 1) * 1024])

        u = in_proj(0)
        part = jnp.concatenate([tail_ref[...], u], axis=0)
        tail_ref[...] = u[ts - POOL_TAIL:ts, :]
        win_sum = []
        for g, w in enumerate(POOL_WINDOWS):
            part = part[:, (POOL_GROUP_WIDTH if g else 0):]
            part = part + pltpu.roll(part, w // 2, 0)
            win_sum.append(part[POOL_TAIL:, 0:POOL_GROUP_WIDTH])
        g_pool = in_proj(1)
        pos = (front_tile % tiles_per_seq) * ts + lax.broadcasted_iota(jnp.int32, (ts, 1), 0)
        for g, w in enumerate(POOL_WINDOWS):
            lo, hi = g * POOL_GROUP_WIDTH, (g + 1) * POOL_GROUP_WIDTH
            u_g = u[:, lo:hi]
            inv_cnt = 1.0 / jnp.minimum(pos + 1, w).astype(F32)
            d = (win_sum[g] * inv_cnt - u_g).astype(BF16)
            y = _dot(d, pool_w_ref[g]) * pool_scale_ref[:, lo:hi]
            gp = g_pool[:, lo:hi]
            mixp_ref[slot_f, :, lo:hi] = (y * _silu(gp)).astype(BF16)

        q_ref[slot_f] = in_proj(2)
        lb = lb_ref[...]
        f = lb + (1.0 - lb) * _sigmoid(in_proj(3))
        k_ref[slot_f] = 1.0 - f
        lf = jnp.log(f) * LOG2E
        v_ref[slot_f] = in_proj(4)
        g_ref[slot_f] = in_proj(5)
        unsafe = None
        for c in range(n_chunks):
            lf3 = jnp.concatenate(_split_bf16(lf[c * CHUNK:(c + 1) * CHUNK, :], SPLIT), axis=0)
            b_c = _dot(wsum_ref[...], lf3)
            bsum_ref[slot_f, c * CHUNK:(c + 1) * CHUNK, :] = b_c
            b_mid = b_c[MID_ROW:MID_ROW + 1, :]
            b_last = b_c[CHUNK - 1:CHUNK, :]
            ok = jnp.logical_and(-b_mid <= DIRECT_MAX_LOG2, b_mid - b_last <= DIRECT_MAX_LOG2)
            bad = jnp.where(ok, 0.0, 1.0)
            unsafe = bad if unsafe is None else jnp.maximum(unsafe, bad)
        flag_ref[slot_f] = (jnp.max(unsafe) < 0.5).astype(jnp.int32)

    def chunk_outputs(rows, qe, k_dec, e_last, attn):
        v_c = v_ref[slot_b, rows, :]
        k_dec_b = k_dec.astype(BF16)
        for hd, hl in enumerate(heads):
            v_t = jnp.transpose(v_c[:, hl]).astype(BF16)
            state_t = state_ref[hd]
            o = _dot_nt(jnp.concatenate([qe[:, hl], attn[hd]], axis=1),
                        jnp.concatenate([state_t.astype(BF16), v_t], axis=1))
            on = o * lax.rsqrt(jnp.mean(o * o, axis=-1, keepdims=True) + EPS) * hnw
            gh = g_ref[slot_b, rows, hl]
            mixh_ref[rows, hl] = (on * _silu(gh)).astype(BF16)
            state_ref[hd] = state_t * e_last[:, hl] + _dot(v_t, k_dec_b[:, hl])

    def direct_chunk(c):
        rows = slice(c * CHUNK, (c + 1) * CHUNK)
        b_c = bsum_ref[slot_b, rows, :]
        b_mid = bsum_ref[slot_b, c * CHUNK + MID_ROW:c * CHUNK + MID_ROW + 1, :]
        b_last = bsum_ref[slot_b, (c + 1) * CHUNK - 1:(c + 1) * CHUNK, :]
        d = b_c - b_mid
        q_s = q_ref[slot_b, rows, :] * jnp.exp2(d)
        k_s = k_ref[slot_b, rows, :] * jnp.exp2(-d)
        q_b = q_s.astype(BF16)
        k_b = k_s.astype(BF16)
        attn = [jnp.where(causal, _dot_nt(q_b[:, hl], k_b[:, hl]), 0.0).astype(BF16) for hl in heads]
        chunk_outputs(rows, (q_s * jnp.exp2(b_mid)).astype(BF16),
                      k_s * jnp.exp2(b_last - b_mid), jnp.exp2(b_last), attn)

    def robust_chunk(c, carry):
        rows = pl.ds(pl.multiple_of(c * CHUNK, CHUNK), CHUNK)
        b_ref[...] = bsum_ref[slot_b, rows, :]
        b_c = b_ref[...]
        q_c = q_ref[slot_b, rows, :]
        k_c = k_ref[slot_b, rows, :]
        ql_ref[0:CHUNK, :] = q_c.astype(BF16)
        kl_ref[0] = k_c.astype(BF16)
        ql_ref[CHUNK:2 * CHUNK, :] = (q_c * (1.0 - k_c)).astype(BF16)
        row_in_group = lax.broadcasted_iota(jnp.int32, (SUBLANES, 1), 0)
        for level in range(2, NUM_LEVELS + 1):
            m = 1 << (level - 1)
            refs = []
            for r in range(0, CHUNK, SUBLANES):
                if 2 * m >= SUBLANES:
                    refs.append(jnp.broadcast_to(b_ref[_ref_row(level, r):_ref_row(level, r) + 1, :],
                                                 (SUBLANES, HGRN_WIDTH)))
                else:
                    lo_ref = b_ref[_ref_row(level, r):_ref_row(level, r) + 1, :]
                    hi_ref = b_ref[_ref_row(level, r + 2 * m):_ref_row(level, r + 2 * m) + 1, :]
                    refs.append(jnp.where(row_in_group < 2 * m, lo_ref, hi_ref))
            e_l = jnp.exp2(-jnp.abs(b_c - jnp.concatenate(refs, axis=0)))
            ql_ref[level * CHUNK:(level + 1) * CHUNK, :] = (q_c * e_l).astype(BF16)
            kl_ref[level] = (k_c * e_l).astype(BF16)
        for hd, hl in enumerate(heads):
            a01 = _dot_nt(ql_ref[0:2 * CHUNK, hl], kl_ref[0, :, hl])
            attn = jnp.where(lvl == 0, a01[0:CHUNK], 0.0)
            attn = jnp.where(lvl == 1, a01[CHUNK:2 * CHUNK], attn)
            for level in range(2, NUM_LEVELS + 1):
                a_l = _dot_nt(ql_ref[level * CHUNK:(level + 1) * CHUNK, hl], kl_ref[level, :, hl])
                attn = jnp.where(lvl == level, a_l, attn)
            attn_ref[hd] = attn.astype(BF16)
        b_last = b_ref[CHUNK - 1:CHUNK, :]
        chunk_outputs(rows, (q_c * jnp.exp2(b_c)).astype(BF16),
                      k_c * jnp.exp2(b_last - b_c), jnp.exp2(b_last),
                      [attn_ref[hd] for hd in range(HGRN_HEADS)])
        return carry

    def back_epilogue():
        gate = modb_ref[0][:, 2 * D_MODEL:3 * D_MODEL]
        mix = jnp.concatenate([mixp_ref[slot_b], mixh_ref[...]], axis=1)
        y = _dot(mix, w_out_ref[...])
        yn = y * lax.rsqrt(jnp.mean(y * y, axis=-1, keepdims=True) + EPS) * npost_ref[...]
        out_ref[0] = xb_ref[0] + gate * yn

    use_direct = flag_ref[slot_b] == 1

    @pl.when(use_direct)
    def _():
        front()
        for c in range(n_chunks):
            direct_chunk(c)
        back_epilogue()

    @pl.when(jnp.logical_not(use_direct))
    def _():
        front()
        lax.fori_loop(0, n_chunks, robust_chunk, 0)
        back_epilogue()


def _layer_call(layer, x, mod, npre, w_in, pool_w, pool_scale, lb, hnw, w_out, npost, wsum, lvl):
    batch, seq, d = x.shape
    ts = SEQ_TILE
    tiles_per_seq = seq // ts
    n_tiles = batch * tiles_per_seq

    def front_tile(g):
        t = jnp.minimum(g, n_tiles - 1)
        return t // tiles_per_seq, t % tiles_per_seq

    def back_tile(g):
        t = jnp.maximum(g - 1, 0)
        return t // tiles_per_seq, t % tiles_per_seq

    const2 = lambda g: (0, 0)
    of_layer3 = lambda g: (layer, 0, 0)
    of_layer4 = lambda g: (layer, 0, 0, 0)
    once = pl.Buffered(1)
    return pl.pallas_call(
        functools.partial(_layer_kernel, tiles_per_seq, n_tiles),
        grid=(n_tiles + 1,),
        in_specs=[
            pl.BlockSpec((1, ts, d), lambda g: (*front_tile(g), 0)),
            pl.BlockSpec((1, ts, d), lambda g: (*back_tile(g), 0)),
            pl.BlockSpec((None, 1, 1, 3 * d), lambda g: (layer, front_tile(g)[0], 0, 0)),
            pl.BlockSpec((None, 1, 1, 3 * d), lambda g: (layer, back_tile(g)[0], 0, 0)),
            pl.BlockSpec((None, 1, d), of_layer3),
            pl.BlockSpec((None, d, IN_WIDTH), of_layer3, pipeline_mode=once),
            pl.BlockSpec((None,) + pool_w.shape[1:], of_layer4, pipeline_mode=once),
            pl.BlockSpec((None, 1, POOL_WIDTH), of_layer3),
            pl.BlockSpec((None, 1, HGRN_WIDTH), of_layer3),
            pl.BlockSpec((None, 1, HEAD_DIM), of_layer3),
            pl.BlockSpec((None, MIX_WIDTH, d), of_layer3, pipeline_mode=once),
            pl.BlockSpec((None, 1, d), of_layer3),
            pl.BlockSpec(wsum.shape, const2),
            pl.BlockSpec(lvl.shape, const2),
        ],
        out_specs=pl.BlockSpec((1, ts, d), lambda g: (*back_tile(g), 0)),
        out_shape=jax.ShapeDtypeStruct(x.shape, F32),
        scratch_shapes=[
            pltpu.VMEM((HGRN_HEADS, HEAD_DIM, HEAD_DIM), F32),
            pltpu.VMEM((POOL_TAIL, POOL_WIDTH), F32),
            pltpu.VMEM((SLOTS, ts, HGRN_WIDTH), F32),
            pltpu.VMEM((SLOTS, ts, HGRN_WIDTH), F32),
            pltpu.VMEM((SLOTS, ts, HGRN_WIDTH), F32),
            pltpu.VMEM((SLOTS, ts, HGRN_WIDTH), F32),
            pltpu.VMEM((SLOTS, ts, HGRN_WIDTH), F32),
            pltpu.VMEM((SLOTS, ts, POOL_WIDTH), BF16),
            pltpu.SMEM((SLOTS,), jnp.int32),
            pltpu.VMEM((ts, HGRN_WIDTH), BF16),
            pltpu.VMEM((HGRN_HEADS, CHUNK, CHUNK), BF16),
            pltpu.VMEM((CHUNK, HGRN_WIDTH), F32),
            pltpu.VMEM(((NUM_LEVELS + 1) * CHUNK, HGRN_WIDTH), BF16),
            pltpu.VMEM((NUM_LEVELS + 1, CHUNK, HGRN_WIDTH), BF16),
        ],
        compiler_params=pltpu.CompilerParams(
            dimension_semantics=("arbitrary",),
            vmem_limit_bytes=VMEM_LIMIT_BYTES),
        name="hybrid_layer",
    )(x, x, mod, mod, npre, w_in, pool_w, pool_scale, lb, hnw, w_out, npost, wsum, lvl)


def kernel(x, c, norm_pre_w, ada_w, ada_b, w_in, pool_w, pool_scale, hgrn_lower_bounds,
           hgrn_norm_w, w_out, norm_post_w):
    depth = ada_w.shape[0]
    batch = x.shape[0]
    lower = _lower_bounds_call(hgrn_lower_bounds)
    mod = _adaln_call(c, ada_w, ada_b)
    wsum = jnp.asarray(_cumsum_matrix(), dtype=BF16)
    lvl = jnp.asarray(_level_map())
    stacked = (
        mod.reshape(depth, batch, 1, 3 * D_MODEL),
        norm_pre_w.astype(F32).reshape(depth, 1, D_MODEL),
        w_in.astype(BF16),
        pool_w.astype(BF16),
        pool_scale.astype(F32).reshape(depth, 1, POOL_WIDTH),
        lower.reshape(depth, 1, HGRN_WIDTH),
        hgrn_norm_w.astype(F32).reshape(depth, 1, HEAD_DIM),
        w_out.astype(BF16),
        norm_post_w.astype(F32).reshape(depth, 1, D_MODEL),
    )
    h = x.astype(F32)
    for l in range(depth):
        h = _layer_call(l, h, *stacked, wsum, lvl)
    return h.astype(x.dtype)
```

```python
import functools

import numpy as np
import jax
import jax.numpy as jnp
from jax import lax
from jax.experimental import pallas as pl
from jax.experimental.pallas import tpu as pltpu

D_MODEL = 1024
CHUNK = 64
POOL_WIDTH = 1024
POOL_WINDOWS = (2, 4, 8, 16)
POOL_GROUP_WIDTH = POOL_WIDTH // len(POOL_WINDOWS)
HGRN_WIDTH = 1024
HEAD_DIM = 128
HGRN_HEADS = HGRN_WIDTH // HEAD_DIM
MIX_WIDTH = POOL_WIDTH + HGRN_WIDTH
IN_WIDTH = 2 * POOL_WIDTH + 4 * HGRN_WIDTH
EPS = 1e-6

SEQ_TILE = 512
POOL_TAIL = 16
assert POOL_WINDOWS == tuple(2 << g for g in range(len(POOL_WINDOWS))) and POOL_TAIL >= POOL_WINDOWS[-1]
NUM_LEVELS = 6
SPLIT = 3
VMEM_LIMIT_BYTES = 62 * 1024 * 1024
SUBLANES = 8
LOG2E = 1.4426950408889634
MID_ROW = CHUNK // 2 - 1
DIRECT_MAX_LOG2 = 110.0
SLOTS = 2

F32 = jnp.float32
BF16 = jnp.bfloat16


def _ref_row(level, row):
    m = 1 << (level - 1)
    return (row // (2 * m)) * (2 * m) + m - 1


def _cumsum_matrix():
    t = np.arange(CHUNK)[:, None]
    j = np.arange(CHUNK)[None, :]
    w = (j <= t).astype(np.float32)
    return np.concatenate([w] * SPLIT, axis=1)


def _level_map():
    t = np.arange(CHUNK)[:, None]
    s = np.arange(CHUNK)[None, :]
    x = t ^ s
    lvl = np.where(x > 0, np.floor(np.log2(np.maximum(x, 1))).astype(np.int32) + 1, 0)
    return np.where(s > t, -1, lvl).astype(np.int32)


def _split_bf16(a, pieces):
    out = []
    rest = a
    for _ in range(pieces):
        p = rest.astype(BF16)
        out.append(p)
        rest = rest - p.astype(F32)
    return out


def _sigmoid(a):
    return 1.0 / (1.0 + jnp.exp(-a))


def _silu(a):
    return a * (0.5 * jnp.tanh(0.5 * a) + 0.5)


def _pack_rows(w):
    *lead, k, n = w.shape
    pairs = jnp.swapaxes(w.reshape(*lead, k // 2, 2, n), -1, -2)
    return lax.bitcast_convert_type(pairs, jnp.uint32)


def _unpack_rows(words):
    return pltpu.bitcast(words, BF16)


def _dot(a, b):
    return jnp.dot(a, b, preferred_element_type=F32)


def _dot_nt(a, b):
    return lax.dot_general(a, b, (((1,), (1,)), ((), ())), preferred_element_type=F32)


def _lower_bounds_kernel(lb_ref, out_ref):
    depth = lb_ref.shape[0]
    rows = [lb_ref[l:l + 1, :] for l in range(depth)]
    mx = rows[0]
    for r in rows[1:]:
        mx = jnp.maximum(mx, r)
    ex = [jnp.exp(r - mx) for r in rows]
    den = ex[0]
    for e in ex[1:]:
        den = den + e
    sm = [e / den for e in ex]
    run = sm[0]
    out_ref[0:1, :] = run - sm[0]
    for l in range(1, depth):
        run = run + sm[l]
        out_ref[l:l + 1, :] = run - sm[0]


def _lower_bounds_call(hgrn_lower_bounds):
    return pl.pallas_call(
        _lower_bounds_kernel,
        out_shape=jax.ShapeDtypeStruct(hgrn_lower_bounds.shape, F32),
        name="lower_bounds",
    )(hgrn_lower_bounds.astype(F32))


def _adaln_kernel(c_ref, w_ref, b_ref, mod_ref):
    c = c_ref[...]
    sc = c * _sigmoid(c)
    w = w_ref[0]
    a_hi, a_lo = _split_bf16(sc, 2)
    w_hi, w_lo = _split_bf16(w, 2)
    acc = _dot(a_hi, w_hi) + _dot(a_hi, w_lo) + _dot(a_lo, w_hi)
    mod_ref[0] = acc + b_ref[0]


def _adaln_call(c, ada_w, ada_b):
    depth, d, d3 = ada_w.shape
    batch = c.shape[0]
    n_col = d3 // d
    return pl.pallas_call(
        _adaln_kernel,
        grid=(depth, n_col),
        in_specs=[
            pl.BlockSpec((batch, d), lambda l, j: (0, 0)),
            pl.BlockSpec((1, d, d), lambda l, j: (l, 0, j)),
            pl.BlockSpec((1, 1, d), lambda l, j: (l, 0, j)),
        ],
        out_specs=pl.BlockSpec((1, batch, d), lambda l, j: (l, 0, j)),
        out_shape=jax.ShapeDtypeStruct((depth, batch, d3), F32),
        name="adaln",
    )(c.astype(F32), ada_w.astype(F32), ada_b.astype(F32).reshape(depth, 1, d3))


def _layer_kernel(tiles_per_seq, n_tiles,
                  xf_ref, xb_ref, modf_ref, modb_ref, npre_ref, w_in_ref, pool_w_ref,
                  pool_scale_ref, lb_ref, hnw_ref, w_out_ref, npost_ref, wsum_ref, lvl_ref,
                  out_ref,
                  state_ref, tail_ref, q_ref, k_ref, v_ref, g_ref, bsum_ref, mixp_ref, flag_ref,
                  mixh_ref, attn_ref, b_ref, ql_ref, kl_ref):
    ts = SEQ_TILE
    n_chunks = ts // CHUNK
    step = pl.program_id(0)
    front_tile = jnp.minimum(step, n_tiles - 1)
    back_tile = jnp.maximum(step - 1, 0)
    slot_f = step % SLOTS
    slot_b = (step + 1) % SLOTS

    @pl.when(step == 0)
    def _():
        for ref in (q_ref, k_ref, v_ref, g_ref, bsum_ref, mixp_ref):
            ref[...] = jnp.zeros_like(ref)
        flag_ref[1] = 1

    @pl.when(front_tile % tiles_per_seq == 0)
    def _():
        tail_ref[...] = jnp.zeros_like(tail_ref)

    @pl.when(back_tile % tiles_per_seq == 0)
    def _():
        state_ref[...] = jnp.zeros_like(state_ref)

    lvl = lvl_ref[...]
    causal = lvl >= 0
    hnw = hnw_ref[...]
    heads = [slice(hd * HEAD_DIM, (hd + 1) * HEAD_DIM) for hd in range(HGRN_HEADS)]

    def front():
        x = xf_ref[0]
        mod = modf_ref[0]
        shift = mod[:, 0:D_MODEL]
        scale = mod[:, D_MODEL:2 * D_MODEL]
        ms = jnp.mean(x * x, axis=-1, keepdims=True)
        xn = x * lax.rsqrt(ms + EPS) * npre_ref[...]
        h = (xn * (1.0 + scale) + shift).astype(BF16)

        def in_proj(col):
            return _dot(h, _unpack_rows(w_in_ref[:, col * 1024:(col + 1) * 1024]))

        u = in_proj(0)
        part = jnp.concatenate([tail_ref[...], u], axis=0)
        tail_ref[...] = u[ts - POOL_TAIL:ts, :]
        win_sum = []
        for g, w in enumerate(POOL_WINDOWS):
            part = part[:, (POOL_GROUP_WIDTH if g else 0):]
            part = part + pltpu.roll(part, w // 2, 0)
            win_sum.append(part[POOL_TAIL:, 0:POOL_GROUP_WIDTH])
        g_pool = in_proj(1)
        pos = (front_tile % tiles_per_seq) * ts + lax.broadcasted_iota(jnp.int32, (ts, 1), 0)
        for g, w in enumerate(POOL_WINDOWS):
            lo, hi = g * POOL_GROUP_WIDTH, (g + 1) * POOL_GROUP_WIDTH
            u_g = u[:, lo:hi]
            inv_cnt = 1.0 / jnp.minimum(pos + 1, w).astype(F32)
            d = (win_sum[g] * inv_cnt - u_g).astype(BF16)
            y = _dot(d, pool_w_ref[g]) * pool_scale_ref[:, lo:hi]
            gp = g_pool[:, lo:hi]
            mixp_ref[slot_f, :, lo:hi] = (y * _silu(gp)).astype(BF16)

        q_ref[slot_f] = in_proj(2)
        lb = lb_ref[...]
        f = lb + (1.0 - lb) * _sigmoid(in_proj(3))
        k_ref[slot_f] = 1.0 - f
        lf = jnp.log(f) * LOG2E
        v_ref[slot_f] = in_proj(4)
        g_ref[slot_f] = in_proj(5)
        unsafe = None
        for c in range(n_chunks):
            lf3 = jnp.concatenate(_split_bf16(lf[c * CHUNK:(c + 1) * CHUNK, :], SPLIT), axis=0)
            b_c = _dot(wsum_ref[...], lf3)
            bsum_ref[slot_f, c * CHUNK:(c + 1) * CHUNK, :] = b_c
            b_mid = b_c[MID_ROW:MID_ROW + 1, :]
            b_last = b_c[CHUNK - 1:CHUNK, :]
            ok = jnp.logical_and(-b_mid <= DIRECT_MAX_LOG2, b_mid - b_last <= DIRECT_MAX_LOG2)
            bad = jnp.where(ok, 0.0, 1.0)
            unsafe = bad if unsafe is None else jnp.maximum(unsafe, bad)
        flag_ref[slot_f] = (jnp.max(unsafe) < 0.5).astype(jnp.int32)

    def chunk_outputs(rows, qe, k_dec, e_last, attn):
        v_c = v_ref[slot_b, rows, :]
        k_dec_b = k_dec.astype(BF16)
        for hd, hl in enumerate(heads):
            v_t = jnp.transpose(v_c[:, hl]).astype(BF16)
            state_t = state_ref[hd]
            o = _dot_nt(jnp.concatenate([qe[:, hl], attn[hd]], axis=1),
                        jnp.concatenate([state_t.astype(BF16), v_t], axis=1))
            on = o * lax.rsqrt(jnp.mean(o * o, axis=-1, keepdims=True) + EPS) * hnw
            gh = g_ref[slot_b, rows, hl]
            mixh_ref[rows, hl] = (on * _silu(gh)).astype(BF16)
            state_ref[hd] = state_t * e_last[:, hl] + _dot(v_t, k_dec_b[:, hl])

    def direct_chunk(c):
        rows = slice(c * CHUNK, (c + 1) * CHUNK)
        b_c = bsum_ref[slot_b, rows, :]
        b_mid = bsum_ref[slot_b, c * CHUNK + MID_ROW:c * CHUNK + MID_ROW + 1, :]
        b_last = bsum_ref[slot_b, (c + 1) * CHUNK - 1:(c + 1) * CHUNK, :]
        d = b_c - b_mid
        q_s = q_ref[slot_b, rows, :] * jnp.exp2(d)
        k_s = k_ref[slot_b, rows, :] * jnp.exp2(-d)
        q_b = q_s.astype(BF16)
        k_b = k_s.astype(BF16)
        attn = [jnp.where(causal, _dot_nt(q_b[:, hl], k_b[:, hl]), 0.0).astype(BF16) for hl in heads]
        chunk_outputs(rows, (q_s * jnp.exp2(b_mid)).astype(BF16),
                      k_s * jnp.exp2(b_last - b_mid), jnp.exp2(b_last), attn)

    def robust_chunk(c, carry):
        rows = pl.ds(pl.multiple_of(c * CHUNK, CHUNK), CHUNK)
        b_ref[...] = bsum_ref[slot_b, rows, :]
        b_c = b_ref[...]
        q_c = q_ref[slot_b, rows, :]
        k_c = k_ref[slot_b, rows, :]
        ql_ref[0:CHUNK, :] = q_c.astype(BF16)
        kl_ref[0] = k_c.astype(BF16)
        ql_ref[CHUNK:2 * CHUNK, :] = (q_c * (1.0 - k_c)).astype(BF16)
        row_in_group = lax.broadcasted_iota(jnp.int32, (SUBLANES, 1), 0)
        for level in range(2, NUM_LEVELS + 1):
            m = 1 << (level - 1)
            refs = []
            for r in range(0, CHUNK, SUBLANES):
                if 2 * m >= SUBLANES:
                    refs.append(jnp.broadcast_to(b_ref[_ref_row(level, r):_ref_row(level, r) + 1, :],
                                                 (SUBLANES, HGRN_WIDTH)))
                else:
                    lo_ref = b_ref[_ref_row(level, r):_ref_row(level, r) + 1, :]
                    hi_ref = b_ref[_ref_row(level, r + 2 * m):_ref_row(level, r + 2 * m) + 1, :]
                    refs.append(jnp.where(row_in_group < 2 * m, lo_ref, hi_ref))
            e_l = jnp.exp2(-jnp.abs(b_c - jnp.concatenate(refs, axis=0)))
            ql_ref[level * CHUNK:(level + 1) * CHUNK, :] = (q_c * e_l).astype(BF16)
            kl_ref[level] = (k_c * e_l).astype(BF16)
        for hd, hl in enumerate(heads):
            a01 = _dot_nt(ql_ref[0:2 * CHUNK, hl], kl_ref[0, :, hl])
            attn = jnp.where(lvl == 0, a01[0:CHUNK], 0.0)
            attn = jnp.where(lvl == 1, a01[CHUNK:2 * CHUNK], attn)
            for level in range(2, NUM_LEVELS + 1):
                a_l = _dot_nt(ql_ref[level * CHUNK:(level + 1) * CHUNK, hl], kl_ref[level, :, hl])
                attn = jnp.where(lvl == level, a_l, attn)
            attn_ref[hd] = attn.astype(BF16)
        b_last = b_ref[CHUNK - 1:CHUNK, :]
        chunk_outputs(rows, (q_c * jnp.exp2(b_c)).astype(BF16),
                      k_c * jnp.exp2(b_last - b_c), jnp.exp2(b_last),
                      [attn_ref[hd] for hd in range(HGRN_HEADS)])
        return carry

    def back_epilogue():
        gate = modb_ref[0][:, 2 * D_MODEL:3 * D_MODEL]
        mix = jnp.concatenate([mixp_ref[slot_b], mixh_ref[...]], axis=1)
        y = _dot(mix, _unpack_rows(w_out_ref[...]))
        yn = y * lax.rsqrt(jnp.mean(y * y, axis=-1, keepdims=True) + EPS) * npost_ref[...]
        out_ref[0] = xb_ref[0] + gate * yn

    use_direct = flag_ref[slot_b] == 1

    @pl.when(use_direct)
    def _():
        front()
        for c in range(n_chunks):
            direct_chunk(c)
        back_epilogue()

    @pl.when(jnp.logical_not(use_direct))
    def _():
        front()
        lax.fori_loop(0, n_chunks, robust_chunk, 0)
        back_epilogue()


def _layer_call(layer, x, mod, npre, w_in, pool_w, pool_scale, lb, hnw, w_out, npost, wsum, lvl):
    batch, seq, d = x.shape
    ts = SEQ_TILE
    tiles_per_seq = seq // ts
    n_tiles = batch * tiles_per_seq

    def front_tile(g):
        t = jnp.minimum(g, n_tiles - 1)
        return t // tiles_per_seq, t % tiles_per_seq

    def back_tile(g):
        t = jnp.maximum(g - 1, 0)
        return t // tiles_per_seq, t % tiles_per_seq

    const2 = lambda g: (0, 0)
    of_layer3 = lambda g: (layer, 0, 0)
    of_layer4 = lambda g: (layer, 0, 0, 0)
    once = pl.Buffered(1)
    return pl.pallas_call(
        functools.partial(_layer_kernel, tiles_per_seq, n_tiles),
        grid=(n_tiles + 1,),
        in_specs=[
            pl.BlockSpec((1, ts, d), lambda g: (*front_tile(g), 0)),
            pl.BlockSpec((1, ts, d), lambda g: (*back_tile(g), 0)),
            pl.BlockSpec((None, 1, 1, 3 * d), lambda g: (layer, front_tile(g)[0], 0, 0)),
            pl.BlockSpec((None, 1, 1, 3 * d), lambda g: (layer, back_tile(g)[0], 0, 0)),
            pl.BlockSpec((None, 1, d), of_layer3),
            pl.BlockSpec((None, d // 2, IN_WIDTH), of_layer3, pipeline_mode=once),
            pl.BlockSpec((None,) + pool_w.shape[1:], of_layer4, pipeline_mode=once),
            pl.BlockSpec((None, 1, POOL_WIDTH), of_layer3),
            pl.BlockSpec((None, 1, HGRN_WIDTH), of_layer3),
            pl.BlockSpec((None, 1, HEAD_DIM), of_layer3),
            pl.BlockSpec((None, MIX_WIDTH // 2, d), of_layer3, pipeline_mode=once),
            pl.BlockSpec((None, 1, d), of_layer3),
            pl.BlockSpec(wsum.shape, const2),
            pl.BlockSpec(lvl.shape, const2),
        ],
        out_specs=pl.BlockSpec((1, ts, d), lambda g: (*back_tile(g), 0)),
        out_shape=jax.ShapeDtypeStruct(x.shape, F32),
        scratch_shapes=[
            pltpu.VMEM((HGRN_HEADS, HEAD_DIM, HEAD_DIM), F32),
            pltpu.VMEM((POOL_TAIL, POOL_WIDTH), F32),
            pltpu.VMEM((SLOTS, ts, HGRN_WIDTH), F32),
            pltpu.VMEM((SLOTS, ts, HGRN_WIDTH), F32),
            pltpu.VMEM((SLOTS, ts, HGRN_WIDTH), F32),
            pltpu.VMEM((SLOTS, ts, HGRN_WIDTH), F32),
            pltpu.VMEM((SLOTS, ts, HGRN_WIDTH), F32),
            pltpu.VMEM((SLOTS, ts, POOL_WIDTH), BF16),
            pltpu.SMEM((SLOTS,), jnp.int32),
            pltpu.VMEM((ts, HGRN_WIDTH), BF16),
            pltpu.VMEM((HGRN_HEADS, CHUNK, CHUNK), BF16),
            pltpu.VMEM((CHUNK, HGRN_WIDTH), F32),
            pltpu.VMEM(((NUM_LEVELS + 1) * CHUNK, HGRN_WIDTH), BF16),
            pltpu.VMEM((NUM_LEVELS + 1, CHUNK, HGRN_WIDTH), BF16),
        ],
        compiler_params=pltpu.CompilerParams(
            dimension_semantics=("arbitrary",),
            vmem_limit_bytes=VMEM_LIMIT_BYTES),
        name="hybrid_layer",
    )(x, x, mod, mod, npre, w_in, pool_w, pool_scale, lb, hnw, w_out, npost, wsum, lvl)


def kernel(x, c, norm_pre_w, ada_w, ada_b, w_in, pool_w, pool_scale, hgrn_lower_bounds,
           hgrn_norm_w, w_out, norm_post_w):
    depth = ada_w.shape[0]
    batch = x.shape[0]
    lower = _lower_bounds_call(hgrn_lower_bounds)
    mod = _adaln_call(c, ada_w, ada_b)
    wsum = jnp.asarray(_cumsum_matrix(), dtype=BF16)
    lvl = jnp.asarray(_level_map())
    stacked = (
        mod.reshape(depth, batch, 1, 3 * D_MODEL),
        norm_pre_w.astype(F32).reshape(depth, 1, D_MODEL),
        _pack_rows(w_in.astype(BF16)),
        pool_w.astype(BF16),
        pool_scale.astype(F32).reshape(depth, 1, POOL_WIDTH),
        lower.reshape(depth, 1, HGRN_WIDTH),
        hgrn_norm_w.astype(F32).reshape(depth, 1, HEAD_DIM),
        _pack_rows(w_out.astype(BF16)),
        norm_post_w.astype(F32).reshape(depth, 1, D_MODEL),
    )
    h = x.astype(F32)
    for l in range(depth):
        h = _layer_call(l, h, *stacked, wsum, lvl)
    return h.astype(x.dtype)
```

```python
import functools

import numpy as np
import jax
import jax.numpy as jnp
from jax import lax
from jax.experimental import pallas as pl
from jax.experimental.pallas import tpu as pltpu

D_MODEL = 1024
CHUNK = 64
POOL_WIDTH = 1024
POOL_WINDOWS = (2, 4, 8, 16)
POOL_GROUP_WIDTH = POOL_WIDTH // len(POOL_WINDOWS)
HGRN_WIDTH = 1024
HEAD_DIM = 128
HGRN_HEADS = HGRN_WIDTH // HEAD_DIM
MIX_WIDTH = POOL_WIDTH + HGRN_WIDTH
IN_WIDTH = 2 * POOL_WIDTH + 4 * HGRN_WIDTH
EPS = 1e-6

SEQ_TILE = 512
POOL_TAIL = 16
assert POOL_WINDOWS == tuple(2 << g for g in range(len(POOL_WINDOWS))) and POOL_TAIL >= POOL_WINDOWS[-1]
NUM_LEVELS = 6
SPLIT = 3
VMEM_LIMIT_BYTES = 62 * 1024 * 1024
SUBLANES = 8
LOG2E = 1.4426950408889634
MID_ROW = CHUNK // 2 - 1
DIRECT_MAX_LOG2 = 110.0
SLOTS = 2

F32 = jnp.float32
BF16 = jnp.bfloat16


def _ref_row(level, row):
    m = 1 << (level - 1)
    return (row // (2 * m)) * (2 * m) + m - 1


def _cumsum_matrix():
    t = np.arange(CHUNK)[:, None]
    j = np.arange(CHUNK)[None, :]
    w = (j <= t).astype(np.float32)
    return np.concatenate([w] * SPLIT, axis=1)


def _level_map():
    t = np.arange(CHUNK)[:, None]
    s = np.arange(CHUNK)[None, :]
    x = t ^ s
    lvl = np.where(x > 0, np.floor(np.log2(np.maximum(x, 1))).astype(np.int32) + 1, 0)
    return np.where(s > t, -1, lvl).astype(np.int32)


def _split_bf16(a, pieces):
    out = []
    rest = a
    for _ in range(pieces):
        p = rest.astype(BF16)
        out.append(p)
        rest = rest - p.astype(F32)
    return out


def _sigmoid(a):
    return 1.0 / (1.0 + jnp.exp(-a))


def _silu(a):
    return a * (0.5 * jnp.tanh(0.5 * a) + 0.5)


PACK_ROWS, PACK_COLS = 512, 1024


def _pack_rows_kernel(w_ref, out_ref):
    out_ref[0] = pltpu.bitcast(w_ref[0].astype(BF16), jnp.uint32)


def _pack_rows(w):
    layers, k, n = w.shape
    rows, cols = min(PACK_ROWS, k), min(PACK_COLS, n)
    return pl.pallas_call(
        _pack_rows_kernel,
        grid=(layers, k // rows, n // cols),
        in_specs=[pl.BlockSpec((1, rows, cols), lambda l, i, j: (l, i, j))],
        out_specs=pl.BlockSpec((1, rows // 2, cols), lambda l, i, j: (l, i, j)),
        out_shape=jax.ShapeDtypeStruct((layers, k // 2, n), jnp.uint32),
        name="pack_rows",
    )(w.astype(F32))


def _unpack_rows(words):
    return pltpu.bitcast(words, BF16)


def _dot(a, b):
    return jnp.dot(a, b, preferred_element_type=F32)


def _dot_nt(a, b):
    return lax.dot_general(a, b, (((1,), (1,)), ((), ())), preferred_element_type=F32)


def _lower_bounds_kernel(lb_ref, out_ref):
    depth = lb_ref.shape[0]
    rows = [lb_ref[l:l + 1, :] for l in range(depth)]
    mx = rows[0]
    for r in rows[1:]:
        mx = jnp.maximum(mx, r)
    ex = [jnp.exp(r - mx) for r in rows]
    den = ex[0]
    for e in ex[1:]:
        den = den + e
    sm = [e / den for e in ex]
    run = sm[0]
    out_ref[0:1, :] = run - sm[0]
    for l in range(1, depth):
        run = run + sm[l]
        out_ref[l:l + 1, :] = run - sm[0]


def _lower_bounds_call(hgrn_lower_bounds):
    return pl.pallas_call(
        _lower_bounds_kernel,
        out_shape=jax.ShapeDtypeStruct(hgrn_lower_bounds.shape, F32),
        name="lower_bounds",
    )(hgrn_lower_bounds.astype(F32))


def _adaln_kernel(c_ref, w_ref, b_ref, mod_ref):
    c = c_ref[...]
    sc = c * _sigmoid(c)
    w = w_ref[0]
    a_hi, a_lo = _split_bf16(sc, 2)
    w_hi, w_lo = _split_bf16(w, 2)
    acc = _dot(a_hi, w_hi) + _dot(a_hi, w_lo) + _dot(a_lo, w_hi)
    mod_ref[0] = acc + b_ref[0]


def _adaln_call(c, ada_w, ada_b):
    depth, d, d3 = ada_w.shape
    batch = c.shape[0]
    n_col = d3 // d
    return pl.pallas_call(
        _adaln_kernel,
        grid=(depth, n_col),
        in_specs=[
            pl.BlockSpec((batch, d), lambda l, j: (0, 0)),
            pl.BlockSpec((1, d, d), lambda l, j: (l, 0, j)),
            pl.BlockSpec((1, 1, d), lambda l, j: (l, 0, j)),
        ],
        out_specs=pl.BlockSpec((1, batch, d), lambda l, j: (l, 0, j)),
        out_shape=jax.ShapeDtypeStruct((depth, batch, d3), F32),
        name="adaln",
    )(c.astype(F32), ada_w.astype(F32), ada_b.astype(F32).reshape(depth, 1, d3))


def _layer_kernel(tiles_per_seq, n_tiles,
                  xf_ref, xb_ref, modf_ref, modb_ref, npre_ref, w_in_ref, pool_w_ref,
                  pool_scale_ref, lb_ref, hnw_ref, w_out_ref, npost_ref, wsum_ref, lvl_ref,
                  out_ref,
                  state_ref, tail_ref, q_ref, k_ref, v_ref, g_ref, bsum_ref, mixp_ref, flag_ref,
                  mixh_ref, attn_ref, b_ref, ql_ref, kl_ref):
    ts = SEQ_TILE
    n_chunks = ts // CHUNK
    step = pl.program_id(0)
    front_tile = jnp.minimum(step, n_tiles - 1)
    back_tile = jnp.maximum(step - 1, 0)
    slot_f = step % SLOTS
    slot_b = (step + 1) % SLOTS

    @pl.when(step == 0)
    def _():
        for ref in (q_ref, k_ref, v_ref, g_ref, bsum_ref, mixp_ref):
            ref[...] = jnp.zeros_like(ref)
        flag_ref[1] = 1

    @pl.when(front_tile % tiles_per_seq == 0)
    def _():
        tail_ref[...] = jnp.zeros_like(tail_ref)

    @pl.when(back_tile % tiles_per_seq == 0)
    def _():
        state_ref[...] = jnp.zeros_like(state_ref)

    lvl = lvl_ref[...]
    causal = lvl >= 0
    hnw = hnw_ref[...]
    heads = [slice(hd * HEAD_DIM, (hd + 1) * HEAD_DIM) for hd in range(HGRN_HEADS)]

    def front():
        x = xf_ref[0]
        mod = modf_ref[0]
        shift = mod[:, 0:D_MODEL]
        scale = mod[:, D_MODEL:2 * D_MODEL]
        ms = jnp.mean(x * x, axis=-1, keepdims=True)
        xn = x * lax.rsqrt(ms + EPS) * npre_ref[...]
        h = (xn * (1.0 + scale) + shift).astype(BF16)

        def in_proj(col):
            return _dot(h, _unpack_rows(w_in_ref[:, col * 1024:(col + 1) * 1024]))

        u = in_proj(0)
        part = jnp.concatenate([tail_ref[...], u], axis=0)
        tail_ref[...] = u[ts - POOL_TAIL:ts, :]
        win_sum = []
        for g, w in enumerate(POOL_WINDOWS):
            part = part[:, (POOL_GROUP_WIDTH if g else 0):]
            part = part + pltpu.roll(part, w // 2, 0)
            win_sum.append(part[POOL_TAIL:, 0:POOL_GROUP_WIDTH])
        g_pool = in_proj(1)
        pos = (front_tile % tiles_per_seq) * ts + lax.broadcasted_iota(jnp.int32, (ts, 1), 0)
        for g, w in enumerate(POOL_WINDOWS):
            lo, hi = g * POOL_GROUP_WIDTH, (g + 1) * POOL_GROUP_WIDTH
            u_g = u[:, lo:hi]
            inv_cnt = 1.0 / jnp.minimum(pos + 1, w).astype(F32)
            d = (win_sum[g] * inv_cnt - u_g).astype(BF16)
            y = _dot(d, pool_w_ref[g]) * pool_scale_ref[:, lo:hi]
            gp = g_pool[:, lo:hi]
            mixp_ref[slot_f, :, lo:hi] = (y * _silu(gp)).astype(BF16)

        q_ref[slot_f] = in_proj(2)
        lb = lb_ref[...]
        f = lb + (1.0 - lb) * _sigmoid(in_proj(3))
        k_ref[slot_f] = 1.0 - f
        lf = jnp.log(f) * LOG2E
        v_ref[slot_f] = in_proj(4)
        g_ref[slot_f] = in_proj(5)
        unsafe = None
        for c in range(n_chunks):
            lf3 = jnp.concatenate(_split_bf16(lf[c * CHUNK:(c + 1) * CHUNK, :], SPLIT), axis=0)
            b_c = _dot(wsum_ref[...], lf3)
            bsum_ref[slot_f, c * CHUNK:(c + 1) * CHUNK, :] = b_c
            b_mid = b_c[MID_ROW:MID_ROW + 1, :]
            b_last = b_c[CHUNK - 1:CHUNK, :]
            ok = jnp.logical_and(-b_mid <= DIRECT_MAX_LOG2, b_mid - b_last <= DIRECT_MAX_LOG2)
            bad = jnp.where(ok, 0.0, 1.0)
            unsafe = bad if unsafe is None else jnp.maximum(unsafe, bad)
        flag_ref[slot_f] = (jnp.max(unsafe) < 0.5).astype(jnp.int32)

    def chunk_outputs(rows, qe, k_dec, e_last, attn):
        v_c = v_ref[slot_b, rows, :]
        k_dec_b = k_dec.astype(BF16)
        for hd, hl in enumerate(heads):
            v_t = jnp.transpose(v_c[:, hl]).astype(BF16)
            state_t = state_ref[hd]
            o = _dot_nt(jnp.concatenate([qe[:, hl], attn[hd]], axis=1),
                        jnp.concatenate([state_t.astype(BF16), v_t], axis=1))
            on = o * lax.rsqrt(jnp.mean(o * o, axis=-1, keepdims=True) + EPS) * hnw
            gh = g_ref[slot_b, rows, hl]
            mixh_ref[rows, hl] = (on * _silu(gh)).astype(BF16)
            state_ref[hd] = state_t * e_last[:, hl] + _dot(v_t, k_dec_b[:, hl])

    def direct_chunk(c):
        rows = slice(c * CHUNK, (c + 1) * CHUNK)
        b_c = bsum_ref[slot_b, rows, :]
        b_mid = bsum_ref[slot_b, c * CHUNK + MID_ROW:c * CHUNK + MID_ROW + 1, :]
        b_last = bsum_ref[slot_b, (c + 1) * CHUNK - 1:(c + 1) * CHUNK, :]
        d = b_c - b_mid
        q_s = q_ref[slot_b, rows, :] * jnp.exp2(d)
        k_s = k_ref[slot_b, rows, :] * jnp.exp2(-d)
        q_b = q_s.astype(BF16)
        k_b = k_s.astype(BF16)
        attn = [jnp.where(causal, _dot_nt(q_b[:, hl], k_b[:, hl]), 0.0).astype(BF16) for hl in heads]
        chunk_outputs(rows, (q_s * jnp.exp2(b_mid)).astype(BF16),
                      k_s * jnp.exp2(b_last - b_mid), jnp.exp2(b_last), attn)

    def robust_chunk(c, carry):
        rows = pl.ds(pl.multiple_of(c * CHUNK, CHUNK), CHUNK)
        b_ref[...] = bsum_ref[slot_b, rows, :]
        b_c = b_ref[...]
        q_c = q_ref[slot_b, rows, :]
        k_c = k_ref[slot_b, rows, :]
        ql_ref[0:CHUNK, :] = q_c.astype(BF16)
        kl_ref[0] = k_c.astype(BF16)
        ql_ref[CHUNK:2 * CHUNK, :] = (q_c * (1.0 - k_c)).astype(BF16)
        row_in_group = lax.broadcasted_iota(jnp.int32, (SUBLANES, 1), 0)
        for level in range(2, NUM_LEVELS + 1):
            m = 1 << (level - 1)
            refs = []
            for r in range(0, CHUNK, SUBLANES):
                if 2 * m >= SUBLANES:
                    refs.append(jnp.broadcast_to(b_ref[_ref_row(level, r):_ref_row(level, r) + 1, :],
                                                 (SUBLANES, HGRN_WIDTH)))
                else:
                    lo_ref = b_ref[_ref_row(level, r):_ref_row(level, r) + 1, :]
                    hi_ref = b_ref[_ref_row(level, r + 2 * m):_ref_row(level, r + 2 * m) + 1, :]
                    refs.append(jnp.where(row_in_group < 2 * m, lo_ref, hi_ref))
            e_l = jnp.exp2(-jnp.abs(b_c - jnp.concatenate(refs, axis=0)))
            ql_ref[level * CHUNK:(level + 1) * CHUNK, :] = (q_c * e_l).astype(BF16)
            kl_ref[level] = (k_c * e_l).astype(BF16)
        for hd, hl in enumerate(heads):
            a01 = _dot_nt(ql_ref[0:2 * CHUNK, hl], kl_ref[0, :, hl])
            attn = jnp.where(lvl == 0, a01[0:CHUNK], 0.0)
            attn = jnp.where(lvl == 1, a01[CHUNK:2 * CHUNK], attn)
            for level in range(2, NUM_LEVELS + 1):
                a_l = _dot_nt(ql_ref[level * CHUNK:(level + 1) * CHUNK, hl], kl_ref[level, :, hl])
                attn = jnp.where(lvl == level, a_l, attn)
            attn_ref[hd] = attn.astype(BF16)
        b_last = b_ref[CHUNK - 1:CHUNK, :]
        chunk_outputs(rows, (q_c * jnp.exp2(b_c)).astype(BF16),
                      k_c * jnp.exp2(b_last - b_c), jnp.exp2(b_last),
                      [attn_ref[hd] for hd in range(HGRN_HEADS)])
        return carry

    def back_epilogue():
        gate = modb_ref[0][:, 2 * D_MODEL:3 * D_MODEL]
        mix = jnp.concatenate([mixp_ref[slot_b], mixh_ref[...]], axis=1)
        y = _dot(mix, _unpack_rows(w_out_ref[...]))
        yn = y * lax.rsqrt(jnp.mean(y * y, axis=-1, keepdims=True) + EPS) * npost_ref[...]
        out_ref[0] = xb_ref[0] + gate * yn

    use_direct = flag_ref[slot_b] == 1

    @pl.when(use_direct)
    def _():
        front()
        for c in range(n_chunks):
            direct_chunk(c)
        back_epilogue()

    @pl.when(jnp.logical_not(use_direct))
    def _():
        front()
        lax.fori_loop(0, n_chunks, robust_chunk, 0)
        back_epilogue()


def _layer_call(layer, x, mod, npre, w_in, pool_w, pool_scale, lb, hnw, w_out, npost, wsum, lvl):
    batch, seq, d = x.shape
    ts = SEQ_TILE
    tiles_per_seq = seq // ts
    n_tiles = batch * tiles_per_seq

    def front_tile(g):
        t = jnp.minimum(g, n_tiles - 1)
        return t // tiles_per_seq, t % tiles_per_seq

    def back_tile(g):
        t = jnp.maximum(g - 1, 0)
        return t // tiles_per_seq, t % tiles_per_seq

    const2 = lambda g: (0, 0)
    of_layer3 = lambda g: (layer, 0, 0)
    of_layer4 = lambda g: (layer, 0, 0, 0)
    once = pl.Buffered(1)
    return pl.pallas_call(
        functools.partial(_layer_kernel, tiles_per_seq, n_tiles),
        grid=(n_tiles + 1,),
        in_specs=[
            pl.BlockSpec((1, ts, d), lambda g: (*front_tile(g), 0)),
            pl.BlockSpec((1, ts, d), lambda g: (*back_tile(g), 0)),
            pl.BlockSpec((None, 1, 1, 3 * d), lambda g: (layer, front_tile(g)[0], 0, 0)),
            pl.BlockSpec((None, 1, 1, 3 * d), lambda g: (layer, back_tile(g)[0], 0, 0)),
            pl.BlockSpec((None, 1, d), of_layer3),
            pl.BlockSpec((None, d // 2, IN_WIDTH), of_layer3, pipeline_mode=once),
            pl.BlockSpec((None,) + pool_w.shape[1:], of_layer4, pipeline_mode=once),
            pl.BlockSpec((None, 1, POOL_WIDTH), of_layer3),
            pl.BlockSpec((None, 1, HGRN_WIDTH), of_layer3),
            pl.BlockSpec((None, 1, HEAD_DIM), of_layer3),
            pl.BlockSpec((None, MIX_WIDTH // 2, d), of_layer3, pipeline_mode=once),
            pl.BlockSpec((None, 1, d), of_layer3),
            pl.BlockSpec(wsum.shape, const2),
            pl.BlockSpec(lvl.shape, const2),
        ],
        out_specs=pl.BlockSpec((1, ts, d), lambda g: (*back_tile(g), 0)),
        out_shape=jax.ShapeDtypeStruct(x.shape, F32),
        scratch_shapes=[
            pltpu.VMEM((HGRN_HEADS, HEAD_DIM, HEAD_DIM), F32),
            pltpu.VMEM((POOL_TAIL, POOL_WIDTH), F32),
            pltpu.VMEM((SLOTS, ts, HGRN_WIDTH), F32),
            pltpu.VMEM((SLOTS, ts, HGRN_WIDTH), F32),
            pltpu.VMEM((SLOTS, ts, HGRN_WIDTH), F32),
            pltpu.VMEM((SLOTS, ts, HGRN_WIDTH), F32),
            pltpu.VMEM((SLOTS, ts, HGRN_WIDTH), F32),
            pltpu.VMEM((SLOTS, ts, POOL_WIDTH), BF16),
            pltpu.SMEM((SLOTS,), jnp.int32),
            pltpu.VMEM((ts, HGRN_WIDTH), BF16),
            pltpu.VMEM((HGRN_HEADS, CHUNK, CHUNK), BF16),
            pltpu.VMEM((CHUNK, HGRN_WIDTH), F32),
            pltpu.VMEM(((NUM_LEVELS + 1) * CHUNK, HGRN_WIDTH), BF16),
            pltpu.VMEM((NUM_LEVELS + 1, CHUNK, HGRN_WIDTH), BF16),
        ],
        compiler_params=pltpu.CompilerParams(
            dimension_semantics=("arbitrary",),
            vmem_limit_bytes=VMEM_LIMIT_BYTES),
        name="hybrid_layer",
    )(x, x, mod, mod, npre, w_in, pool_w, pool_scale, lb, hnw, w_out, npost, wsum, lvl)


def kernel(x, c, norm_pre_w, ada_w, ada_b, w_in, pool_w, pool_scale, hgrn_lower_bounds,
           hgrn_norm_w, w_out, norm_post_w):
    depth = ada_w.shape[0]
    batch = x.shape[0]
    lower = _lower_bounds_call(hgrn_lower_bounds)
    mod = _adaln_call(c, ada_w, ada_b)
    wsum = jnp.asarray(_cumsum_matrix(), dtype=BF16)
    lvl = jnp.asarray(_level_map())
    stacked = (
        mod.reshape(depth, batch, 1, 3 * D_MODEL),
        norm_pre_w.astype(F32).reshape(depth, 1, D_MODEL),
        _pack_rows(w_in),
        pool_w.astype(BF16),
        pool_scale.astype(F32).reshape(depth, 1, POOL_WIDTH),
        lower.reshape(depth, 1, HGRN_WIDTH),
        hgrn_norm_w.astype(F32).reshape(depth, 1, HEAD_DIM),
        _pack_rows(w_out),
        norm_post_w.astype(F32).reshape(depth, 1, D_MODEL),
    )
    h = x.astype(F32)
    for l in range(depth):
        h = _layer_call(l, h, *stacked, wsum, lvl)
    return h.astype(x.dtype)
```

```python
import functools

import numpy as np
import jax
import jax.numpy as jnp
from jax import lax
from jax.experimental import pallas as pl
from jax.experimental.pallas import tpu as pltpu

D_MODEL = 1024
CHUNK = 64
POOL_WIDTH = 1024
POOL_WINDOWS = (2, 4, 8, 16)
POOL_GROUP_WIDTH = POOL_WIDTH // len(POOL_WINDOWS)
HGRN_WIDTH = 1024
HEAD_DIM = 128
HGRN_HEADS = HGRN_WIDTH // HEAD_DIM
MIX_WIDTH = POOL_WIDTH + HGRN_WIDTH
IN_WIDTH = 2 * POOL_WIDTH + 4 * HGRN_WIDTH
EPS = 1e-6

SEQ_TILE = 512
POOL_TAIL = 16
assert POOL_WINDOWS == tuple(2 << g for g in range(len(POOL_WINDOWS))) and POOL_TAIL >= POOL_WINDOWS[-1]
NUM_LEVELS = 6
SPLIT = 2
VMEM_LIMIT_BYTES = 62 * 1024 * 1024
SUBLANES = 8
LOG2E = 1.4426950408889634
MID_ROW = CHUNK // 2 - 1
DIRECT_MAX_LOG2 = 110.0
SLOTS = 2

F32 = jnp.float32
BF16 = jnp.bfloat16


def _ref_row(level, row):
    m = 1 << (level - 1)
    return (row // (2 * m)) * (2 * m) + m - 1


def _cumsum_matrix():
    t = np.arange(CHUNK)[:, None]
    j = np.arange(CHUNK)[None, :]
    w = (j <= t).astype(np.float32)
    return np.concatenate([w] * SPLIT, axis=1)


def _level_map():
    t = np.arange(CHUNK)[:, None]
    s = np.arange(CHUNK)[None, :]
    x = t ^ s
    lvl = np.where(x > 0, np.floor(np.log2(np.maximum(x, 1))).astype(np.int32) + 1, 0)
    return np.where(s > t, -1, lvl).astype(np.int32)


def _split_bf16(a, pieces):
    out = []
    rest = a
    for _ in range(pieces):
        p = rest.astype(BF16)
        out.append(p)
        rest = rest - p.astype(F32)
    return out


def _sigmoid(a):
    return 1.0 / (1.0 + jnp.exp(-a))


def _silu(a):
    return a * (0.5 * jnp.tanh(0.5 * a) + 0.5)


PACK_ROWS, PACK_COLS = 512, 1024


def _pack_rows_kernel(w_ref, out_ref):
    out_ref[0] = pltpu.bitcast(w_ref[0].astype(BF16), jnp.uint32)


def _pack_rows(w):
    layers, k, n = w.shape
    rows, cols = min(PACK_ROWS, k), min(PACK_COLS, n)
    return pl.pallas_call(
        _pack_rows_kernel,
        grid=(layers, k // rows, n // cols),
        in_specs=[pl.BlockSpec((1, rows, cols), lambda l, i, j: (l, i, j))],
        out_specs=pl.BlockSpec((1, rows // 2, cols), lambda l, i, j: (l, i, j)),
        out_shape=jax.ShapeDtypeStruct((layers, k // 2, n), jnp.uint32),
        name="pack_rows",
    )(w.astype(F32))


def _unpack_rows(words):
    return pltpu.bitcast(words, BF16)


def _dot(a, b):
    return jnp.dot(a, b, preferred_element_type=F32)


def _dot_nt(a, b):
    return lax.dot_general(a, b, (((1,), (1,)), ((), ())), preferred_element_type=F32)


def _lower_bounds_kernel(lb_ref, out_ref):
    depth = lb_ref.shape[0]
    rows = [lb_ref[l:l + 1, :] for l in range(depth)]
    mx = rows[0]
    for r in rows[1:]:
        mx = jnp.maximum(mx, r)
    ex = [jnp.exp(r - mx) for r in rows]
    den = ex[0]
    for e in ex[1:]:
        den = den + e
    sm = [e / den for e in ex]
    run = sm[0]
    out_ref[0:1, :] = run - sm[0]
    for l in range(1, depth):
        run = run + sm[l]
        out_ref[l:l + 1, :] = run - sm[0]


def _lower_bounds_call(hgrn_lower_bounds):
    return pl.pallas_call(
        _lower_bounds_kernel,
        out_shape=jax.ShapeDtypeStruct(hgrn_lower_bounds.shape, F32),
        name="lower_bounds",
    )(hgrn_lower_bounds.astype(F32))


def _adaln_kernel(c_ref, w_ref, b_ref, mod_ref):
    c = c_ref[...]
    sc = c * _sigmoid(c)
    w = w_ref[0]
    a_hi, a_lo = _split_bf16(sc, 2)
    w_hi, w_lo = _split_bf16(w, 2)
    acc = _dot(a_hi, w_hi) + _dot(a_hi, w_lo) + _dot(a_lo, w_hi)
    mod_ref[0] = acc + b_ref[0]


def _adaln_call(c, ada_w, ada_b):
    depth, d, d3 = ada_w.shape
    batch = c.shape[0]
    n_col = d3 // d
    return pl.pallas_call(
        _adaln_kernel,
        grid=(depth, n_col),
        in_specs=[
            pl.BlockSpec((batch, d), lambda l, j: (0, 0)),
            pl.BlockSpec((1, d, d), lambda l, j: (l, 0, j)),
            pl.BlockSpec((1, 1, d), lambda l, j: (l, 0, j)),
        ],
        out_specs=pl.BlockSpec((1, batch, d), lambda l, j: (l, 0, j)),
        out_shape=jax.ShapeDtypeStruct((depth, batch, d3), F32),
        name="adaln",
    )(c.astype(F32), ada_w.astype(F32), ada_b.astype(F32).reshape(depth, 1, d3))


def _layer_kernel(tiles_per_seq, n_tiles,
                  xf_ref, xb_ref, modf_ref, modb_ref, npre_ref, w_in_ref, pool_w_ref,
                  pool_scale_ref, lb_ref, hnw_ref, w_out_ref, npost_ref, wsum_ref, lvl_ref,
                  out_ref,
                  state_ref, tail_ref, q_ref, k_ref, v_ref, g_ref, bsum_ref, mixp_ref, flag_ref,
                  mixh_ref, attn_ref, b_ref, ql_ref, kl_ref):
    ts = SEQ_TILE
    n_chunks = ts // CHUNK
    step = pl.program_id(0)
    front_tile = jnp.minimum(step, n_tiles - 1)
    back_tile = jnp.maximum(step - 1, 0)
    slot_f = step % SLOTS
    slot_b = (step + 1) % SLOTS

    @pl.when(step == 0)
    def _():
        for ref in (q_ref, k_ref, v_ref, g_ref, bsum_ref, mixp_ref):
            ref[...] = jnp.zeros_like(ref)
        flag_ref[1] = 1

    @pl.when(front_tile % tiles_per_seq == 0)
    def _():
        tail_ref[...] = jnp.zeros_like(tail_ref)

    @pl.when(back_tile % tiles_per_seq == 0)
    def _():
        state_ref[...] = jnp.zeros_like(state_ref)

    lvl = lvl_ref[...]
    causal = lvl >= 0
    hnw = hnw_ref[...]
    heads = [slice(hd * HEAD_DIM, (hd + 1) * HEAD_DIM) for hd in range(HGRN_HEADS)]

    def front():
        x = xf_ref[0]
        mod = modf_ref[0]
        shift = mod[:, 0:D_MODEL]
        scale = mod[:, D_MODEL:2 * D_MODEL]
        ms = jnp.mean(x * x, axis=-1, keepdims=True)
        xn = x * lax.rsqrt(ms + EPS) * npre_ref[...]
        h = (xn * (1.0 + scale) + shift).astype(BF16)

        def in_proj(col):
            return _dot(h, _unpack_rows(w_in_ref[:, col * 1024:(col + 1) * 1024]))

        u = in_proj(0)
        part = jnp.concatenate([tail_ref[...], u], axis=0)
        tail_ref[...] = u[ts - POOL_TAIL:ts, :]
        win_sum = []
        for g, w in enumerate(POOL_WINDOWS):
            part = part[:, (POOL_GROUP_WIDTH if g else 0):]
            part = part + pltpu.roll(part, w // 2, 0)
            win_sum.append(part[POOL_TAIL:, 0:POOL_GROUP_WIDTH])
        g_pool = in_proj(1)
        pos = (front_tile % tiles_per_seq) * ts + lax.broadcasted_iota(jnp.int32, (ts, 1), 0)
        for g, w in enumerate(POOL_WINDOWS):
            lo, hi = g * POOL_GROUP_WIDTH, (g + 1) * POOL_GROUP_WIDTH
            u_g = u[:, lo:hi]
            inv_cnt = 1.0 / jnp.minimum(pos + 1, w).astype(F32)
            d = (win_sum[g] * inv_cnt - u_g).astype(BF16)
            y = _dot(d, pool_w_ref[g]) * pool_scale_ref[:, lo:hi]
            gp = g_pool[:, lo:hi]
            mixp_ref[slot_f, :, lo:hi] = (y * _silu(gp)).astype(BF16)

        q_ref[slot_f] = in_proj(2)
        lb = lb_ref[...]
        f = lb + (1.0 - lb) * _sigmoid(in_proj(3))
        k_ref[slot_f] = 1.0 - f
        lf = jnp.log(f) * LOG2E
        v_ref[slot_f] = in_proj(4)
        g_ref[slot_f] = in_proj(5)
        unsafe = None
        for c in range(n_chunks):
            lf3 = jnp.concatenate(_split_bf16(lf[c * CHUNK:(c + 1) * CHUNK, :], SPLIT), axis=0)
            b_c = _dot(wsum_ref[...], lf3)
            bsum_ref[slot_f, c * CHUNK:(c + 1) * CHUNK, :] = b_c
            b_mid = b_c[MID_ROW:MID_ROW + 1, :]
            b_last = b_c[CHUNK - 1:CHUNK, :]
            ok = jnp.logical_and(-b_mid <= DIRECT_MAX_LOG2, b_mid - b_last <= DIRECT_MAX_LOG2)
            bad = jnp.where(ok, 0.0, 1.0)
            unsafe = bad if unsafe is None else jnp.maximum(unsafe, bad)
        flag_ref[slot_f] = (jnp.max(unsafe) < 0.5).astype(jnp.int32)

    def chunk_outputs(rows, qe, k_dec, e_last, attn):
        v_c = v_ref[slot_b, rows, :]
        k_dec_b = k_dec.astype(BF16)
        for hd, hl in enumerate(heads):
            v_t = jnp.transpose(v_c[:, hl]).astype(BF16)
            state_t = state_ref[hd]
            o = _dot_nt(jnp.concatenate([qe[:, hl], attn[hd]], axis=1),
                        jnp.concatenate([state_t.astype(BF16), v_t], axis=1))
            on = o * lax.rsqrt(jnp.mean(o * o, axis=-1, keepdims=True) + EPS) * hnw
            gh = g_ref[slot_b, rows, hl]
            mixh_ref[rows, hl] = (on * _silu(gh)).astype(BF16)
            state_ref[hd] = state_t * e_last[:, hl] + _dot(v_t, k_dec_b[:, hl])

    def direct_chunk(c):
        rows = slice(c * CHUNK, (c + 1) * CHUNK)
        b_c = bsum_ref[slot_b, rows, :]
        b_mid = bsum_ref[slot_b, c * CHUNK + MID_ROW:c * CHUNK + MID_ROW + 1, :]
        b_last = bsum_ref[slot_b, (c + 1) * CHUNK - 1:(c + 1) * CHUNK, :]
        d = b_c - b_mid
        q_s = q_ref[slot_b, rows, :] * jnp.exp2(d)
        k_s = k_ref[slot_b, rows, :] * jnp.exp2(-d)
        q_b = q_s.astype(BF16)
        k_b = k_s.astype(BF16)
        attn = [jnp.where(causal, _dot_nt(q_b[:, hl], k_b[:, hl]), 0.0).astype(BF16) for hl in heads]
        chunk_outputs(rows, (q_s * jnp.exp2(b_mid)).astype(BF16),
                      k_s * jnp.exp2(b_last - b_mid), jnp.exp2(b_last), attn)

    def robust_chunk(c, carry):
        rows = pl.ds(pl.multiple_of(c * CHUNK, CHUNK), CHUNK)
        b_ref[...] = bsum_ref[slot_b, rows, :]
        b_c = b_ref[...]
        q_c = q_ref[slot_b, rows, :]
        k_c = k_ref[slot_b, rows, :]
        ql_ref[0:CHUNK, :] = q_c.astype(BF16)
        kl_ref[0] = k_c.astype(BF16)
        ql_ref[CHUNK:2 * CHUNK, :] = (q_c * (1.0 - k_c)).astype(BF16)
        row_in_group = lax.broadcasted_iota(jnp.int32, (SUBLANES, 1), 0)
        for level in range(2, NUM_LEVELS + 1):
            m = 1 << (level - 1)
            refs = []
            for r in range(0, CHUNK, SUBLANES):
                if 2 * m >= SUBLANES:
                    refs.append(jnp.broadcast_to(b_ref[_ref_row(level, r):_ref_row(level, r) + 1, :],
                                                 (SUBLANES, HGRN_WIDTH)))
                else:
                    lo_ref = b_ref[_ref_row(level, r):_ref_row(level, r) + 1, :]
                    hi_ref = b_ref[_ref_row(level, r + 2 * m):_ref_row(level, r + 2 * m) + 1, :]
                    refs.append(jnp.where(row_in_group < 2 * m, lo_ref, hi_ref))
            e_l = jnp.exp2(-jnp.abs(b_c - jnp.concatenate(refs, axis=0)))
            ql_ref[level * CHUNK:(level + 1) * CHUNK, :] = (q_c * e_l).astype(BF16)
            kl_ref[level] = (k_c * e_l).astype(BF16)
        for hd, hl in enumerate(heads):
            a01 = _dot_nt(ql_ref[0:2 * CHUNK, hl], kl_ref[0, :, hl])
            attn = jnp.where(lvl == 0, a01[0:CHUNK], 0.0)
            attn = jnp.where(lvl == 1, a01[CHUNK:2 * CHUNK], attn)
            for level in range(2, NUM_LEVELS + 1):
                a_l = _dot_nt(ql_ref[level * CHUNK:(level + 1) * CHUNK, hl], kl_ref[level, :, hl])
                attn = jnp.where(lvl == level, a_l, attn)
            attn_ref[hd] = attn.astype(BF16)
        b_last = b_ref[CHUNK - 1:CHUNK, :]
        chunk_outputs(rows, (q_c * jnp.exp2(b_c)).astype(BF16),
                      k_c * jnp.exp2(b_last - b_c), jnp.exp2(b_last),
                      [attn_ref[hd] for hd in range(HGRN_HEADS)])
        return carry

    def back_epilogue():
        gate = modb_ref[0][:, 2 * D_MODEL:3 * D_MODEL]
        mix = jnp.concatenate([mixp_ref[slot_b], mixh_ref[...]], axis=1)
        y = _dot(mix, _unpack_rows(w_out_ref[...]))
        yn = y * lax.rsqrt(jnp.mean(y * y, axis=-1, keepdims=True) + EPS) * npost_ref[...]
        out_ref[0] = xb_ref[0] + gate * yn

    use_direct = flag_ref[slot_b] == 1

    @pl.when(use_direct)
    def _():
        front()
        for c in range(n_chunks):
            direct_chunk(c)
        back_epilogue()

    @pl.when(jnp.logical_not(use_direct))
    def _():
        front()
        lax.fori_loop(0, n_chunks, robust_chunk, 0)
        back_epilogue()


def _layer_call(layer, x, mod, npre, w_in, pool_w, pool_scale, lb, hnw, w_out, npost, wsum, lvl):
    batch, seq, d = x.shape
    ts = SEQ_TILE
    tiles_per_seq = seq // ts
    n_tiles = batch * tiles_per_seq

    def front_tile(g):
        t = jnp.minimum(g, n_tiles - 1)
        return t // tiles_per_seq, t % tiles_per_seq

    def back_tile(g):
        t = jnp.maximum(g - 1, 0)
        return t // tiles_per_seq, t % tiles_per_seq

    const2 = lambda g: (0, 0)
    of_layer3 = lambda g: (layer, 0, 0)
    of_layer4 = lambda g: (layer, 0, 0, 0)
    once = pl.Buffered(1)
    return pl.pallas_call(
        functools.partial(_layer_kernel, tiles_per_seq, n_tiles),
        grid=(n_tiles + 1,),
        in_specs=[
            pl.BlockSpec((1, ts, d), lambda g: (*front_tile(g), 0)),
            pl.BlockSpec((1, ts, d), lambda g: (*back_tile(g), 0)),
            pl.BlockSpec((None, 1, 1, 3 * d), lambda g: (layer, front_tile(g)[0], 0, 0)),
            pl.BlockSpec((None, 1, 1, 3 * d), lambda g: (layer, back_tile(g)[0], 0, 0)),
            pl.BlockSpec((None, 1, d), of_layer3),
            pl.BlockSpec((None, d // 2, IN_WIDTH), of_layer3, pipeline_mode=once),
            pl.BlockSpec((None,) + pool_w.shape[1:], of_layer4, pipeline_mode=once),
            pl.BlockSpec((None, 1, POOL_WIDTH), of_layer3),
            pl.BlockSpec((None, 1, HGRN_WIDTH), of_layer3),
            pl.BlockSpec((None, 1, HEAD_DIM), of_layer3),
            pl.BlockSpec((None, MIX_WIDTH // 2, d), of_layer3, pipeline_mode=once),
            pl.BlockSpec((None, 1, d), of_layer3),
            pl.BlockSpec(wsum.shape, const2),
            pl.BlockSpec(lvl.shape, const2),
        ],
        out_specs=pl.BlockSpec((1, ts, d), lambda g: (*back_tile(g), 0)),
        out_shape=jax.ShapeDtypeStruct(x.shape, F32),
        scratch_shapes=[
            pltpu.VMEM((HGRN_HEADS, HEAD_DIM, HEAD_DIM), F32),
            pltpu.VMEM((POOL_TAIL, POOL_WIDTH), F32),
            pltpu.VMEM((SLOTS, ts, HGRN_WIDTH), F32),
            pltpu.VMEM((SLOTS, ts, HGRN_WIDTH), F32),
            pltpu.VMEM((SLOTS, ts, HGRN_WIDTH), F32),
            pltpu.VMEM((SLOTS, ts, HGRN_WIDTH), F32),
            pltpu.VMEM((SLOTS, ts, HGRN_WIDTH), F32),
            pltpu.VMEM((SLOTS, ts, POOL_WIDTH), BF16),
            pltpu.SMEM((SLOTS,), jnp.int32),
            pltpu.VMEM((ts, HGRN_WIDTH), BF16),
            pltpu.VMEM((HGRN_HEADS, CHUNK, CHUNK), BF16),
            pltpu.VMEM((CHUNK, HGRN_WIDTH), F32),
            pltpu.VMEM(((NUM_LEVELS + 1) * CHUNK, HGRN_WIDTH), BF16),
            pltpu.VMEM((NUM_LEVELS + 1, CHUNK, HGRN_WIDTH), BF16),
        ],
        compiler_params=pltpu.CompilerParams(
            dimension_semantics=("arbitrary",),
            vmem_limit_bytes=VMEM_LIMIT_BYTES),
        name="hybrid_layer",
    )(x, x, mod, mod, npre, w_in, pool_w, pool_scale, lb, hnw, w_out, npost, wsum, lvl)


def kernel(x, c, norm_pre_w, ada_w, ada_b, w_in, pool_w, pool_scale, hgrn_lower_bounds,
           hgrn_norm_w, w_out, norm_post_w):
    depth = ada_w.shape[0]
    batch = x.shape[0]
    lower = _lower_bounds_call(hgrn_lower_bounds)
    mod = _adaln_call(c, ada_w, ada_b)
    wsum = jnp.asarray(_cumsum_matrix(), dtype=BF16)
    lvl = jnp.asarray(_level_map())
    stacked = (
        mod.reshape(depth, batch, 1, 3 * D_MODEL),
        norm_pre_w.astype(F32).reshape(depth, 1, D_MODEL),
        _pack_rows(w_in),
        pool_w.astype(BF16),
        pool_scale.astype(F32).reshape(depth, 1, POOL_WIDTH),
        lower.reshape(depth, 1, HGRN_WIDTH),
        hgrn_norm_w.astype(F32).reshape(depth, 1, HEAD_DIM),
        _pack_rows(w_out),
        norm_post_w.astype(F32).reshape(depth, 1, D_MODEL),
    )
    h = x.astype(F32)
    for l in range(depth):
        h = _layer_call(l, h, *stacked, wsum, lvl)
    return h.astype(x.dtype)
```

```python
import functools

import numpy as np
import jax
import jax.numpy as jnp
from jax import lax
from jax.experimental import pallas as pl
from jax.experimental.pallas import tpu as pltpu

D_MODEL = 1024
CHUNK = 64
POOL_WIDTH = 1024
POOL_WINDOWS = (2, 4, 8, 16)
POOL_GROUP_WIDTH = POOL_WIDTH // len(POOL_WINDOWS)
HGRN_WIDTH = 1024
HEAD_DIM = 128
HGRN_HEADS = HGRN_WIDTH // HEAD_DIM
MIX_WIDTH = POOL_WIDTH + HGRN_WIDTH
IN_WIDTH = 2 * POOL_WIDTH + 4 * HGRN_WIDTH
SECTION_WIDTHS = (POOL_WIDTH, POOL_WIDTH, HGRN_WIDTH, HGRN_WIDTH, HGRN_WIDTH, HGRN_WIDTH)
SECTION_STARTS = tuple(int(v) for v in np.cumsum((0,) + SECTION_WIDTHS))
assert SECTION_STARTS[-1] == IN_WIDTH
SEC_POOL_U, SEC_POOL_GATE, SEC_Q, SEC_FORGET, SEC_V, SEC_OUT_GATE = range(len(SECTION_WIDTHS))
EPS = 1e-6

SEQ_TILE = 512
POOL_TAIL = 16
assert POOL_WINDOWS == tuple(2 << g for g in range(len(POOL_WINDOWS))) and POOL_TAIL >= POOL_WINDOWS[-1]
NUM_LEVELS = 6
SPLIT = 2
VMEM_LIMIT_BYTES = 62 * 1024 * 1024
SUBLANES = 8
LOG2E = 1.4426950408889634
MID_ROW = CHUNK // 2 - 1
DIRECT_MAX_LOG2 = 110.0
SLOTS = 2

F32 = jnp.float32
BF16 = jnp.bfloat16


def _ref_row(level, row):
    m = 1 << (level - 1)
    return (row // (2 * m)) * (2 * m) + m - 1


def _cumsum_matrix():
    t = np.arange(CHUNK)[:, None]
    j = np.arange(CHUNK)[None, :]
    w = (j <= t).astype(np.float32)
    return np.concatenate([w] * SPLIT, axis=1)


def _level_map():
    t = np.arange(CHUNK)[:, None]
    s = np.arange(CHUNK)[None, :]
    x = t ^ s
    lvl = np.where(x > 0, np.floor(np.log2(np.maximum(x, 1))).astype(np.int32) + 1, 0)
    return np.where(s > t, -1, lvl).astype(np.int32)


def _split_bf16(a, pieces):
    out = []
    rest = a
    for _ in range(pieces):
        p = rest.astype(BF16)
        out.append(p)
        rest = rest - p.astype(F32)
    return out


def _sigmoid(a):
    return 1.0 / (1.0 + jnp.exp(-a))


def _silu(a):
    return a * (0.5 * jnp.tanh(0.5 * a) + 0.5)


PACK_ROWS, PACK_COLS = 1024, 1024


def _pack_rows_kernel(w_ref, out_ref):
    out_ref[0] = pltpu.bitcast(w_ref[0].astype(BF16), jnp.uint32)


def _pack_rows(w):
    layers, k, n = w.shape
    rows, cols = min(PACK_ROWS, k), min(PACK_COLS, n)
    return pl.pallas_call(
        _pack_rows_kernel,
        grid=(layers, k // rows, n // cols),
        in_specs=[pl.BlockSpec((1, rows, cols), lambda l, i, j: (l, i, j))],
        out_specs=pl.BlockSpec((1, rows // 2, cols), lambda l, i, j: (l, i, j)),
        out_shape=jax.ShapeDtypeStruct((layers, k // 2, n), jnp.uint32),
        name="pack_rows",
    )(w.astype(F32))


def _unpack_rows(words):
    return pltpu.bitcast(words, BF16)


def _dot(a, b):
    return jnp.dot(a, b, preferred_element_type=F32)


def _dot_nt(a, b):
    return lax.dot_general(a, b, (((1,), (1,)), ((), ())), preferred_element_type=F32)


def _lower_bounds_kernel(lb_ref, out_ref):
    depth = lb_ref.shape[0]
    rows = [lb_ref[l:l + 1, :] for l in range(depth)]
    mx = rows[0]
    for r in rows[1:]:
        mx = jnp.maximum(mx, r)
    ex = [jnp.exp(r - mx) for r in rows]
    den = ex[0]
    for e in ex[1:]:
        den = den + e
    sm = [e / den for e in ex]
    run = sm[0]
    out_ref[0:1, :] = run - sm[0]
    for l in range(1, depth):
        run = run + sm[l]
        out_ref[l:l + 1, :] = run - sm[0]


def _lower_bounds_call(hgrn_lower_bounds):
    return pl.pallas_call(
        _lower_bounds_kernel,
        out_shape=jax.ShapeDtypeStruct(hgrn_lower_bounds.shape, F32),
        name="lower_bounds",
    )(hgrn_lower_bounds.astype(F32))


def _adaln_kernel(c_ref, w_ref, b_ref, mod_ref):
    c = c_ref[...]
    sc = c * _sigmoid(c)
    w = w_ref[0]
    a_hi, a_lo = _split_bf16(sc, 2)
    w_hi, w_lo = _split_bf16(w, 2)
    acc = _dot(a_hi, w_hi) + _dot(a_hi, w_lo) + _dot(a_lo, w_hi)
    mod_ref[0] = acc + b_ref[0]


def _adaln_call(c, ada_w, ada_b):
    depth, d, d3 = ada_w.shape
    batch = c.shape[0]
    n_col = d3 // d
    return pl.pallas_call(
        _adaln_kernel,
        grid=(depth, n_col),
        in_specs=[
            pl.BlockSpec((batch, d), lambda l, j: (0, 0)),
            pl.BlockSpec((1, d, d), lambda l, j: (l, 0, j)),
            pl.BlockSpec((1, 1, d), lambda l, j: (l, 0, j)),
        ],
        out_specs=pl.BlockSpec((1, batch, d), lambda l, j: (l, 0, j)),
        out_shape=jax.ShapeDtypeStruct((depth, batch, d3), F32),
        name="adaln",
    )(c.astype(F32), ada_w.astype(F32), ada_b.astype(F32).reshape(depth, 1, d3))


def _layer_kernel(tiles_per_seq, n_tiles,
                  xf_ref, xb_ref, modf_ref, modb_ref, npre_ref, w_in_ref, pool_w_ref,
                  pool_scale_ref, lb_ref, hnw_ref, w_out_ref, npost_ref, wsum_ref, lvl_ref,
                  out_ref,
                  state_ref, tail_ref, q_ref, k_ref, v_ref, g_ref, bsum_ref, mixp_ref, flag_ref,
                  mixh_ref, attn_ref, b_ref, ql_ref, kl_ref):
    ts = SEQ_TILE
    n_chunks = ts // CHUNK
    step = pl.program_id(0)
    front_tile = jnp.minimum(step, n_tiles - 1)
    back_tile = jnp.maximum(step - 1, 0)
    slot_f = step % SLOTS
    slot_b = (step + 1) % SLOTS

    @pl.when(step == 0)
    def _():
        for ref in (q_ref, k_ref, v_ref, g_ref, bsum_ref, mixp_ref):
            ref[...] = jnp.zeros_like(ref)
        flag_ref[1] = 1

    @pl.when(front_tile % tiles_per_seq == 0)
    def _():
        tail_ref[...] = jnp.zeros_like(tail_ref)

    @pl.when(back_tile % tiles_per_seq == 0)
    def _():
        state_ref[...] = jnp.zeros_like(state_ref)

    lvl = lvl_ref[...]
    causal = lvl >= 0
    hnw = hnw_ref[...]
    heads = [slice(hd * HEAD_DIM, (hd + 1) * HEAD_DIM) for hd in range(HGRN_HEADS)]

    def front():
        x = xf_ref[0]
        mod = modf_ref[0]
        shift = mod[:, 0:D_MODEL]
        scale = mod[:, D_MODEL:2 * D_MODEL]
        ms = jnp.mean(x * x, axis=-1, keepdims=True)
        xn = x * lax.rsqrt(ms + EPS) * npre_ref[...]
        h = (xn * (1.0 + scale) + shift).astype(BF16)

        def in_proj(section):
            return _dot(h, _unpack_rows(w_in_ref[:, SECTION_STARTS[section]:SECTION_STARTS[section + 1]]))

        u = in_proj(SEC_POOL_U)
        part = jnp.concatenate([tail_ref[...], u], axis=0)
        tail_ref[...] = u[ts - POOL_TAIL:ts, :]
        win_sum = []
        for g, w in enumerate(POOL_WINDOWS):
            part = part[:, (POOL_GROUP_WIDTH if g else 0):]
            part = part + pltpu.roll(part, w // 2, 0)
            win_sum.append(part[POOL_TAIL:, 0:POOL_GROUP_WIDTH])
        g_pool = in_proj(SEC_POOL_GATE)
        pos = (front_tile % tiles_per_seq) * ts + lax.broadcasted_iota(jnp.int32, (ts, 1), 0)
        for g, w in enumerate(POOL_WINDOWS):
            lo, hi = g * POOL_GROUP_WIDTH, (g + 1) * POOL_GROUP_WIDTH
            u_g = u[:, lo:hi]
            inv_cnt = 1.0 / jnp.minimum(pos + 1, w).astype(F32)
            d = (win_sum[g] * inv_cnt - u_g).astype(BF16)
            y = _dot(d, pool_w_ref[g]) * pool_scale_ref[:, lo:hi]
            gp = g_pool[:, lo:hi]
            mixp_ref[slot_f, :, lo:hi] = (y * _silu(gp)).astype(BF16)

        q_ref[slot_f] = in_proj(SEC_Q)
        lb = lb_ref[...]
        f = lb + (1.0 - lb) * _sigmoid(in_proj(SEC_FORGET))
        k_ref[slot_f] = 1.0 - f
        lf = jnp.log(f) * LOG2E
        v_ref[slot_f] = in_proj(SEC_V)
        g_ref[slot_f] = in_proj(SEC_OUT_GATE)
        unsafe = None
        for c in range(n_chunks):
            lf3 = jnp.concatenate(_split_bf16(lf[c * CHUNK:(c + 1) * CHUNK, :], SPLIT), axis=0)
            b_c = _dot(wsum_ref[...], lf3)
            bsum_ref[slot_f, c * CHUNK:(c + 1) * CHUNK, :] = b_c
            b_mid = b_c[MID_ROW:MID_ROW + 1, :]
            b_last = b_c[CHUNK - 1:CHUNK, :]
            ok = jnp.logical_and(-b_mid <= DIRECT_MAX_LOG2, b_mid - b_last <= DIRECT_MAX_LOG2)
            bad = jnp.where(ok, 0.0, 1.0)
            unsafe = bad if unsafe is None else jnp.maximum(unsafe, bad)
        flag_ref[slot_f] = (jnp.max(unsafe) < 0.5).astype(jnp.int32)

    def chunk_outputs(rows, qe, k_dec, e_last, attn):
        v_c = v_ref[slot_b, rows, :]
        k_dec_b = k_dec.astype(BF16)
        for hd, hl in enumerate(heads):
            v_t = jnp.transpose(v_c[:, hl]).astype(BF16)
            state_t = state_ref[hd]
            o = _dot_nt(jnp.concatenate([qe[:, hl], attn[hd]], axis=1),
                        jnp.concatenate([state_t.astype(BF16), v_t], axis=1))
            on = o * lax.rsqrt(jnp.mean(o * o, axis=-1, keepdims=True) + EPS) * hnw
            gh = g_ref[slot_b, rows, hl]
            mixh_ref[rows, hl] = (on * _silu(gh)).astype(BF16)
            state_ref[hd] = state_t * e_last[:, hl] + _dot(v_t, k_dec_b[:, hl])

    def direct_chunk(c):
        rows = slice(c * CHUNK, (c + 1) * CHUNK)
        b_c = bsum_ref[slot_b, rows, :]
        b_mid = bsum_ref[slot_b, c * CHUNK + MID_ROW:c * CHUNK + MID_ROW + 1, :]
        b_last = bsum_ref[slot_b, (c + 1) * CHUNK - 1:(c + 1) * CHUNK, :]
        d = b_c - b_mid
        q_s = q_ref[slot_b, rows, :] * jnp.exp2(d)
        k_s = k_ref[slot_b, rows, :] * jnp.exp2(-d)
        q_b = q_s.astype(BF16)
        k_b = k_s.astype(BF16)
        attn = [jnp.where(causal, _dot_nt(q_b[:, hl], k_b[:, hl]), 0.0).astype(BF16) for hl in heads]
        chunk_outputs(rows, (q_s * jnp.exp2(b_mid)).astype(BF16),
                      k_s * jnp.exp2(b_last - b_mid), jnp.exp2(b_last), attn)

    def robust_chunk(c, carry):
        rows = pl.ds(pl.multiple_of(c * CHUNK, CHUNK), CHUNK)
        b_ref[...] = bsum_ref[slot_b, rows, :]
        b_c = b_ref[...]
        q_c = q_ref[slot_b, rows, :]
        k_c = k_ref[slot_b, rows, :]
        ql_ref[0:CHUNK, :] = q_c.astype(BF16)
        kl_ref[0] = k_c.astype(BF16)
        ql_ref[CHUNK:2 * CHUNK, :] = (q_c * (1.0 - k_c)).astype(BF16)
        row_in_group = lax.broadcasted_iota(jnp.int32, (SUBLANES, 1), 0)
        for level in range(2, NUM_LEVELS + 1):
            m = 1 << (level - 1)
            refs = []
            for r in range(0, CHUNK, SUBLANES):
                if 2 * m >= SUBLANES:
                    refs.append(jnp.broadcast_to(b_ref[_ref_row(level, r):_ref_row(level, r) + 1, :],
                                                 (SUBLANES, HGRN_WIDTH)))
                else:
                    lo_ref = b_ref[_ref_row(level, r):_ref_row(level, r) + 1, :]
                    hi_ref = b_ref[_ref_row(level, r + 2 * m):_ref_row(level, r + 2 * m) + 1, :]
                    refs.append(jnp.where(row_in_group < 2 * m, lo_ref, hi_ref))
            e_l = jnp.exp2(-jnp.abs(b_c - jnp.concatenate(refs, axis=0)))
            ql_ref[level * CHUNK:(level + 1) * CHUNK, :] = (q_c * e_l).astype(BF16)
            kl_ref[level] = (k_c * e_l).astype(BF16)
        for hd, hl in enumerate(heads):
            a01 = _dot_nt(ql_ref[0:2 * CHUNK, hl], kl_ref[0, :, hl])
            attn = jnp.where(lvl == 0, a01[0:CHUNK], 0.0)
            attn = jnp.where(lvl == 1, a01[CHUNK:2 * CHUNK], attn)
            for level in range(2, NUM_LEVELS + 1):
                a_l = _dot_nt(ql_ref[level * CHUNK:(level + 1) * CHUNK, hl], kl_ref[level, :, hl])
                attn = jnp.where(lvl == level, a_l, attn)
            attn_ref[hd] = attn.astype(BF16)
        b_last = b_ref[CHUNK - 1:CHUNK, :]
        chunk_outputs(rows, (q_c * jnp.exp2(b_c)).astype(BF16),
                      k_c * jnp.exp2(b_last - b_c), jnp.exp2(b_last),
                      [attn_ref[hd] for hd in range(HGRN_HEADS)])
        return carry

    def back_epilogue():
        gate = modb_ref[0][:, 2 * D_MODEL:3 * D_MODEL]
        mix = jnp.concatenate([mixp_ref[slot_b], mixh_ref[...]], axis=1)
        y = _dot(mix, _unpack_rows(w_out_ref[...]))
        yn = y * lax.rsqrt(jnp.mean(y * y, axis=-1, keepdims=True) + EPS) * npost_ref[...]
        out_ref[0] = xb_ref[0] + gate * yn

    use_direct = flag_ref[slot_b] == 1

    @pl.when(use_direct)
    def _():
        front()
        for c in range(n_chunks):
            direct_chunk(c)
        back_epilogue()

    @pl.when(jnp.logical_not(use_direct))
    def _():
        front()
        lax.fori_loop(0, n_chunks, robust_chunk, 0)
        back_epilogue()


def _layer_call(layer, x, mod, npre, w_in, pool_w, pool_scale, lb, hnw, w_out, npost, wsum, lvl):
    batch, seq, d = x.shape
    ts = SEQ_TILE
    tiles_per_seq = seq // ts
    n_tiles = batch * tiles_per_seq

    def front_tile(g):
        t = jnp.minimum(g, n_tiles - 1)
        return t // tiles_per_seq, t % tiles_per_seq

    def back_tile(g):
        t = jnp.maximum(g - 1, 0)
        return t // tiles_per_seq, t % tiles_per_seq

    const2 = lambda g: (0, 0)
    of_layer3 = lambda g: (layer, 0, 0)
    of_layer4 = lambda g: (layer, 0, 0, 0)
    once = pl.Buffered(1)
    return pl.pallas_call(
        functools.partial(_layer_kernel, tiles_per_seq, n_tiles),
        grid=(n_tiles + 1,),
        in_specs=[
            pl.BlockSpec((1, ts, d), lambda g: (*front_tile(g), 0)),
            pl.BlockSpec((1, ts, d), lambda g: (*back_tile(g), 0)),
            pl.BlockSpec((None, 1, 1, 3 * d), lambda g: (layer, front_tile(g)[0], 0, 0)),
            pl.BlockSpec((None, 1, 1, 3 * d), lambda g: (layer, back_tile(g)[0], 0, 0)),
            pl.BlockSpec((None, 1, d), of_layer3),
            pl.BlockSpec((None, d // 2, IN_WIDTH), of_layer3, pipeline_mode=once),
            pl.BlockSpec((None,) + pool_w.shape[1:], of_layer4, pipeline_mode=once),
            pl.BlockSpec((None, 1, POOL_WIDTH), of_layer3),
            pl.BlockSpec((None, 1, HGRN_WIDTH), of_layer3),
            pl.BlockSpec((None, 1, HEAD_DIM), of_layer3),
            pl.BlockSpec((None, MIX_WIDTH // 2, d), of_layer3, pipeline_mode=once),
            pl.BlockSpec((None, 1, d), of_layer3),
            pl.BlockSpec(wsum.shape, const2),
            pl.BlockSpec(lvl.shape, const2),
        ],
        out_specs=pl.BlockSpec((1, ts, d), lambda g: (*back_tile(g), 0)),
        out_shape=jax.ShapeDtypeStruct(x.shape, F32),
        scratch_shapes=[
            pltpu.VMEM((HGRN_HEADS, HEAD_DIM, HEAD_DIM), F32),
            pltpu.VMEM((POOL_TAIL, POOL_WIDTH), F32),
            pltpu.VMEM((SLOTS, ts, HGRN_WIDTH), F32),
            pltpu.VMEM((SLOTS, ts, HGRN_WIDTH), F32),
            pltpu.VMEM((SLOTS, ts, HGRN_WIDTH), F32),
            pltpu.VMEM((SLOTS, ts, HGRN_WIDTH), F32),
            pltpu.VMEM((SLOTS, ts, HGRN_WIDTH), F32),
            pltpu.VMEM((SLOTS, ts, POOL_WIDTH), BF16),
            pltpu.SMEM((SLOTS,), jnp.int32),
            pltpu.VMEM((ts, HGRN_WIDTH), BF16),
            pltpu.VMEM((HGRN_HEADS, CHUNK, CHUNK), BF16),
            pltpu.VMEM((CHUNK, HGRN_WIDTH), F32),
            pltpu.VMEM(((NUM_LEVELS + 1) * CHUNK, HGRN_WIDTH), BF16),
            pltpu.VMEM((NUM_LEVELS + 1, CHUNK, HGRN_WIDTH), BF16),
        ],
        compiler_params=pltpu.CompilerParams(
            dimension_semantics=("arbitrary",),
            vmem_limit_bytes=VMEM_LIMIT_BYTES),
        name="hybrid_layer",
    )(x, x, mod, mod, npre, w_in, pool_w, pool_scale, lb, hnw, w_out, npost, wsum, lvl)


def kernel(x, c, norm_pre_w, ada_w, ada_b, w_in, pool_w, pool_scale, hgrn_lower_bounds,
           hgrn_norm_w, w_out, norm_post_w):
    depth = ada_w.shape[0]
    batch = x.shape[0]
    lower = _lower_bounds_call(hgrn_lower_bounds)
    mod = _adaln_call(c, ada_w, ada_b)
    wsum = jnp.asarray(_cumsum_matrix(), dtype=BF16)
    lvl = jnp.asarray(_level_map())
    stacked = (
        mod.reshape(depth, batch, 1, 3 * D_MODEL),
        norm_pre_w.astype(F32).reshape(depth, 1, D_MODEL),
        _pack_rows(w_in),
        pool_w.astype(BF16),
        pool_scale.astype(F32).reshape(depth, 1, POOL_WIDTH),
        lower.reshape(depth, 1, HGRN_WIDTH),
        hgrn_norm_w.astype(F32).reshape(depth, 1, HEAD_DIM),
        _pack_rows(w_out),
        norm_post_w.astype(F32).reshape(depth, 1, D_MODEL),
    )
    h = x.astype(F32)
    for l in range(depth):
        h = _layer_call(l, h, *stacked, wsum, lvl)
    return h.astype(x.dtype)
```

```python
import functools

import numpy as np
import jax
import jax.numpy as jnp
from jax import lax
from jax.experimental import pallas as pl
from jax.experimental.pallas import tpu as pltpu

D_MODEL = 1024
CHUNK = 64
POOL_WIDTH = 1024
POOL_WINDOWS = (2, 4, 8, 16)
POOL_GROUP_WIDTH = POOL_WIDTH // len(POOL_WINDOWS)
HGRN_WIDTH = 1024
HEAD_DIM = 128
HGRN_HEADS = HGRN_WIDTH // HEAD_DIM
MIX_WIDTH = POOL_WIDTH + HGRN_WIDTH
IN_WIDTH = 2 * POOL_WIDTH + 4 * HGRN_WIDTH
SECTION_WIDTHS = (POOL_WIDTH, POOL_WIDTH, HGRN_WIDTH, HGRN_WIDTH, HGRN_WIDTH, HGRN_WIDTH)
SECTION_STARTS = tuple(int(v) for v in np.cumsum((0,) + SECTION_WIDTHS))
assert SECTION_STARTS[-1] == IN_WIDTH
SEC_POOL_U, SEC_POOL_GATE, SEC_Q, SEC_FORGET, SEC_V, SEC_OUT_GATE = range(len(SECTION_WIDTHS))
EPS = 1e-6

SEQ_TILE = 512
POOL_TAIL = 16
assert POOL_WINDOWS == tuple(2 << g for g in range(len(POOL_WINDOWS))) and POOL_TAIL >= POOL_WINDOWS[-1]
NUM_LEVELS = 6
SPLIT = 2
VMEM_LIMIT_BYTES = 62 * 1024 * 1024
SUBLANES = 8
LOG2E = 1.4426950408889634
MID_ROW = CHUNK // 2 - 1
DIRECT_MAX_LOG2 = 110.0
SLOTS = 2

F32 = jnp.float32
BF16 = jnp.bfloat16


def _ref_row(level, row):
    m = 1 << (level - 1)
    return (row // (2 * m)) * (2 * m) + m - 1


def _cumsum_matrix():
    t = np.arange(CHUNK)[:, None]
    j = np.arange(CHUNK)[None, :]
    w = (j <= t).astype(np.float32)
    return np.concatenate([w] * SPLIT, axis=1)


def _level_map():
    t = np.arange(CHUNK)[:, None]
    s = np.arange(CHUNK)[None, :]
    x = t ^ s
    lvl = np.where(x > 0, np.floor(np.log2(np.maximum(x, 1))).astype(np.int32) + 1, 0)
    return np.where(s > t, -1, lvl).astype(np.int32)


def _split_bf16(a, pieces):
    out = []
    rest = a
    for _ in range(pieces):
        p = rest.astype(BF16)
        out.append(p)
        rest = rest - p.astype(F32)
    return out


def _sigmoid(a):
    return 1.0 / (1.0 + jnp.exp(-a))


def _twice_silu(a):
    return a * (jnp.tanh(0.5 * a) + 1.0)


PACK_ROWS, PACK_COLS = 1024, 1024


def _pack_rows_kernel(w_ref, out_ref):
    out_ref[0] = pltpu.bitcast(w_ref[0].astype(BF16), jnp.uint32)


def _pack_rows(w):
    layers, k, n = w.shape
    rows, cols = min(PACK_ROWS, k), min(PACK_COLS, n)
    return pl.pallas_call(
        _pack_rows_kernel,
        grid=(layers, k // rows, n // cols),
        in_specs=[pl.BlockSpec((1, rows, cols), lambda l, i, j: (l, i, j))],
        out_specs=pl.BlockSpec((1, rows // 2, cols), lambda l, i, j: (l, i, j)),
        out_shape=jax.ShapeDtypeStruct((layers, k // 2, n), jnp.uint32),
        name="pack_rows",
    )(w.astype(F32))


def _unpack_rows(words):
    return pltpu.bitcast(words, BF16)


def _dot(a, b):
    return jnp.dot(a, b, preferred_element_type=F32)


def _dot_nt(a, b):
    return lax.dot_general(a, b, (((1,), (1,)), ((), ())), preferred_element_type=F32)


def _lower_bounds_kernel(lb_ref, out_ref):
    depth = lb_ref.shape[0]
    rows = [lb_ref[l:l + 1, :] for l in range(depth)]
    mx = rows[0]
    for r in rows[1:]:
        mx = jnp.maximum(mx, r)
    ex = [jnp.exp(r - mx) for r in rows]
    den = ex[0]
    for e in ex[1:]:
        den = den + e
    sm = [e / den for e in ex]
    run = sm[0]
    out_ref[0:1, :] = run - sm[0]
    for l in range(1, depth):
        run = run + sm[l]
        out_ref[l:l + 1, :] = run - sm[0]


def _lower_bounds_call(hgrn_lower_bounds):
    return pl.pallas_call(
        _lower_bounds_kernel,
        out_shape=jax.ShapeDtypeStruct(hgrn_lower_bounds.shape, F32),
        name="lower_bounds",
    )(hgrn_lower_bounds.astype(F32))


def _adaln_kernel(c_ref, w_ref, b_ref, mod_ref):
    c = c_ref[...]
    sc = c * _sigmoid(c)
    w = w_ref[0]
    a_hi, a_lo = _split_bf16(sc, 2)
    w_hi, w_lo = _split_bf16(w, 2)
    acc = _dot(a_hi, w_hi) + _dot(a_hi, w_lo) + _dot(a_lo, w_hi)
    mod_ref[0] = acc + b_ref[0]


def _adaln_call(c, ada_w, ada_b):
    depth, d, d3 = ada_w.shape
    batch = c.shape[0]
    n_col = d3 // d
    return pl.pallas_call(
        _adaln_kernel,
        grid=(depth, n_col),
        in_specs=[
            pl.BlockSpec((batch, d), lambda l, j: (0, 0)),
            pl.BlockSpec((1, d, d), lambda l, j: (l, 0, j)),
            pl.BlockSpec((1, 1, d), lambda l, j: (l, 0, j)),
        ],
        out_specs=pl.BlockSpec((1, batch, d), lambda l, j: (l, 0, j)),
        out_shape=jax.ShapeDtypeStruct((depth, batch, d3), F32),
        name="adaln",
    )(c.astype(F32), ada_w.astype(F32), ada_b.astype(F32).reshape(depth, 1, d3))


def _layer_kernel(tiles_per_seq, n_tiles,
                  xf_ref, xb_ref, modf_ref, modb_ref, npre_ref, w_in_ref, pool_w_ref,
                  pool_scale_ref, lb_ref, hnw_ref, w_out_ref, npost_ref, wsum_ref, lvl_ref,
                  out_ref,
                  state_ref, tail_ref, q_ref, k_ref, v_ref, g_ref, bsum_ref, mixp_ref, flag_ref,
                  mixh_ref, attn_ref, b_ref, ql_ref, kl_ref):
    ts = SEQ_TILE
    n_chunks = ts // CHUNK
    step = pl.program_id(0)
    front_tile = jnp.minimum(step, n_tiles - 1)
    back_tile = jnp.maximum(step - 1, 0)
    slot_f = step % SLOTS
    slot_b = (step + 1) % SLOTS

    @pl.when(step == 0)
    def _():
        for ref in (q_ref, k_ref, v_ref, g_ref, bsum_ref, mixp_ref):
            ref[...] = jnp.zeros_like(ref)
        flag_ref[1] = 1

    @pl.when(front_tile % tiles_per_seq == 0)
    def _():
        tail_ref[...] = jnp.zeros_like(tail_ref)

    @pl.when(back_tile % tiles_per_seq == 0)
    def _():
        state_ref[...] = jnp.zeros_like(state_ref)

    lvl = lvl_ref[...]
    causal = lvl >= 0
    half_hnw = 0.5 * hnw_ref[...]
    heads = [slice(hd * HEAD_DIM, (hd + 1) * HEAD_DIM) for hd in range(HGRN_HEADS)]

    def front():
        x = xf_ref[0]
        mod = modf_ref[0]
        shift = mod[:, 0:D_MODEL]
        scale = mod[:, D_MODEL:2 * D_MODEL]
        ms = jnp.mean(x * x, axis=-1, keepdims=True)
        row_gain = npre_ref[...] * (1.0 + scale)
        h = ((x * lax.rsqrt(ms + EPS)) * row_gain + shift).astype(BF16)

        def in_proj(section):
            return _dot(h, _unpack_rows(w_in_ref[:, SECTION_STARTS[section]:SECTION_STARTS[section + 1]]))

        u = in_proj(SEC_POOL_U)
        part = jnp.concatenate([tail_ref[...], u], axis=0)
        tail_ref[...] = u[ts - POOL_TAIL:ts, :]
        win_sum = []
        for g, w in enumerate(POOL_WINDOWS):
            part = part[:, (POOL_GROUP_WIDTH if g else 0):]
            part = part + pltpu.roll(part, w // 2, 0)
            win_sum.append(part[POOL_TAIL:, 0:POOL_GROUP_WIDTH])
        g_pool = in_proj(SEC_POOL_GATE)
        pos = (front_tile % tiles_per_seq) * ts + lax.broadcasted_iota(jnp.int32, (ts, 1), 0)
        for g, w in enumerate(POOL_WINDOWS):
            lo, hi = g * POOL_GROUP_WIDTH, (g + 1) * POOL_GROUP_WIDTH
            u_g = u[:, lo:hi]
            inv_cnt = 1.0 / jnp.minimum(pos + 1, w).astype(F32)
            d = (win_sum[g] * inv_cnt - u_g).astype(BF16)
            y = _dot(d, pool_w_ref[g]) * (0.5 * pool_scale_ref[:, lo:hi])
            gp = g_pool[:, lo:hi]
            mixp_ref[slot_f, :, lo:hi] = (y * _twice_silu(gp)).astype(BF16)

        q_ref[slot_f] = in_proj(SEC_Q)
        lb = lb_ref[...]
        f = lb + (1.0 - lb) * _sigmoid(in_proj(SEC_FORGET))
        k_ref[slot_f] = 1.0 - f
        lf = jnp.log(f) * LOG2E
        v_ref[slot_f] = in_proj(SEC_V)
        g_ref[slot_f] = in_proj(SEC_OUT_GATE)
        unsafe = None
        for c in range(n_chunks):
            lf3 = jnp.concatenate(_split_bf16(lf[c * CHUNK:(c + 1) * CHUNK, :], SPLIT), axis=0)
            b_c = _dot(wsum_ref[...], lf3)
            bsum_ref[slot_f, c * CHUNK:(c + 1) * CHUNK, :] = b_c
            b_mid = b_c[MID_ROW:MID_ROW + 1, :]
            b_last = b_c[CHUNK - 1:CHUNK, :]
            ok = jnp.logical_and(-b_mid <= DIRECT_MAX_LOG2, b_mid - b_last <= DIRECT_MAX_LOG2)
            bad = jnp.where(ok, 0.0, 1.0)
            unsafe = bad if unsafe is None else jnp.maximum(unsafe, bad)
        flag_ref[slot_f] = (jnp.max(unsafe) < 0.5).astype(jnp.int32)

    def chunk_outputs(rows, qe, k_dec, e_last, attn):
        v_c = v_ref[slot_b, rows, :]
        k_dec_b = k_dec.astype(BF16)
        for hd, hl in enumerate(heads):
            v_t = jnp.transpose(v_c[:, hl]).astype(BF16)
            state_t = state_ref[hd]
            o = _dot_nt(jnp.concatenate([qe[:, hl], attn[hd]], axis=1),
                        jnp.concatenate([state_t.astype(BF16), v_t], axis=1))
            on = o * lax.rsqrt(jnp.mean(o * o, axis=-1, keepdims=True) + EPS) * half_hnw
            gh = g_ref[slot_b, rows, hl]
            mixh_ref[rows, hl] = (on * _twice_silu(gh)).astype(BF16)
            state_ref[hd] = state_t * e_last[:, hl] + _dot(v_t, k_dec_b[:, hl])

    def direct_chunk(c):
        rows = slice(c * CHUNK, (c + 1) * CHUNK)
        b_c = bsum_ref[slot_b, rows, :]
        b_mid = bsum_ref[slot_b, c * CHUNK + MID_ROW:c * CHUNK + MID_ROW + 1, :]
        b_last = bsum_ref[slot_b, (c + 1) * CHUNK - 1:(c + 1) * CHUNK, :]
        d = b_c - b_mid
        q_s = q_ref[slot_b, rows, :] * jnp.exp2(d)
        k_s = k_ref[slot_b, rows, :] * jnp.exp2(-d)
        q_b = q_s.astype(BF16)
        k_b = k_s.astype(BF16)
        attn = [jnp.where(causal, _dot_nt(q_b[:, hl], k_b[:, hl]), 0.0).astype(BF16) for hl in heads]
        chunk_outputs(rows, (q_s * jnp.exp2(b_mid)).astype(BF16),
                      k_s * jnp.exp2(b_last - b_mid), jnp.exp2(b_last), attn)

    def robust_chunk(c, carry):
        rows = pl.ds(pl.multiple_of(c * CHUNK, CHUNK), CHUNK)
        b_ref[...] = bsum_ref[slot_b, rows, :]
        b_c = b_ref[...]
        q_c = q_ref[slot_b, rows, :]
        k_c = k_ref[slot_b, rows, :]
        ql_ref[0:CHUNK, :] = q_c.astype(BF16)
        kl_ref[0] = k_c.astype(BF16)
        ql_ref[CHUNK:2 * CHUNK, :] = (q_c * (1.0 - k_c)).astype(BF16)
        row_in_group = lax.broadcasted_iota(jnp.int32, (SUBLANES, 1), 0)
        for level in range(2, NUM_LEVELS + 1):
            m = 1 << (level - 1)
            refs = []
            for r in range(0, CHUNK, SUBLANES):
                if 2 * m >= SUBLANES:
                    refs.append(jnp.broadcast_to(b_ref[_ref_row(level, r):_ref_row(level, r) + 1, :],
                                                 (SUBLANES, HGRN_WIDTH)))
                else:
                    lo_ref = b_ref[_ref_row(level, r):_ref_row(level, r) + 1, :]
                    hi_ref = b_ref[_ref_row(level, r + 2 * m):_ref_row(level, r + 2 * m) + 1, :]
                    refs.append(jnp.where(row_in_group < 2 * m, lo_ref, hi_ref))
            e_l = jnp.exp2(-jnp.abs(b_c - jnp.concatenate(refs, axis=0)))
            ql_ref[level * CHUNK:(level + 1) * CHUNK, :] = (q_c * e_l).astype(BF16)
            kl_ref[level] = (k_c * e_l).astype(BF16)
        for hd, hl in enumerate(heads):
            a01 = _dot_nt(ql_ref[0:2 * CHUNK, hl], kl_ref[0, :, hl])
            attn = jnp.where(lvl == 0, a01[0:CHUNK], 0.0)
            attn = jnp.where(lvl == 1, a01[CHUNK:2 * CHUNK], attn)
            for level in range(2, NUM_LEVELS + 1):
                a_l = _dot_nt(ql_ref[level * CHUNK:(level + 1) * CHUNK, hl], kl_ref[level, :, hl])
                attn = jnp.where(lvl == level, a_l, attn)
            attn_ref[hd] = attn.astype(BF16)
        b_last = b_ref[CHUNK - 1:CHUNK, :]
        chunk_outputs(rows, (q_c * jnp.exp2(b_c)).astype(BF16),
                      k_c * jnp.exp2(b_last - b_c), jnp.exp2(b_last),
                      [attn_ref[hd] for hd in range(HGRN_HEADS)])
        return carry

    def back_epilogue():
        gate = modb_ref[0][:, 2 * D_MODEL:3 * D_MODEL]
        mix = jnp.concatenate([mixp_ref[slot_b], mixh_ref[...]], axis=1)
        y = _dot(mix, _unpack_rows(w_out_ref[...]))
        row_gain = gate * npost_ref[...]
        out_ref[0] = xb_ref[0] + (y * lax.rsqrt(jnp.mean(y * y, axis=-1, keepdims=True) + EPS)) * row_gain

    use_direct = flag_ref[slot_b] == 1

    @pl.when(use_direct)
    def _():
        front()
        for c in range(n_chunks):
            direct_chunk(c)
        back_epilogue()

    @pl.when(jnp.logical_not(use_direct))
    def _():
        front()
        lax.fori_loop(0, n_chunks, robust_chunk, 0)
        back_epilogue()


def _layer_call(layer, x, mod, npre, w_in, pool_w, pool_scale, lb, hnw, w_out, npost, wsum, lvl):
    batch, seq, d = x.shape
    ts = SEQ_TILE
    tiles_per_seq = seq // ts
    n_tiles = batch * tiles_per_seq

    def front_tile(g):
        t = jnp.minimum(g, n_tiles - 1)
        return t // tiles_per_seq, t % tiles_per_seq

    def back_tile(g):
        t = jnp.maximum(g - 1, 0)
        return t // tiles_per_seq, t % tiles_per_seq

    const2 = lambda g: (0, 0)
    of_layer3 = lambda g: (layer, 0, 0)
    of_layer4 = lambda g: (layer, 0, 0, 0)
    once = pl.Buffered(1)
    return pl.pallas_call(
        functools.partial(_layer_kernel, tiles_per_seq, n_tiles),
        grid=(n_tiles + 1,),
        in_specs=[
            pl.BlockSpec((1, ts, d), lambda g: (*front_tile(g), 0)),
            pl.BlockSpec((1, ts, d), lambda g: (*back_tile(g), 0)),
            pl.BlockSpec((None, 1, 1, 3 * d), lambda g: (layer, front_tile(g)[0], 0, 0)),
            pl.BlockSpec((None, 1, 1, 3 * d), lambda g: (layer, back_tile(g)[0], 0, 0)),
            pl.BlockSpec((None, 1, d), of_layer3),
            pl.BlockSpec((None, d // 2, IN_WIDTH), of_layer3, pipeline_mode=once),
            pl.BlockSpec((None,) + pool_w.shape[1:], of_layer4, pipeline_mode=once),
            pl.BlockSpec((None, 1, POOL_WIDTH), of_layer3),
            pl.BlockSpec((None, 1, HGRN_WIDTH), of_layer3),
            pl.BlockSpec((None, 1, HEAD_DIM), of_layer3),
            pl.BlockSpec((None, MIX_WIDTH // 2, d), of_layer3, pipeline_mode=once),
            pl.BlockSpec((None, 1, d), of_layer3),
            pl.BlockSpec(wsum.shape, const2),
            pl.BlockSpec(lvl.shape, const2),
        ],
        out_specs=pl.BlockSpec((1, ts, d), lambda g: (*back_tile(g), 0)),
        out_shape=jax.ShapeDtypeStruct(x.shape, F32),
        scratch_shapes=[
            pltpu.VMEM((HGRN_HEADS, HEAD_DIM, HEAD_DIM), F32),
            pltpu.VMEM((POOL_TAIL, POOL_WIDTH), F32),
            pltpu.VMEM((SLOTS, ts, HGRN_WIDTH), F32),
            pltpu.VMEM((SLOTS, ts, HGRN_WIDTH), F32),
            pltpu.VMEM((SLOTS, ts, HGRN_WIDTH), F32),
            pltpu.VMEM((SLOTS, ts, HGRN_WIDTH), F32),
            pltpu.VMEM((SLOTS, ts, HGRN_WIDTH), F32),
            pltpu.VMEM((SLOTS, ts, POOL_WIDTH), BF16),
            pltpu.SMEM((SLOTS,), jnp.int32),
            pltpu.VMEM((ts, HGRN_WIDTH), BF16),
            pltpu.VMEM((HGRN_HEADS, CHUNK, CHUNK), BF16),
            pltpu.VMEM((CHUNK, HGRN_WIDTH), F32),
            pltpu.VMEM(((NUM_LEVELS + 1) * CHUNK, HGRN_WIDTH), BF16),
            pltpu.VMEM((NUM_LEVELS + 1, CHUNK, HGRN_WIDTH), BF16),
        ],
        compiler_params=pltpu.CompilerParams(
            dimension_semantics=("arbitrary",),
            vmem_limit_bytes=VMEM_LIMIT_BYTES),
        name="hybrid_layer",
    )(x, x, mod, mod, npre, w_in, pool_w, pool_scale, lb, hnw, w_out, npost, wsum, lvl)


def kernel(x, c, norm_pre_w, ada_w, ada_b, w_in, pool_w, pool_scale, hgrn_lower_bounds,
           hgrn_norm_w, w_out, norm_post_w):
    depth = ada_w.shape[0]
    batch = x.shape[0]
    lower = _lower_bounds_call(hgrn_lower_bounds)
    mod = _adaln_call(c, ada_w, ada_b)
    wsum = jnp.asarray(_cumsum_matrix(), dtype=BF16)
    lvl = jnp.asarray(_level_map())
    stacked = (
        mod.reshape(depth, batch, 1, 3 * D_MODEL),
        norm_pre_w.astype(F32).reshape(depth, 1, D_MODEL),
        _pack_rows(w_in),
        pool_w.astype(BF16),
        pool_scale.astype(F32).reshape(depth, 1, POOL_WIDTH),
        lower.reshape(depth, 1, HGRN_WIDTH),
        hgrn_norm_w.astype(F32).reshape(depth, 1, HEAD_DIM),
        _pack_rows(w_out),
        norm_post_w.astype(F32).reshape(depth, 1, D_MODEL),
    )
    h = x.astype(F32)
    for l in range(depth):
        h = _layer_call(l, h, *stacked, wsum, lvl)
    return h.astype(x.dtype)
```

```python
import functools

import numpy as np
import jax
import jax.numpy as jnp
from jax import lax
from jax.experimental import pallas as pl
from jax.experimental.pallas import tpu as pltpu

D_MODEL = 1024
CHUNK = 64
POOL_WIDTH = 1024
POOL_WINDOWS = (2, 4, 8, 16)
POOL_GROUP_WIDTH = POOL_WIDTH // len(POOL_WINDOWS)
HGRN_WIDTH = 1024
HEAD_DIM = 128
HGRN_HEADS = HGRN_WIDTH // HEAD_DIM
MIX_WIDTH = POOL_WIDTH + HGRN_WIDTH
IN_WIDTH = 2 * POOL_WIDTH + 4 * HGRN_WIDTH
SECTION_WIDTHS = (POOL_WIDTH, POOL_WIDTH, HGRN_WIDTH, HGRN_WIDTH, HGRN_WIDTH, HGRN_WIDTH)
SECTION_STARTS = tuple(int(v) for v in np.cumsum((0,) + SECTION_WIDTHS))
assert SECTION_STARTS[-1] == IN_WIDTH
SEC_POOL_U, SEC_POOL_GATE, SEC_Q, SEC_FORGET, SEC_V, SEC_OUT_GATE = range(len(SECTION_WIDTHS))
EPS = 1e-6

SEQ_TILE = 512
POOL_TAIL = 16
assert POOL_WINDOWS == tuple(2 << g for g in range(len(POOL_WINDOWS))) and POOL_TAIL >= POOL_WINDOWS[-1]
NUM_LEVELS = 6
SPLIT = 2
VMEM_LIMIT_BYTES = 62 * 1024 * 1024
SUBLANES = 8
LOG2E = 1.4426950408889634
MID_ROW = CHUNK // 2 - 1
DIRECT_MAX_LOG2 = 110.0
SLOTS = 2
FRONT_ROWS = SEQ_TILE // 2

F32 = jnp.float32
BF16 = jnp.bfloat16


def _ref_row(level, row):
    m = 1 << (level - 1)
    return (row // (2 * m)) * (2 * m) + m - 1


def _cumsum_matrix():
    t = np.arange(CHUNK)[:, None]
    j = np.arange(CHUNK)[None, :]
    w = (j <= t).astype(np.float32)
    return np.concatenate([w] * SPLIT, axis=1)


def _level_map():
    t = np.arange(CHUNK)[:, None]
    s = np.arange(CHUNK)[None, :]
    x = t ^ s
    lvl = np.where(x > 0, np.floor(np.log2(np.maximum(x, 1))).astype(np.int32) + 1, 0)
    return np.where(s > t, -1, lvl).astype(np.int32)


def _split_bf16(a, pieces):
    out = []
    rest = a
    for _ in range(pieces):
        p = rest.astype(BF16)
        out.append(p)
        rest = rest - p.astype(F32)
    return out


def _sigmoid(a):
    return 1.0 / (1.0 + jnp.exp(-a))


def _twice_silu(a):
    return a * (jnp.tanh(0.5 * a) + 1.0)


PACK_ROWS, PACK_COLS = 1024, 1024


def _pack_rows_kernel(w_ref, out_ref):
    out_ref[0] = pltpu.bitcast(w_ref[0].astype(BF16), jnp.uint32)


def _pack_rows(w):
    layers, k, n = w.shape
    rows, cols = min(PACK_ROWS, k), min(PACK_COLS, n)
    return pl.pallas_call(
        _pack_rows_kernel,
        grid=(layers, k // rows, n // cols),
        in_specs=[pl.BlockSpec((1, rows, cols), lambda l, i, j: (l, i, j))],
        out_specs=pl.BlockSpec((1, rows // 2, cols), lambda l, i, j: (l, i, j)),
        out_shape=jax.ShapeDtypeStruct((layers, k // 2, n), jnp.uint32),
        name="pack_rows",
    )(w.astype(F32))


def _unpack_rows(words):
    return pltpu.bitcast(words, BF16)


def _dot(a, b):
    return jnp.dot(a, b, preferred_element_type=F32)


def _dot_nt(a, b):
    return lax.dot_general(a, b, (((1,), (1,)), ((), ())), preferred_element_type=F32)


def _lower_bounds_kernel(lb_ref, out_ref):
    depth = lb_ref.shape[0]
    rows = [lb_ref[l:l + 1, :] for l in range(depth)]
    mx = rows[0]
    for r in rows[1:]:
        mx = jnp.maximum(mx, r)
    ex = [jnp.exp(r - mx) for r in rows]
    den = ex[0]
    for e in ex[1:]:
        den = den + e
    sm = [e / den for e in ex]
    run = sm[0]
    out_ref[0:1, :] = run - sm[0]
    for l in range(1, depth):
        run = run + sm[l]
        out_ref[l:l + 1, :] = run - sm[0]


def _lower_bounds_call(hgrn_lower_bounds):
    return pl.pallas_call(
        _lower_bounds_kernel,
        out_shape=jax.ShapeDtypeStruct(hgrn_lower_bounds.shape, F32),
        name="lower_bounds",
    )(hgrn_lower_bounds.astype(F32))


def _adaln_kernel(c_ref, w_ref, b_ref, mod_ref):
    c = c_ref[...]
    sc = c * _sigmoid(c)
    w = w_ref[0]
    a_hi, a_lo = _split_bf16(sc, 2)
    w_hi, w_lo = _split_bf16(w, 2)
    acc = _dot(a_hi, w_hi) + _dot(a_hi, w_lo) + _dot(a_lo, w_hi)
    mod_ref[0] = acc + b_ref[0]


def _adaln_call(c, ada_w, ada_b):
    depth, d, d3 = ada_w.shape
    batch = c.shape[0]
    n_col = d3 // d
    return pl.pallas_call(
        _adaln_kernel,
        grid=(depth, n_col),
        in_specs=[
            pl.BlockSpec((batch, d), lambda l, j: (0, 0)),
            pl.BlockSpec((1, d, d), lambda l, j: (l, 0, j)),
            pl.BlockSpec((1, 1, d), lambda l, j: (l, 0, j)),
        ],
        out_specs=pl.BlockSpec((1, batch, d), lambda l, j: (l, 0, j)),
        out_shape=jax.ShapeDtypeStruct((depth, batch, d3), F32),
        name="adaln",
    )(c.astype(F32), ada_w.astype(F32), ada_b.astype(F32).reshape(depth, 1, d3))


def _layer_kernel(tiles_per_seq, n_tiles,
                  xf_ref, xb_ref, modf_ref, modb_ref, npre_ref, w_in_ref, pool_w_ref,
                  pool_scale_ref, lb_ref, hnw_ref, w_out_ref, npost_ref, wsum_ref, lvl_ref,
                  out_ref,
                  state_ref, tail_ref, q_ref, k_ref, v_ref, g_ref, bsum_ref, mixp_ref, flag_ref,
                  mixh_ref, attn_ref, b_ref, ql_ref, kl_ref):
    ts = SEQ_TILE
    n_chunks = ts // CHUNK
    step = pl.program_id(0)
    front_tile = jnp.minimum(step, n_tiles - 1)
    back_tile = jnp.maximum(step - 1, 0)
    slot_f = step % SLOTS
    slot_b = (step + 1) % SLOTS

    @pl.when(step == 0)
    def _():
        for ref in (q_ref, k_ref, v_ref, g_ref, bsum_ref, mixp_ref):
            ref[...] = jnp.zeros_like(ref)
        flag_ref[1] = 1

    @pl.when(front_tile % tiles_per_seq == 0)
    def _():
        tail_ref[...] = jnp.zeros_like(tail_ref)

    @pl.when(back_tile % tiles_per_seq == 0)
    def _():
        state_ref[...] = jnp.zeros_like(state_ref)

    lvl = lvl_ref[...]
    causal = lvl >= 0
    half_hnw = 0.5 * hnw_ref[...]
    heads = [slice(hd * HEAD_DIM, (hd + 1) * HEAD_DIM) for hd in range(HGRN_HEADS)]

    def front():
        unsafe = None
        for r0 in range(0, ts, FRONT_ROWS):
            bad = front_rows(r0, FRONT_ROWS)
            unsafe = bad if unsafe is None else jnp.maximum(unsafe, bad)
        flag_ref[slot_f] = (jnp.max(unsafe) < 0.5).astype(jnp.int32)

    def front_rows(r0, ts):
        out_rows = slice(r0, r0 + ts)
        x = xf_ref[0, out_rows, :]
        mod = modf_ref[0]
        shift = mod[:, 0:D_MODEL]
        scale = mod[:, D_MODEL:2 * D_MODEL]
        ms = jnp.mean(x * x, axis=-1, keepdims=True)
        row_gain = npre_ref[...] * (1.0 + scale)
        h = ((x * lax.rsqrt(ms + EPS)) * row_gain + shift).astype(BF16)

        def in_proj(section):
            return _dot(h, _unpack_rows(w_in_ref[:, SECTION_STARTS[section]:SECTION_STARTS[section + 1]]))

        u = in_proj(SEC_POOL_U)
        part = jnp.concatenate([tail_ref[...], u], axis=0)
        tail_ref[...] = u[ts - POOL_TAIL:ts, :]
        win_sum = []
        for g, w in enumerate(POOL_WINDOWS):
            part = part[:, (POOL_GROUP_WIDTH if g else 0):]
            part = part + pltpu.roll(part, w // 2, 0)
            win_sum.append(part[POOL_TAIL:, 0:POOL_GROUP_WIDTH])
        g_pool = in_proj(SEC_POOL_GATE)
        pos = (front_tile % tiles_per_seq) * SEQ_TILE + r0 + lax.broadcasted_iota(jnp.int32, (ts, 1), 0)
        for g, w in enumerate(POOL_WINDOWS):
            lo, hi = g * POOL_GROUP_WIDTH, (g + 1) * POOL_GROUP_WIDTH
            u_g = u[:, lo:hi]
            inv_cnt = 1.0 / jnp.minimum(pos + 1, w).astype(F32)
            d = (win_sum[g] * inv_cnt - u_g).astype(BF16)
            y = _dot(d, pool_w_ref[g]) * (0.5 * pool_scale_ref[:, lo:hi])
            gp = g_pool[:, lo:hi]
            mixp_ref[slot_f, out_rows, lo:hi] = (y * _twice_silu(gp)).astype(BF16)

        q_ref[slot_f, out_rows, :] = in_proj(SEC_Q)
        lb = lb_ref[...]
        f = lb + (1.0 - lb) * _sigmoid(in_proj(SEC_FORGET))
        k_ref[slot_f, out_rows, :] = 1.0 - f
        lf = jnp.log(f) * LOG2E
        v_ref[slot_f, out_rows, :] = in_proj(SEC_V)
        g_ref[slot_f, out_rows, :] = in_proj(SEC_OUT_GATE)
        unsafe = None
        for c in range(ts // CHUNK):
            lf3 = jnp.concatenate(_split_bf16(lf[c * CHUNK:(c + 1) * CHUNK, :], SPLIT), axis=0)
            b_c = _dot(wsum_ref[...], lf3)
            bsum_ref[slot_f, r0 + c * CHUNK:r0 + (c + 1) * CHUNK, :] = b_c
            b_mid = b_c[MID_ROW:MID_ROW + 1, :]
            b_last = b_c[CHUNK - 1:CHUNK, :]
            ok = jnp.logical_and(-b_mid <= DIRECT_MAX_LOG2, b_mid - b_last <= DIRECT_MAX_LOG2)
            bad = jnp.where(ok, 0.0, 1.0)
            unsafe = bad if unsafe is None else jnp.maximum(unsafe, bad)
        return unsafe

    def chunk_outputs(rows, qe, k_dec, e_last, attn):
        v_c = v_ref[slot_b, rows, :]
        k_dec_b = k_dec.astype(BF16)
        for hd, hl in enumerate(heads):
            v_t = jnp.transpose(v_c[:, hl]).astype(BF16)
            state_t = state_ref[hd]
            o = _dot_nt(jnp.concatenate([qe[:, hl], attn[hd]], axis=1),
                        jnp.concatenate([state_t.astype(BF16), v_t], axis=1))
            on = o * lax.rsqrt(jnp.mean(o * o, axis=-1, keepdims=True) + EPS) * half_hnw
            gh = g_ref[slot_b, rows, hl]
            mixh_ref[rows, hl] = (on * _twice_silu(gh)).astype(BF16)
            state_ref[hd] = state_t * e_last[:, hl] + _dot(v_t, k_dec_b[:, hl])

    def direct_chunk(c):
        rows = slice(c * CHUNK, (c + 1) * CHUNK)
        b_c = bsum_ref[slot_b, rows, :]
        b_mid = bsum_ref[slot_b, c * CHUNK + MID_ROW:c * CHUNK + MID_ROW + 1, :]
        b_last = bsum_ref[slot_b, (c + 1) * CHUNK - 1:(c + 1) * CHUNK, :]
        d = b_c - b_mid
        q_s = q_ref[slot_b, rows, :] * jnp.exp2(d)
        k_s = k_ref[slot_b, rows, :] * jnp.exp2(-d)
        q_b = q_s.astype(BF16)
        k_b = k_s.astype(BF16)
        attn = [jnp.where(causal, _dot_nt(q_b[:, hl], k_b[:, hl]), 0.0).astype(BF16) for hl in heads]
        chunk_outputs(rows, (q_s * jnp.exp2(b_mid)).astype(BF16),
                      k_s * jnp.exp2(b_last - b_mid), jnp.exp2(b_last), attn)

    def robust_chunk(c, carry):
        rows = pl.ds(pl.multiple_of(c * CHUNK, CHUNK), CHUNK)
        b_ref[...] = bsum_ref[slot_b, rows, :]
        b_c = b_ref[...]
        q_c = q_ref[slot_b, rows, :]
        k_c = k_ref[slot_b, rows, :]
        ql_ref[0:CHUNK, :] = q_c.astype(BF16)
        kl_ref[0] = k_c.astype(BF16)
        ql_ref[CHUNK:2 * CHUNK, :] = (q_c * (1.0 - k_c)).astype(BF16)
        row_in_group = lax.broadcasted_iota(jnp.int32, (SUBLANES, 1), 0)
        for level in range(2, NUM_LEVELS + 1):
            m = 1 << (level - 1)
            refs = []
            for r in range(0, CHUNK, SUBLANES):
                if 2 * m >= SUBLANES:
                    refs.append(jnp.broadcast_to(b_ref[_ref_row(level, r):_ref_row(level, r) + 1, :],
                                                 (SUBLANES, HGRN_WIDTH)))
                else:
                    lo_ref = b_ref[_ref_row(level, r):_ref_row(level, r) + 1, :]
                    hi_ref = b_ref[_ref_row(level, r + 2 * m):_ref_row(level, r + 2 * m) + 1, :]
                    refs.append(jnp.where(row_in_group < 2 * m, lo_ref, hi_ref))
            e_l = jnp.exp2(-jnp.abs(b_c - jnp.concatenate(refs, axis=0)))
            ql_ref[level * CHUNK:(level + 1) * CHUNK, :] = (q_c * e_l).astype(BF16)
            kl_ref[level] = (k_c * e_l).astype(BF16)
        for hd, hl in enumerate(heads):
            a01 = _dot_nt(ql_ref[0:2 * CHUNK, hl], kl_ref[0, :, hl])
            attn = jnp.where(lvl == 0, a01[0:CHUNK], 0.0)
            attn = jnp.where(lvl == 1, a01[CHUNK:2 * CHUNK], attn)
            for level in range(2, NUM_LEVELS + 1):
                a_l = _dot_nt(ql_ref[level * CHUNK:(level + 1) * CHUNK, hl], kl_ref[level, :, hl])
                attn = jnp.where(lvl == level, a_l, attn)
            attn_ref[hd] = attn.astype(BF16)
        b_last = b_ref[CHUNK - 1:CHUNK, :]
        chunk_outputs(rows, (q_c * jnp.exp2(b_c)).astype(BF16),
                      k_c * jnp.exp2(b_last - b_c), jnp.exp2(b_last),
                      [attn_ref[hd] for hd in range(HGRN_HEADS)])
        return carry

    def back_epilogue():
        gate = modb_ref[0][:, 2 * D_MODEL:3 * D_MODEL]
        mix = jnp.concatenate([mixp_ref[slot_b], mixh_ref[...]], axis=1)
        y = _dot(mix, _unpack_rows(w_out_ref[...]))
        row_gain = gate * npost_ref[...]
        out_ref[0] = xb_ref[0] + (y * lax.rsqrt(jnp.mean(y * y, axis=-1, keepdims=True) + EPS)) * row_gain

    use_direct = flag_ref[slot_b] == 1

    @pl.when(use_direct)
    def _():
        front()
        for c in range(n_chunks):
            direct_chunk(c)
        back_epilogue()

    @pl.when(jnp.logical_not(use_direct))
    def _():
        front()
        lax.fori_loop(0, n_chunks, robust_chunk, 0)
        back_epilogue()


def _layer_call(layer, x, mod, npre, w_in, pool_w, pool_scale, lb, hnw, w_out, npost, wsum, lvl):
    batch, seq, d = x.shape
    ts = SEQ_TILE
    tiles_per_seq = seq // ts
    n_tiles = batch * tiles_per_seq

    def front_tile(g):
        t = jnp.minimum(g, n_tiles - 1)
        return t // tiles_per_seq, t % tiles_per_seq

    def back_tile(g):
        t = jnp.maximum(g - 1, 0)
        return t // tiles_per_seq, t % tiles_per_seq

    const2 = lambda g: (0, 0)
    of_layer3 = lambda g: (layer, 0, 0)
    of_layer4 = lambda g: (layer, 0, 0, 0)
    once = pl.Buffered(1)
    return pl.pallas_call(
        functools.partial(_layer_kernel, tiles_per_seq, n_tiles),
        grid=(n_tiles + 1,),
        in_specs=[
            pl.BlockSpec((1, ts, d), lambda g: (*front_tile(g), 0)),
            pl.BlockSpec((1, ts, d), lambda g: (*back_tile(g), 0)),
            pl.BlockSpec((None, 1, 1, 3 * d), lambda g: (layer, front_tile(g)[0], 0, 0)),
            pl.BlockSpec((None, 1, 1, 3 * d), lambda g: (layer, back_tile(g)[0], 0, 0)),
            pl.BlockSpec((None, 1, d), of_layer3),
            pl.BlockSpec((None, d // 2, IN_WIDTH), of_layer3, pipeline_mode=once),
            pl.BlockSpec((None,) + pool_w.shape[1:], of_layer4, pipeline_mode=once),
            pl.BlockSpec((None, 1, POOL_WIDTH), of_layer3),
            pl.BlockSpec((None, 1, HGRN_WIDTH), of_layer3),
            pl.BlockSpec((None, 1, HEAD_DIM), of_layer3),
            pl.BlockSpec((None, MIX_WIDTH // 2, d), of_layer3, pipeline_mode=once),
            pl.BlockSpec((None, 1, d), of_layer3),
            pl.BlockSpec(wsum.shape, const2),
            pl.BlockSpec(lvl.shape, const2),
        ],
        out_specs=pl.BlockSpec((1, ts, d), lambda g: (*back_tile(g), 0)),
        out_shape=jax.ShapeDtypeStruct(x.shape, F32),
        scratch_shapes=[
            pltpu.VMEM((HGRN_HEADS, HEAD_DIM, HEAD_DIM), F32),
            pltpu.VMEM((POOL_TAIL, POOL_WIDTH), F32),
            pltpu.VMEM((SLOTS, ts, HGRN_WIDTH), F32),
            pltpu.VMEM((SLOTS, ts, HGRN_WIDTH), F32),
            pltpu.VMEM((SLOTS, ts, HGRN_WIDTH), F32),
            pltpu.VMEM((SLOTS, ts, HGRN_WIDTH), F32),
            pltpu.VMEM((SLOTS, ts, HGRN_WIDTH), F32),
            pltpu.VMEM((SLOTS, ts, POOL_WIDTH), BF16),
            pltpu.SMEM((SLOTS,), jnp.int32),
            pltpu.VMEM((ts, HGRN_WIDTH), BF16),
            pltpu.VMEM((HGRN_HEADS, CHUNK, CHUNK), BF16),
            pltpu.VMEM((CHUNK, HGRN_WIDTH), F32),
            pltpu.VMEM(((NUM_LEVELS + 1) * CHUNK, HGRN_WIDTH), BF16),
            pltpu.VMEM((NUM_LEVELS + 1, CHUNK, HGRN_WIDTH), BF16),
        ],
        compiler_params=pltpu.CompilerParams(
            dimension_semantics=("arbitrary",),
            vmem_limit_bytes=VMEM_LIMIT_BYTES),
        name="hybrid_layer",
    )(x, x, mod, mod, npre, w_in, pool_w, pool_scale, lb, hnw, w_out, npost, wsum, lvl)


def kernel(x, c, norm_pre_w, ada_w, ada_b, w_in, pool_w, pool_scale, hgrn_lower_bounds,
           hgrn_norm_w, w_out, norm_post_w):
    depth = ada_w.shape[0]
    batch = x.shape[0]
    lower = _lower_bounds_call(hgrn_lower_bounds)
    mod = _adaln_call(c, ada_w, ada_b)
    wsum = jnp.asarray(_cumsum_matrix(), dtype=BF16)
    lvl = jnp.asarray(_level_map())
    stacked = (
        mod.reshape(depth, batch, 1, 3 * D_MODEL),
        norm_pre_w.astype(F32).reshape(depth, 1, D_MODEL),
        _pack_rows(w_in),
        pool_w.astype(BF16),
        pool_scale.astype(F32).reshape(depth, 1, POOL_WIDTH),
        lower.reshape(depth, 1, HGRN_WIDTH),
        hgrn_norm_w.astype(F32).reshape(depth, 1, HEAD_DIM),
        _pack_rows(w_out),
        norm_post_w.astype(F32).reshape(depth, 1, D_MODEL),
    )
    h = x.astype(F32)
    for l in range(depth):
        h = _layer_call(l, h, *stacked, wsum, lvl)
    return h.astype(x.dtype)
```

```python
import functools

import numpy as np
import jax
import jax.numpy as jnp
from jax import lax
from jax.experimental import pallas as pl
from jax.experimental.pallas import tpu as pltpu

D_MODEL = 1024
CHUNK = 64
POOL_WIDTH = 1024
POOL_WINDOWS = (2, 4, 8, 16)
POOL_GROUP_WIDTH = POOL_WIDTH // len(POOL_WINDOWS)
HGRN_WIDTH = 1024
HEAD_DIM = 128
HGRN_HEADS = HGRN_WIDTH // HEAD_DIM
MIX_WIDTH = POOL_WIDTH + HGRN_WIDTH
IN_WIDTH = 2 * POOL_WIDTH + 4 * HGRN_WIDTH
SECTION_WIDTHS = (POOL_WIDTH, POOL_WIDTH, HGRN_WIDTH, HGRN_WIDTH, HGRN_WIDTH, HGRN_WIDTH)
SECTION_STARTS = tuple(int(v) for v in np.cumsum((0,) + SECTION_WIDTHS))
assert SECTION_STARTS[-1] == IN_WIDTH
SEC_POOL_U, SEC_POOL_GATE, SEC_Q, SEC_FORGET, SEC_V, SEC_OUT_GATE = range(len(SECTION_WIDTHS))
EPS = 1e-6

SEQ_TILE = 512
POOL_TAIL = 16
assert POOL_WINDOWS == tuple(2 << g for g in range(len(POOL_WINDOWS))) and POOL_TAIL >= POOL_WINDOWS[-1]
NUM_LEVELS = 6
SPLIT = 2
VMEM_LIMIT_BYTES = 62 * 1024 * 1024
SUBLANES = 8
LOG2E = 1.4426950408889634
MID_ROW = CHUNK // 2 - 1
DIRECT_MAX_LOG2 = 110.0
SLOTS = 2

F32 = jnp.float32
BF16 = jnp.bfloat16


def _ref_row(level, row):
    m = 1 << (level - 1)
    return (row // (2 * m)) * (2 * m) + m - 1


def _cumsum_matrix():
    t = np.arange(CHUNK)[:, None]
    j = np.arange(CHUNK)[None, :]
    w = (j <= t).astype(np.float32)
    return np.concatenate([w] * SPLIT, axis=1)


def _level_map():
    t = np.arange(CHUNK)[:, None]
    s = np.arange(CHUNK)[None, :]
    x = t ^ s
    lvl = np.where(x > 0, np.floor(np.log2(np.maximum(x, 1))).astype(np.int32) + 1, 0)
    return np.where(s > t, -1, lvl).astype(np.int32)


def _split_bf16(a, pieces):
    out = []
    rest = a
    for _ in range(pieces):
        p = rest.astype(BF16)
        out.append(p)
        rest = rest - p.astype(F32)
    return out


def _sigmoid(a):
    return 1.0 / (1.0 + jnp.exp(-a))


def _twice_silu(a):
    return a * (jnp.tanh(0.5 * a) + 1.0)


PACK_ROWS, PACK_COLS = 1024, 1024


def _pack_rows_kernel(w_ref, out_ref):
    out_ref[0] = pltpu.bitcast(w_ref[0].astype(BF16), jnp.uint32)


def _pack_rows(w):
    layers, k, n = w.shape
    rows, cols = min(PACK_ROWS, k), min(PACK_COLS, n)
    return pl.pallas_call(
        _pack_rows_kernel,
        grid=(layers, k // rows, n // cols),
        in_specs=[pl.BlockSpec((1, rows, cols), lambda l, i, j: (l, i, j))],
        out_specs=pl.BlockSpec((1, rows // 2, cols), lambda l, i, j: (l, i, j)),
        out_shape=jax.ShapeDtypeStruct((layers, k // 2, n), jnp.uint32),
        name="pack_rows",
    )(w.astype(F32))


def _unpack_rows(words):
    return pltpu.bitcast(words, BF16)


def _dot(a, b):
    return jnp.dot(a, b, preferred_element_type=F32)


def _dot_nt(a, b):
    return lax.dot_general(a, b, (((1,), (1,)), ((), ())), preferred_element_type=F32)


def _lower_bounds_kernel(lb_ref, out_ref):
    depth = lb_ref.shape[0]
    rows = [lb_ref[l:l + 1, :] for l in range(depth)]
    mx = rows[0]
    for r in rows[1:]:
        mx = jnp.maximum(mx, r)
    ex = [jnp.exp(r - mx) for r in rows]
    den = ex[0]
    for e in ex[1:]:
        den = den + e
    sm = [e / den for e in ex]
    run = sm[0]
    out_ref[0:1, :] = run - sm[0]
    for l in range(1, depth):
        run = run + sm[l]
        out_ref[l:l + 1, :] = run - sm[0]


def _lower_bounds_call(hgrn_lower_bounds):
    return pl.pallas_call(
        _lower_bounds_kernel,
        out_shape=jax.ShapeDtypeStruct(hgrn_lower_bounds.shape, F32),
        name="lower_bounds",
    )(hgrn_lower_bounds.astype(F32))


def _adaln_kernel(c_ref, w_ref, b_ref, mod_ref):
    c = c_ref[...]
    sc = c * _sigmoid(c)
    w = w_ref[0]
    a_hi, a_lo = _split_bf16(sc, 2)
    w_hi, w_lo = _split_bf16(w, 2)
    acc = _dot(a_hi, w_hi) + _dot(a_hi, w_lo) + _dot(a_lo, w_hi)
    mod_ref[0] = acc + b_ref[0]


def _adaln_call(c, ada_w, ada_b):
    depth, d, d3 = ada_w.shape
    batch = c.shape[0]
    n_col = d3 // d
    return pl.pallas_call(
        _adaln_kernel,
        grid=(depth, n_col),
        in_specs=[
            pl.BlockSpec((batch, d), lambda l, j: (0, 0)),
            pl.BlockSpec((1, d, d), lambda l, j: (l, 0, j)),
            pl.BlockSpec((1, 1, d), lambda l, j: (l, 0, j)),
        ],
        out_specs=pl.BlockSpec((1, batch, d), lambda l, j: (l, 0, j)),
        out_shape=jax.ShapeDtypeStruct((depth, batch, d3), F32),
        name="adaln",
    )(c.astype(F32), ada_w.astype(F32), ada_b.astype(F32).reshape(depth, 1, d3))


def _layer_kernel(tiles_per_seq, n_tiles,
                  xf_ref, xb_ref, modf_ref, modb_ref, npre_ref, w_in_ref, pool_w_ref,
                  pool_scale_ref, lb_ref, hnw_ref, w_out_ref, npost_ref, wsum_ref, lvl_ref,
                  out_ref,
                  state_ref, tail_ref, q_ref, k_ref, v_ref, g_ref, bsum_ref, mixp_ref, flag_ref,
                  mixh_ref, attn_ref, b_ref, ql_ref, kl_ref):
    ts = SEQ_TILE
    n_chunks = ts // CHUNK
    step = pl.program_id(0)
    front_tile = jnp.minimum(step, n_tiles - 1)
    back_tile = jnp.maximum(step - 1, 0)
    slot_f = step % SLOTS
    slot_b = (step + 1) % SLOTS

    @pl.when(step == 0)
    def _():
        for ref in (q_ref, k_ref, v_ref, g_ref, bsum_ref, mixp_ref):
            ref[...] = jnp.zeros_like(ref)
        flag_ref[1] = 1

    @pl.when(front_tile % tiles_per_seq == 0)
    def _():
        tail_ref[...] = jnp.zeros_like(tail_ref)

    @pl.when(back_tile % tiles_per_seq == 0)
    def _():
        state_ref[...] = jnp.zeros_like(state_ref)

    lvl = lvl_ref[...]
    causal = lvl >= 0
    half_hnw = 0.5 * hnw_ref[...]
    heads = [slice(hd * HEAD_DIM, (hd + 1) * HEAD_DIM) for hd in range(HGRN_HEADS)]

    def front():
        x = xf_ref[0]
        mod = modf_ref[0]
        shift = mod[:, 0:D_MODEL]
        scale = mod[:, D_MODEL:2 * D_MODEL]
        ms = jnp.mean(x * x, axis=-1, keepdims=True)
        row_gain = npre_ref[...] * (1.0 + scale)
        h = ((x * lax.rsqrt(ms + EPS)) * row_gain + shift).astype(BF16)

        def in_proj(section):
            return _dot(h, _unpack_rows(w_in_ref[:, SECTION_STARTS[section]:SECTION_STARTS[section + 1]]))

        u = in_proj(SEC_POOL_U)
        part = jnp.concatenate([tail_ref[...], u], axis=0)
        tail_ref[...] = u[ts - POOL_TAIL:ts, :]
        win_sum = []
        for g, w in enumerate(POOL_WINDOWS):
            part = part[:, (POOL_GROUP_WIDTH if g else 0):]
            part = part + pltpu.roll(part, w // 2, 0)
            win_sum.append(part[POOL_TAIL:, 0:POOL_GROUP_WIDTH])
        g_pool = in_proj(SEC_POOL_GATE)
        pos = (front_tile % tiles_per_seq) * ts + lax.broadcasted_iota(jnp.int32, (ts, 1), 0)
        for g, w in enumerate(POOL_WINDOWS):
            lo, hi = g * POOL_GROUP_WIDTH, (g + 1) * POOL_GROUP_WIDTH
            u_g = u[:, lo:hi]
            inv_cnt = 1.0 / jnp.minimum(pos + 1, w).astype(F32)
            d = (win_sum[g] * inv_cnt - u_g).astype(BF16)
            y = _dot(d, pool_w_ref[g]) * (0.5 * pool_scale_ref[:, lo:hi])
            gp = g_pool[:, lo:hi]
            mixp_ref[slot_f, :, lo:hi] = (y * _twice_silu(gp)).astype(BF16)

        q_ref[slot_f] = in_proj(SEC_Q)
        lb = lb_ref[...]
        f = lb + (1.0 - lb) * _sigmoid(in_proj(SEC_FORGET))
        k_ref[slot_f] = 1.0 - f
        lf = jnp.log(f) * LOG2E
        v_ref[slot_f] = in_proj(SEC_V)
        g_ref[slot_f] = in_proj(SEC_OUT_GATE)
        unsafe = None
        for c in range(n_chunks):
            lf3 = jnp.concatenate(_split_bf16(lf[c * CHUNK:(c + 1) * CHUNK, :], SPLIT), axis=0)
            b_c = _dot(wsum_ref[...], lf3)
            bsum_ref[slot_f, c * CHUNK:(c + 1) * CHUNK, :] = b_c
            b_mid = b_c[MID_ROW:MID_ROW + 1, :]
            b_last = b_c[CHUNK - 1:CHUNK, :]
            ok = jnp.logical_and(-b_mid <= DIRECT_MAX_LOG2, b_mid - b_last <= DIRECT_MAX_LOG2)
            bad = jnp.where(ok, 0.0, 1.0)
            unsafe = bad if unsafe is None else jnp.maximum(unsafe, bad)
        flag_ref[slot_f] = (jnp.max(unsafe) < 0.5).astype(jnp.int32)

    def chunk_outputs(rows, qe, k_dec, e_last, attn):
        k_dec_b = k_dec.astype(BF16)
        for hd, hl in enumerate(heads):
            v_t = jnp.transpose(v_ref[slot_b, rows, hl]).astype(BF16)
            state_t = state_ref[hd]
            o = _dot_nt(jnp.concatenate([qe[:, hl], attn[hd]], axis=1),
                        jnp.concatenate([state_t.astype(BF16), v_t], axis=1))
            on = o * lax.rsqrt(jnp.mean(o * o, axis=-1, keepdims=True) + EPS) * half_hnw
            gh = g_ref[slot_b, rows, hl]
            mixh_ref[rows, hl] = (on * _twice_silu(gh)).astype(BF16)
            state_ref[hd] = state_t * e_last[:, hl] + _dot(v_t, k_dec_b[:, hl])

    def direct_chunk(c):
        rows = slice(c * CHUNK, (c + 1) * CHUNK)
        b_c = bsum_ref[slot_b, rows, :]
        b_mid = bsum_ref[slot_b, c * CHUNK + MID_ROW:c * CHUNK + MID_ROW + 1, :]
        b_last = bsum_ref[slot_b, (c + 1) * CHUNK - 1:(c + 1) * CHUNK, :]
        d = b_c - b_mid
        q_s = q_ref[slot_b, rows, :] * jnp.exp2(d)
        k_s = k_ref[slot_b, rows, :] * jnp.exp2(-d)
        q_b = q_s.astype(BF16)
        k_b = k_s.astype(BF16)
        attn = [jnp.where(causal, _dot_nt(q_b[:, hl], k_b[:, hl]), 0.0).astype(BF16) for hl in heads]
        chunk_outputs(rows, (q_s * jnp.exp2(b_mid)).astype(BF16),
                      k_s * jnp.exp2(b_last - b_mid), jnp.exp2(b_last), attn)

    def robust_chunk(c, carry):
        rows = pl.ds(pl.multiple_of(c * CHUNK, CHUNK), CHUNK)
        b_ref[...] = bsum_ref[slot_b, rows, :]
        b_c = b_ref[...]
        q_c = q_ref[slot_b, rows, :]
        k_c = k_ref[slot_b, rows, :]
        ql_ref[0:CHUNK, :] = q_c.astype(BF16)
        kl_ref[0] = k_c.astype(BF16)
        ql_ref[CHUNK:2 * CHUNK, :] = (q_c * (1.0 - k_c)).astype(BF16)
        row_in_group = lax.broadcasted_iota(jnp.int32, (SUBLANES, 1), 0)
        for level in range(2, NUM_LEVELS + 1):
            m = 1 << (level - 1)
            refs = []
            for r in range(0, CHUNK, SUBLANES):
                if 2 * m >= SUBLANES:
                    refs.append(jnp.broadcast_to(b_ref[_ref_row(level, r):_ref_row(level, r) + 1, :],
                                                 (SUBLANES, HGRN_WIDTH)))
                else:
                    lo_ref = b_ref[_ref_row(level, r):_ref_row(level, r) + 1, :]
                    hi_ref = b_ref[_ref_row(level, r + 2 * m):_ref_row(level, r + 2 * m) + 1, :]
                    refs.append(jnp.where(row_in_group < 2 * m, lo_ref, hi_ref))
            e_l = jnp.exp2(-jnp.abs(b_c - jnp.concatenate(refs, axis=0)))
            ql_ref[level * CHUNK:(level + 1) * CHUNK, :] = (q_c * e_l).astype(BF16)
            kl_ref[level] = (k_c * e_l).astype(BF16)
        for hd, hl in enumerate(heads):
            a01 = _dot_nt(ql_ref[0:2 * CHUNK, hl], kl_ref[0, :, hl])
            attn = jnp.where(lvl == 0, a01[0:CHUNK], 0.0)
            attn = jnp.where(lvl == 1, a01[CHUNK:2 * CHUNK], attn)
            for level in range(2, NUM_LEVELS + 1):
                a_l = _dot_nt(ql_ref[level * CHUNK:(level + 1) * CHUNK, hl], kl_ref[level, :, hl])
                attn = jnp.where(lvl == level, a_l, attn)
            attn_ref[hd] = attn.astype(BF16)
        b_last = b_ref[CHUNK - 1:CHUNK, :]
        chunk_outputs(rows, (q_c * jnp.exp2(b_c)).astype(BF16),
                      k_c * jnp.exp2(b_last - b_c), jnp.exp2(b_last),
                      [attn_ref[hd] for hd in range(HGRN_HEADS)])
        return carry

    def back_epilogue():
        gate = modb_ref[0][:, 2 * D_MODEL:3 * D_MODEL]
        mix = jnp.concatenate([mixp_ref[slot_b], mixh_ref[...]], axis=1)
        y = _dot(mix, _unpack_rows(w_out_ref[...]))
        row_gain = gate * npost_ref[...]
        out_ref[0] = xb_ref[0] + (y * lax.rsqrt(jnp.mean(y * y, axis=-1, keepdims=True) + EPS)) * row_gain

    use_direct = flag_ref[slot_b] == 1

    @pl.when(use_direct)
    def _():
        front()
        for c in range(n_chunks):
            direct_chunk(c)
        back_epilogue()

    @pl.when(jnp.logical_not(use_direct))
    def _():
        front()
        lax.fori_loop(0, n_chunks, robust_chunk, 0)
        back_epilogue()


def _layer_call(layer, x, mod, npre, w_in, pool_w, pool_scale, lb, hnw, w_out, npost, wsum, lvl):
    batch, seq, d = x.shape
    ts = SEQ_TILE
    tiles_per_seq = seq // ts
    n_tiles = batch * tiles_per_seq

    def front_tile(g):
        t = jnp.minimum(g, n_tiles - 1)
        return t // tiles_per_seq, t % tiles_per_seq

    def back_tile(g):
        t = jnp.maximum(g - 1, 0)
        return t // tiles_per_seq, t % tiles_per_seq

    const2 = lambda g: (0, 0)
    of_layer3 = lambda g: (layer, 0, 0)
    of_layer4 = lambda g: (layer, 0, 0, 0)
    once = pl.Buffered(1)
    return pl.pallas_call(
        functools.partial(_layer_kernel, tiles_per_seq, n_tiles),
        grid=(n_tiles + 1,),
        in_specs=[
            pl.BlockSpec((1, ts, d), lambda g: (*front_tile(g), 0)),
            pl.BlockSpec((1, ts, d), lambda g: (*back_tile(g), 0)),
            pl.BlockSpec((None, 1, 1, 3 * d), lambda g: (layer, front_tile(g)[0], 0, 0)),
            pl.BlockSpec((None, 1, 1, 3 * d), lambda g: (layer, back_tile(g)[0], 0, 0)),
            pl.BlockSpec((None, 1, d), of_layer3),
            pl.BlockSpec((None, d // 2, IN_WIDTH), of_layer3, pipeline_mode=once),
            pl.BlockSpec((None,) + pool_w.shape[1:], of_layer4, pipeline_mode=once),
            pl.BlockSpec((None, 1, POOL_WIDTH), of_layer3),
            pl.BlockSpec((None, 1, HGRN_WIDTH), of_layer3),
            pl.BlockSpec((None, 1, HEAD_DIM), of_layer3),
            pl.BlockSpec((None, MIX_WIDTH // 2, d), of_layer3, pipeline_mode=once),
            pl.BlockSpec((None, 1, d), of_layer3),
            pl.BlockSpec(wsum.shape, const2),
            pl.BlockSpec(lvl.shape, const2),
        ],
        out_specs=pl.BlockSpec((1, ts, d), lambda g: (*back_tile(g), 0)),
        out_shape=jax.ShapeDtypeStruct(x.shape, F32),
        scratch_shapes=[
            pltpu.VMEM((HGRN_HEADS, HEAD_DIM, HEAD_DIM), F32),
            pltpu.VMEM((POOL_TAIL, POOL_WIDTH), F32),
            pltpu.VMEM((SLOTS, ts, HGRN_WIDTH), F32),
            pltpu.VMEM((SLOTS, ts, HGRN_WIDTH), F32),
            pltpu.VMEM((SLOTS, ts, HGRN_WIDTH), F32),
            pltpu.VMEM((SLOTS, ts, HGRN_WIDTH), F32),
            pltpu.VMEM((SLOTS, ts, HGRN_WIDTH), F32),
            pltpu.VMEM((SLOTS, ts, POOL_WIDTH), BF16),
            pltpu.SMEM((SLOTS,), jnp.int32),
            pltpu.VMEM((ts, HGRN_WIDTH), BF16),
            pltpu.VMEM((HGRN_HEADS, CHUNK, CHUNK), BF16),
            pltpu.VMEM((CHUNK, HGRN_WIDTH), F32),
            pltpu.VMEM(((NUM_LEVELS + 1) * CHUNK, HGRN_WIDTH), BF16),
            pltpu.VMEM((NUM_LEVELS + 1, CHUNK, HGRN_WIDTH), BF16),
        ],
        compiler_params=pltpu.CompilerParams(
            dimension_semantics=("arbitrary",),
            vmem_limit_bytes=VMEM_LIMIT_BYTES),
        name="hybrid_layer",
    )(x, x, mod, mod, npre, w_in, pool_w, pool_scale, lb, hnw, w_out, npost, wsum, lvl)


def kernel(x, c, norm_pre_w, ada_w, ada_b, w_in, pool_w, pool_scale, hgrn_lower_bounds,
           hgrn_norm_w, w_out, norm_post_w):
    depth = ada_w.shape[0]
    batch = x.shape[0]
    lower = _lower_bounds_call(hgrn_lower_bounds)
    mod = _adaln_call(c, ada_w, ada_b)
    wsum = jnp.asarray(_cumsum_matrix(), dtype=BF16)
    lvl = jnp.asarray(_level_map())
    stacked = (
        mod.reshape(depth, batch, 1, 3 * D_MODEL),
        norm_pre_w.astype(F32).reshape(depth, 1, D_MODEL),
        _pack_rows(w_in),
        pool_w.astype(BF16),
        pool_scale.astype(F32).reshape(depth, 1, POOL_WIDTH),
        lower.reshape(depth, 1, HGRN_WIDTH),
        hgrn_norm_w.astype(F32).reshape(depth, 1, HEAD_DIM),
        _pack_rows(w_out),
        norm_post_w.astype(F32).reshape(depth, 1, D_MODEL),
    )
    h = x.astype(F32)
    for l in range(depth):
        h = _layer_call(l, h, *stacked, wsum, lvl)
    return h.astype(x.dtype)
```
